```python
import jax, jax.numpy as jnp
from jax import lax
import numpy as np

D_MODEL = 1024
BATCH = 8
SEQ = 4096
DEPTH = 4

CTX_LEN = 256
GRID_W = 64
N_MIXERS = 2
N_A_LAYERS = (DEPTH + 1) // 2
N_B_LAYERS = DEPTH // 2
HEAD_DIM = 64
A_Q_HEADS = D_MODEL // HEAD_DIM
A_KV_HEADS = 4
A_QW = A_Q_HEADS * HEAD_DIM
A_KW = A_KV_HEADS * HEAD_DIM
WINDOW = 128
BLOCK = 128
B_HEADS = 16
B_NOPE = 64
B_ROPE = 32
B_V = 64
Q_LORA = 512
KV_LORA = 256
D_FF = 2816
CONV_W = 3
ROPE_BASE = 10000.0
EPS = 1e-6
NEG = -1e30

kernel_name = "hybrid_swa_sink_mla_convglu_prefix_ctx"


def rmsnorm(x, g):
    xf = x.astype(jnp.float32)
    y = xf * lax.rsqrt(jnp.mean(xf * xf, axis=-1, keepdims=True) + EPS)
    return (y * g.astype(jnp.float32)).astype(x.dtype)


def modulate(h, shift, scale):
    return h * (1.0 + scale) + shift


def axial_angles(rows, rot_dim):
    row = jnp.repeat(jnp.arange(rows), GRID_W).astype(jnp.float32)
    col = jnp.tile(jnp.arange(GRID_W), rows).astype(jnp.float32)
    n_freq = rot_dim // 4
    inv = ROPE_BASE ** (-jnp.arange(n_freq, dtype=jnp.float32) / n_freq)
    return jnp.concatenate([row[:, None] * inv, col[:, None] * inv], axis=-1)


def apply_rope(x, ang):
    ang = ang.reshape((ang.shape[0],) + (1,) * (x.ndim - 3) + (ang.shape[-1],))
    cos = jnp.cos(ang).astype(x.dtype)
    sin = jnp.sin(ang).astype(x.dtype)
    x1 = x[..., 0::2]
    x2 = x[..., 1::2]
    return jnp.stack([x1 * cos - x2 * sin, x1 * sin + x2 * cos], axis=-1).reshape(x.shape)


def dwconv_centred(u, w, b):
    up = jnp.pad(u, ((0, 0), (1, 1), (0, 0)))
    return up[:, :-2] * w[0] + up[:, 1:-1] * w[1] + up[:, 2:] * w[2] + b


def conv_glu(h, w_in, conv_w, conv_b, w_out):
    ab = h @ w_in
    a, v = ab[..., :D_FF], ab[..., D_FF:]
    a = dwconv_centred(a, conv_w, conv_b)
    return (jax.nn.silu(a) * v) @ w_out


def window_gqa(h_lat, h_ctx, wqkv, wo, sink, ang, with_ctx_out):
    B, S, _ = h_lat.shape
    L = h_ctx.shape[1]
    G = A_Q_HEADS // A_KV_HEADS
    scale = HEAD_DIM ** -0.5

    def proj(h):
        n = h.shape[1]
        qkv = h @ wqkv
        q = qkv[..., :A_QW].reshape(B, n, A_KV_HEADS, G, HEAD_DIM)
        k = qkv[..., A_QW:A_QW + A_KW].reshape(B, n, A_KV_HEADS, HEAD_DIM)
        v = qkv[..., A_QW + A_KW:].reshape(B, n, A_KV_HEADS, HEAD_DIM)
        return q, k, v

    q_lat, k_lat, v_lat = proj(h_lat)
    q_lat = apply_rope(q_lat, ang) * scale
    k_lat = apply_rope(k_lat, ang)
    q_ctx, k_ctx, v_ctx = proj(h_ctx)
    sink_f = sink.astype(jnp.float32).reshape(A_KV_HEADS, G)[None, :, :, None, None]

    nb = S // BLOCK
    kv_len = BLOCK + 2 * WINDOW
    kp = jnp.pad(k_lat, ((0, 0), (WINDOW, WINDOW), (0, 0), (0, 0)))
    vp = jnp.pad(v_lat, ((0, 0), (WINDOW, WINDOW), (0, 0), (0, 0)))

    def block(i):
        q0 = i * BLOCK
        qb = lax.dynamic_slice_in_dim(q_lat, q0, BLOCK, axis=1)
        kb = lax.dynamic_slice_in_dim(kp, q0, kv_len, axis=1)
        vb = lax.dynamic_slice_in_dim(vp, q0, kv_len, axis=1)
        s_win = jnp.einsum('bqkgd,bskd->bkgqs', qb, kb).astype(jnp.float32)
        qpos = q0 + jnp.arange(BLOCK)
        kpos = q0 - WINDOW + jnp.arange(kv_len)
        valid = (jnp.abs(qpos[:, None] - kpos[None, :]) <= WINDOW) & (kpos >= 0) & (kpos < S)
        s_win = jnp.where(valid, s_win, NEG)
        s_ctx = jnp.einsum('bqkgd,bskd->bkgqs', qb, k_ctx).astype(jnp.float32)
        s_snk = jnp.broadcast_to(sink_f, s_win.shape[:-1] + (1,))
        p = jax.nn.softmax(jnp.concatenate([s_win, s_ctx, s_snk], axis=-1), axis=-1)
        o = (jnp.einsum('bkgqs,bskd->bqkgd', p[..., :kv_len].astype(vb.dtype), vb)
             + jnp.einsum('bkgqs,bskd->bqkgd', p[..., kv_len:kv_len + L].astype(v_ctx.dtype), v_ctx))
        return o.reshape(B, BLOCK, A_QW)

    o_lat = lax.map(block, jnp.arange(nb))
    o_lat = o_lat.transpose(1, 0, 2, 3).reshape(B, S, A_QW)
    out_lat = o_lat @ wo
    if not with_ctx_out:
        return out_lat, None
    s = jnp.einsum('bqkgd,bskd->bkgqs', q_ctx * scale, k_ctx).astype(jnp.float32)
    s_snk = jnp.broadcast_to(sink_f, s.shape[:-1] + (1,))
    p = jax.nn.softmax(jnp.concatenate([s, s_snk], axis=-1), axis=-1)[..., :-1]
    o_ctx = jnp.einsum('bkgqs,bskd->bqkgd', p.astype(v_ctx.dtype), v_ctx).reshape(B, L, A_QW)
    return out_lat, o_ctx @ wo


def mla(h_lat, h_ctx, wdown, qnorm_g, wuq, kvnorm_g, wuk, wuv, wo, ang, with_ctx_out):
    B, S, _ = h_lat.shape
    L = h_ctx.shape[1]
    scale = (B_NOPE + B_ROPE) ** -0.5

    def proj(h, ang_h):
        n = h.shape[1]
        d = h @ wdown
        cq = rmsnorm(d[..., :Q_LORA], qnorm_g)
        ckv = rmsnorm(d[..., Q_LORA:Q_LORA + KV_LORA], kvnorm_g)
        k_rope = d[..., Q_LORA + KV_LORA:][:, :, None, :]
        q = (cq @ wuq).reshape(B, n, B_HEADS, B_NOPE + B_ROPE)
        q_nope, q_rope = q[..., :B_NOPE], q[..., B_NOPE:]
        if ang_h is not None:
            q_rope = apply_rope(q_rope, ang_h)
            k_rope = apply_rope(k_rope, ang_h)
        k_nope = (ckv @ wuk).reshape(B, n, B_HEADS, B_NOPE)
        v = (ckv @ wuv).reshape(B, n, B_HEADS, B_V)
        q = jnp.concatenate([q_nope, q_rope], axis=-1) * scale
        k = jnp.concatenate([k_nope, jnp.broadcast_to(k_rope, (B, n, B_HEADS, B_ROPE))], axis=-1)
        return q, k, v

    q_lat, k_lat, v_lat = proj(h_lat, ang)
    q_ctx, k_ctx, v_ctx = proj(h_ctx, None)
    k_all = jnp.concatenate([k_ctx, k_lat], axis=1)
    v_all = jnp.concatenate([v_ctx, v_lat], axis=1)

    def block(i):
        qb = lax.dynamic_slice_in_dim(q_lat, i * BLOCK, BLOCK, axis=1)
        s = jnp.einsum('bqhd,bkhd->bhqk', qb, k_all).astype(jnp.float32)
        p = jax.nn.softmax(s, axis=-1)
        return jnp.einsum('bhqk,bkhd->bqhd', p.astype(v_all.dtype), v_all).reshape(B, BLOCK, B_HEADS * B_V)

    o_lat = lax.map(block, jnp.arange(S // BLOCK))
    o_lat = o_lat.transpose(1, 0, 2, 3).reshape(B, S, B_HEADS * B_V)
    out_lat = o_lat @ wo
    if not with_ctx_out:
        return out_lat, None
    s = jnp.einsum('bqhd,bkhd->bhqk', q_ctx, k_ctx).astype(jnp.float32)
    p = jax.nn.softmax(s, axis=-1)
    o_ctx = jnp.einsum('bhqk,bkhd->bqhd', p.astype(v_ctx.dtype), v_ctx).reshape(B, L, B_HEADS * B_V)
    return out_lat, o_ctx @ wo


def setup_inputs(seed: int = 0) -> dict:
    key = jax.random.key(seed)
    ks = jax.random.split(key, 24)
    f32 = jnp.float32

    def nrm(k, shape, fan_in):
        return jax.random.normal(k, shape, f32) * (fan_in ** -0.5)

    def gain(k, shape):
        return 1.0 + 0.02 * jax.random.normal(k, shape, f32)

    def bias(k, shape):
        return 0.02 * jax.random.normal(k, shape, f32)

    return {
        "x": jax.random.normal(ks[0], (BATCH, SEQ, D_MODEL), f32),
        "c": jax.random.normal(ks[1], (BATCH, D_MODEL), f32),
        "ctx": jax.random.normal(ks[2], (BATCH, CTX_LEN, D_MODEL), f32),
        "c_ctx": jax.random.normal(ks[3], (D_MODEL,), f32),
        "mod_w": nrm(ks[4], (DEPTH, D_MODEL, 6 * D_MODEL), D_MODEL),
        "mod_b": bias(ks[5], (DEPTH, 6 * D_MODEL)),
        "norm1_g": gain(ks[6], (DEPTH, D_MODEL)),
        "norm2_g": gain(ks[7], (DEPTH, D_MODEL)),
        "a_wqkv": nrm(ks[8], (N_A_LAYERS, D_MODEL, A_QW + 2 * A_KW), D_MODEL),
        "a_wo": nrm(ks[9], (N_A_LAYERS, A_QW, D_MODEL), A_QW),
        "a_sink": jax.random.normal(ks[10], (N_A_LAYERS, A_Q_HEADS), f32),
        "b_wdown": nrm(ks[11], (N_B_LAYERS, D_MODEL, Q_LORA + KV_LORA + B_ROPE), D_MODEL),
        "b_qnorm_g": gain(ks[12], (N_B_LAYERS, Q_LORA)),
        "b_wuq": nrm(ks[13], (N_B_LAYERS, Q_LORA, B_HEADS * (B_NOPE + B_ROPE)), Q_LORA),
        "b_kvnorm_g": gain(ks[14], (N_B_LAYERS, KV_LORA)),
        "b_wuk": nrm(ks[15], (N_B_LAYERS, KV_LORA, B_HEADS * B_NOPE), KV_LORA),
        "b_wuv": nrm(ks[16], (N_B_LAYERS, KV_LORA, B_HEADS * B_V), KV_LORA),
        "b_wo": nrm(ks[17], (N_B_LAYERS, B_HEADS * B_V, D_MODEL), B_HEADS * B_V),
        "f_win": nrm(ks[18], (DEPTH, D_MODEL, 2 * D_FF), D_MODEL),
        "f_conv_w": nrm(ks[19], (DEPTH, CONV_W, D_FF), CONV_W),
        "f_conv_b": bias(ks[20], (DEPTH, D_FF)),
        "f_wout": nrm(ks[21], (DEPTH, D_FF, D_MODEL), D_FF),
        "final_g": gain(ks[22], (D_MODEL,)),
    }


def reference(x, c, ctx, c_ctx, mod_w, mod_b, norm1_g, norm2_g, a_wqkv, a_wo, a_sink,
              b_wdown, b_qnorm_g, b_wuq, b_kvnorm_g, b_wuk, b_wuv, b_wo,
              f_win, f_conv_w, f_conv_b, f_wout, final_g):
    rows = x.shape[1] // GRID_W
    ang_a = axial_angles(rows, HEAD_DIM)
    ang_b = axial_angles(rows, B_ROPE)
    y = ctx
    silu_c = jax.nn.silu(c)
    silu_cc = jax.nn.silu(c_ctx)
    for i in range(DEPTH):
        with_ctx = i < DEPTH - 1
        mod_l = (silu_c @ mod_w[i] + mod_b[i])[:, None, :]
        mod_c = silu_cc @ mod_w[i] + mod_b[i]
        sh1, sc1, g1, sh2, sc2, g2 = jnp.split(mod_l, 6, axis=-1)
        csh1, csc1, cg1, csh2, csc2, cg2 = jnp.split(mod_c, 6, axis=-1)
        h_lat = modulate(rmsnorm(x, norm1_g[i]), sh1, sc1)
        h_ctx = modulate(rmsnorm(y, norm1_g[i]), csh1, csc1)
        j = i // N_MIXERS
        if i % N_MIXERS == 0:
            o_lat, o_ctx = window_gqa(h_lat, h_ctx, a_wqkv[j], a_wo[j], a_sink[j], ang_a, with_ctx)
        else:
            o_lat, o_ctx = mla(h_lat, h_ctx, b_wdown[j], b_qnorm_g[j], b_wuq[j], b_kvnorm_g[j],
                               b_wuk[j], b_wuv[j], b_wo[j], ang_b, with_ctx)
        x = x + g1 * o_lat
        h2 = modulate(rmsnorm(x, norm2_g[i]), sh2, sc2)
        x = x + g2 * conv_glu(h2, f_win[i], f_conv_w[i], f_conv_b[i], f_wout[i])
        if with_ctx:
            y = y + cg1 * o_ctx
            hc2 = modulate(rmsnorm(y, norm2_g[i]), csh2, csc2)
            y = y + cg2 * conv_glu(hc2, f_win[i], f_conv_w[i], f_conv_b[i], f_wout[i])
    return rmsnorm(x, final_g)
```

```python
import functools

import jax
import jax.numpy as jnp
from jax import lax
from jax.experimental import pallas as pl
from jax.experimental.pallas import tpu as pltpu

GRID_W = 64
HEAD_DIM = 64
A_KV_HEADS = 4
WINDOW = 128
B_HEADS = 16
B_NOPE = 64
B_ROPE = 32
B_V = 64
ROPE_BASE = 10000.0
EPS = 1e-6
NEG = -1e30

LANES = 128
TM = 256
HALO = 16
TK_MLA = 512
FF_CHUNK = 768
VMEM_LIMIT = 56 * 1024 * 1024

F32 = jnp.float32
BF16 = jnp.bfloat16
_NT = (((1,), (1,)), ((), ()))


def _dot(a, b):
    return jnp.dot(a, b, preferred_element_type=F32)


def _dot_nt(a, b):
    return lax.dot_general(a, b, _NT, preferred_element_type=F32)


def _rms(xf, g):
    ms = jnp.mean(xf * xf, axis=-1, keepdims=True)
    return xf * lax.rsqrt(ms + EPS) * g


def _norm_mod(xf, g, shift, scale):
    return _rms(xf, g) * (1.0 + scale) + shift


def _params(n_axes):
    return pltpu.CompilerParams(dimension_semantics=("parallel",) * n_axes,
                                vmem_limit_bytes=VMEM_LIMIT)


def _resident(shape):
    nd = len(shape)
    return pl.BlockSpec(shape, lambda *_: (0,) * nd, pipeline_mode=pl.Buffered(1))


def _mod_kernel(c_ref, w_ref, b_ref, o_ref):
    c = c_ref[...]
    silu = c * (1.0 / (1.0 + jnp.exp(-c)))
    o_ref[0] = _dot(silu.astype(BF16), w_ref[0].astype(BF16)) + b_ref[0]


def _modulation(cond, mod_w, mod_b):
    depth, d, d6 = mod_w.shape
    rows = cond.shape[0]
    return pl.pallas_call(
        _mod_kernel,
        grid=(depth, d6 // d),
        in_specs=[pl.BlockSpec((rows, d), lambda i, j: (0, 0)),
                  pl.BlockSpec((1, d, d), lambda i, j: (i, 0, j)),
                  pl.BlockSpec((1, 1, d), lambda i, j: (i, 0, j))],
        out_specs=pl.BlockSpec((1, rows, d), lambda i, j: (i, 0, j)),
        out_shape=jax.ShapeDtypeStruct((depth, rows, d6), F32),
        compiler_params=_params(2),
        name="modulation",
    )(cond, mod_w, mod_b.reshape(depth, 1, d6))


def _rope_block(xb, cos, sin_signed, first_half, half):
    swapped = jnp.where(first_half, pltpu.roll(xb, LANES - half, 1), pltpu.roll(xb, half, 1))
    return xb * cos + swapped * sin_signed


def _proj_a_kernel(x_ref, mod_ref, g_ref, w_ref, tab_ref, q_ref, k_ref, v_ref, *, qw, kw):
    m = mod_ref[0, 0]
    h = _norm_mod(x_ref[0], g_ref[...], m[0:1], m[1:2]).astype(BF16)
    qkv = _dot(h, w_ref[...])
    lane = lax.broadcasted_iota(jnp.int32, (TM, LANES), 1)
    first = (lane % HEAD_DIM) < (HEAD_DIM // 2)
    cq, sq, ck, sk = tab_ref[0], tab_ref[1], tab_ref[2], tab_ref[3]
    for j in range(qw // LANES):
        blk = qkv[:, j * LANES:(j + 1) * LANES]
        q_ref[0, :, j * LANES:(j + 1) * LANES] = _rope_block(blk, cq, sq, first, HEAD_DIM // 2).astype(BF16)
    for j in range(kw // LANES):
        blk = qkv[:, qw + j * LANES:qw + (j + 1) * LANES]
        k_ref[0, :, j * LANES:(j + 1) * LANES] = _rope_block(blk, ck, sk, first, HEAD_DIM // 2).astype(BF16)
    v_ref[0] = qkv[:, qw + kw:].astype(BF16)


def _proj_a(xs, modl, g, w, tab, qw, kw):
    b, s_tot, d = xs.shape
    nt = s_tot // TM
    out = lambda n: jax.ShapeDtypeStruct((b, s_tot, n), BF16)
    ospec = lambda n: pl.BlockSpec((1, TM, n), lambda i, t: (i, t, 0))
    return pl.pallas_call(
        functools.partial(_proj_a_kernel, qw=qw, kw=kw),
        grid=(b, nt),
        in_specs=[pl.BlockSpec((1, TM, d), lambda i, t: (i, t, 0)),
                  pl.BlockSpec((1, 1, 6, d), lambda i, t: (i, jnp.minimum(t, 1), 0, 0)),
                  _resident(g.shape), _resident(w.shape),
                  pl.BlockSpec((4, TM, LANES), lambda i, t: (0, t, 0))],
        out_specs=[ospec(qw), ospec(kw), ospec(kw)],
        out_shape=[out(qw), out(kw), out(kw)],
        compiler_params=_params(2),
        name="proj_gqa",
    )(xs, modl, g, w, tab)


def _proj_b_kernel(x_ref, mod_ref, g_ref, wd_ref, gq_ref, gkv_ref, wuq_ref, wuk_ref, wuv_ref, tab_ref,
                   q_ref, k_ref, v_ref, *, q_lora, kv_lora):
    m = mod_ref[0, 0]
    h = _norm_mod(x_ref[0], g_ref[...], m[0:1], m[1:2]).astype(BF16)
    d = _dot(h, wd_ref[...])
    cq = _rms(d[:, :q_lora], gq_ref[...]).astype(BF16)
    ckv = _rms(d[:, q_lora:q_lora + kv_lora], gkv_ref[...]).astype(BF16)
    lane = lax.broadcasted_iota(jnp.int32, (TM, LANES), 1)
    first = lane < (B_NOPE + B_ROPE // 2)
    tq_c, tq_s, tk_c, tk_s = tab_ref[0], tab_ref[1], tab_ref[2], tab_ref[3]
    k_rope = _rope_block(d[:, q_lora + kv_lora:], tk_c, tk_s, first, B_ROPE // 2)
    q = _dot(cq, wuq_ref[...])
    k = _dot(ckv, wuk_ref[...])
    for j in range(B_HEADS):
        sl = slice(j * LANES, (j + 1) * LANES)
        q_ref[0, :, sl] = _rope_block(q[:, sl], tq_c, tq_s, first, B_ROPE // 2).astype(BF16)
        k_ref[0, :, sl] = (k[:, sl] + k_rope).astype(BF16)
    v_ref[0] = _dot(ckv, wuv_ref[...]).astype(BF16)


def _proj_b(xs, modl, g, wd, gq, gkv, wuq, wuk, wuv, tab):
    b, s_tot, d = xs.shape
    nt = s_tot // TM
    out = lambda n: jax.ShapeDtypeStruct((b, s_tot, n), BF16)
    ospec = lambda n: pl.BlockSpec((1, TM, n), lambda i, t: (i, t, 0))
    nq, nv = wuq.shape[1], wuv.shape[1]
    return pl.pallas_call(
        functools.partial(_proj_b_kernel, q_lora=gq.shape[1], kv_lora=gkv.shape[1]),
        grid=(b, nt),
        in_specs=[pl.BlockSpec((1, TM, d), lambda i, t: (i, t, 0)),
                  pl.BlockSpec((1, 1, 6, d), lambda i, t: (i, jnp.minimum(t, 1), 0, 0)),
                  _resident(g.shape), _resident(wd.shape), _resident(gq.shape), _resident(gkv.shape),
                  _resident(wuq.shape), _resident(wuk.shape), _resident(wuv.shape),
                  pl.BlockSpec((4, TM, LANES), lambda i, t: (0, t, 0))],
        out_specs=[ospec(nq), ospec(nq), ospec(nv)],
        out_shape=[out(nq), out(nq), out(nv)],
        compiler_params=_params(2),
        name="proj_mla",
    )(xs, modl, g, wd, gq, gkv, wuq, wuk, wuv, tab)


def _attn_a_kernel(sink_ref, q_ref, kc_ref, kp_ref, kt_ref, kn_ref, vc_ref, vp_ref, vt_ref, vn_ref, o_ref,
                   *, seq, ctx_tiles, t0):
    t = pl.program_id(1) + t0
    nwin = TM + 2 * WINDOW
    row = lax.broadcasted_iota(jnp.int32, (TM, nwin), 0)
    col = lax.broadcasted_iota(jnp.int32, (TM, nwin), 1)
    kpos = (t - ctx_tiles) * TM - WINDOW + col
    rel = col - row
    valid = (rel >= 0) & (rel <= 2 * WINDOW) & (kpos >= 0) & (kpos < seq) & (t >= ctx_tiles)
    bias = jnp.where(valid, 0.0, NEG).astype(F32)
    lane = lax.broadcasted_iota(jnp.int32, (1, LANES), 1)
    lo = lane < HEAD_DIM
    zero = jnp.zeros((), BF16)
    for j in range(A_KV_HEADS):
        sl = slice(j * LANES, (j + 1) * LANES)
        k_ctx = kc_ref[0, :, sl]
        k_win = jnp.concatenate([kp_ref[0, :, sl], kt_ref[0, :, sl], kn_ref[0, :, sl]], axis=0)
        v_ctx = vc_ref[0, :, sl]
        v_win = jnp.concatenate([vp_ref[0, :, sl], vt_ref[0, :, sl], vn_ref[0, :, sl]], axis=0)
        v_half = [(jnp.where(lo, v_ctx, zero), jnp.where(lo, v_win, zero)),
                  (jnp.where(lo, zero, v_ctx), jnp.where(lo, zero, v_win))]
        for pp in range(2):
            pair = 2 * j + pp
            psl = slice(pair * LANES, (pair + 1) * LANES)
            qp = q_ref[0, :, psl]
            out = None
            for e in range(2):
                qe = jnp.where(lo, qp, zero) if e == 0 else jnp.where(lo, zero, qp)
                sink = sink_ref[2 * pair + e]
                s_ctx = _dot_nt(qe, k_ctx)
                s_win = _dot_nt(qe, k_win) + bias
                mx = jnp.maximum(jnp.maximum(jnp.max(s_ctx, axis=1, keepdims=True),
                                             jnp.max(s_win, axis=1, keepdims=True)), sink)
                p_ctx = jnp.exp(s_ctx - mx)
                p_win = jnp.exp(s_win - mx)
                den = (jnp.sum(p_ctx, axis=1, keepdims=True) + jnp.sum(p_win, axis=1, keepdims=True)
                       + jnp.exp(sink - mx))
                pv = _dot(p_ctx.astype(BF16), v_half[e][0]) + _dot(p_win.astype(BF16), v_half[e][1])
                pv = pv * (1.0 / den)
                out = pv if out is None else out + pv
            o_ref[0, :, psl] = out.astype(BF16)


def _attn_a(q, k, v, sink, seq, with_ctx):
    b, s_tot, qw = q.shape
    kw = k.shape[2]
    t0 = 0 if with_ctx else (s_tot - seq) // TM
    nt = s_tot // TM - t0
    hb = TM // WINDOW
    n_half = s_tot // WINDOW
    ctx = pl.BlockSpec((1, TM, kw), lambda i, t: (i, 0, 0))
    prev = pl.BlockSpec((1, WINDOW, kw), lambda i, t: (i, jnp.maximum((t + t0) * hb - 1, 0), 0))
    cur = pl.BlockSpec((1, TM, kw), lambda i, t: (i, t + t0, 0))
    nxt = pl.BlockSpec((1, WINDOW, kw), lambda i, t: (i, jnp.minimum((t + t0 + 1) * hb, n_half - 1), 0))
    return pl.pallas_call(
        functools.partial(_attn_a_kernel, seq=seq, ctx_tiles=(s_tot - seq) // TM, t0=t0),
        grid=(b, nt),
        in_specs=[pl.BlockSpec(memory_space=pltpu.SMEM),
                  pl.BlockSpec((1, TM, qw), lambda i, t: (i, t + t0, 0)),
                  ctx, prev, cur, nxt, ctx, prev, cur, nxt],
        out_specs=pl.BlockSpec((1, TM, qw), lambda i, t: (i, t + t0, 0)),
        out_shape=jax.ShapeDtypeStruct((b, s_tot, qw), BF16),
        compiler_params=_params(2),
        name="attn_gqa",
    )(sink, q, k, k, k, k, v, v, v, v)


def _attn_b_kernel(q_ref, k_ref, v_ref, o_ref, *, ctx_len, seq, t0):
    t = pl.program_id(2) + t0
    lane = lax.broadcasted_iota(jnp.int32, (1, LANES), 1)
    lo = lane < B_V
    zero = jnp.zeros((), BF16)
    q = q_ref[0]
    qs = (q[:, :LANES], q[:, LANES:])

    def step(k0, tk, state):
        ms, ls, acc = state
        kc = k_ref[0, pl.ds(k0, tk), :]
        vc = v_ref[0, pl.ds(k0, tk), :]
        vs = (jnp.where(lo, vc, zero), jnp.where(lo, zero, vc))
        new_m, new_l, alphas, pv = [], [], [], None
        for e in range(2):
            s = _dot_nt(qs[e], kc[:, e * LANES:(e + 1) * LANES])
            m_new = jnp.maximum(ms[e], jnp.max(s, axis=1, keepdims=True))
            alpha = jnp.exp(ms[e] - m_new)
            p = jnp.exp(s - m_new)
            new_l.append(alpha * ls[e] + jnp.sum(p, axis=1, keepdims=True))
            new_m.append(m_new)
            alphas.append(alpha)
            c = _dot(p.astype(BF16), vs[e])
            pv = c if pv is None else pv + c
        acc = acc * jnp.where(lo, alphas[0], alphas[1]) + pv
        return tuple(new_m), tuple(new_l), acc

    init = ((jnp.full((TM, 1), NEG, F32),) * 2, (jnp.zeros((TM, 1), F32),) * 2,
            jnp.zeros((TM, LANES), F32))
    state = step(0, ctx_len, init)
    n_lat = jnp.where(t * TM < ctx_len, 0, seq // TK_MLA)
    state = lax.fori_loop(
        0, n_lat, lambda c, st: step(pl.multiple_of(ctx_len + c * TK_MLA, TM), TK_MLA, st), state)
    _, ls, acc = state
    o_ref[0] = (acc * jnp.where(lo, 1.0 / ls[0], 1.0 / ls[1])).astype(BF16)


def _attn_b(q, k, v, seq, with_ctx):
    b, s_tot, qw = q.shape
    vw = v.shape[2]
    ctx_len = s_tot - seq
    t0 = 0 if with_ctx else ctx_len // TM
    nt = s_tot // TM - t0
    pair_q, pair_v = 2 * LANES, LANES
    return pl.pallas_call(
        functools.partial(_attn_b_kernel, ctx_len=ctx_len, seq=seq, t0=t0),
        grid=(b, qw // pair_q, nt),
        in_specs=[pl.BlockSpec((1, TM, pair_q), lambda i, h, t: (i, t + t0, h)),
                  pl.BlockSpec((1, s_tot, pair_q), lambda i, h, t: (i, 0, h)),
                  pl.BlockSpec((1, s_tot, pair_v), lambda i, h, t: (i, 0, h))],
        out_specs=pl.BlockSpec((1, TM, pair_v), lambda i, h, t: (i, t + t0, h)),
        out_shape=jax.ShapeDtypeStruct((b, s_tot, vw), BF16),
        compiler_params=_params(3),
        name="attn_mla",
    )(q, k, v)


def _channel_kernel(xp_ref, xt_ref, xn_ref, op_ref, ot_ref, on_ref, mod_ref, g_ref, wo_ref, wa_ref, wv_ref,
                    cw_ref, cb_ref, wout_ref, gf_ref, y_ref, *, ctx_tiles, n_tiles, t0, final):
    t = pl.program_id(1) + t0
    ext = TM + 2 * HALO
    m = mod_ref[0, 0]
    x_ext = jnp.concatenate([xp_ref[0], xt_ref[0], xn_ref[0]], axis=0)
    o_ext = jnp.concatenate([op_ref[0], ot_ref[0], on_ref[0]], axis=0)
    x1 = x_ext + m[2:3] * _dot(o_ext, wo_ref[...])
    h2 = _norm_mod(x1, g_ref[...], m[3:4], m[4:5])
    prev_ok = (t != 0) & (t != ctx_tiles)
    next_ok = (t != ctx_tiles - 1) & (t != n_tiles - 1)
    row = lax.broadcasted_iota(jnp.int32, (ext, 1), 0)
    keep = ((row >= HALO) | prev_ok) & ((row < HALO + TM) | next_ok)
    h2 = jnp.where(keep, h2, 0.0).astype(BF16)
    h2_mid = h2[HALO:HALO + TM]
    cw = cw_ref[...]
    cb = cb_ref[...]
    d_ff = wa_ref.shape[1]
    y = None
    for c0 in range(0, d_ff, FF_CHUNK):
        cs = slice(c0, min(c0 + FF_CHUNK, d_ff))
        a = _dot(h2, wa_ref[:, cs])
        a_prev = pltpu.roll(a, 1, 0)[HALO:HALO + TM]
        a_next = pltpu.roll(a, ext - 1, 0)[HALO:HALO + TM]
        conv = a_prev * cw[0:1, cs] + a[HALO:HALO + TM] * cw[1:2, cs] + a_next * cw[2:3, cs] + cb[:, cs]
        gate = conv * (1.0 / (1.0 + jnp.exp(-conv)))
        hid = (gate * _dot(h2_mid, wv_ref[:, cs])).astype(BF16)
        part = _dot(hid, wout_ref[cs, :])
        y = part if y is None else y + part
    x2 = x1[HALO:HALO + TM] + m[5:6] * y
    y_ref[0] = _rms(x2, gf_ref[...]) if final else x2


def _channel(xs, o, modl, g, wo, wa, wv, cw, cb, wout, gf, seq, final):
    b, s_tot, d = xs.shape
    n_tiles = s_tot // TM
    ctx_tiles = (s_tot - seq) // TM
    t0 = ctx_tiles if final else 0
    nt = n_tiles - t0
    hb = TM // HALO
    n_halo = s_tot // HALO
    prev = lambda i, t: (i, jnp.maximum((t + t0) * hb - 1, t0 * hb), 0)
    cur = lambda i, t: (i, t + t0, 0)
    nxt = lambda i, t: (i, jnp.minimum((t + t0 + 1) * hb, n_halo - 1), 0)
    out_rows = seq if final else s_tot
    return pl.pallas_call(
        functools.partial(_channel_kernel, ctx_tiles=ctx_tiles, n_tiles=n_tiles, t0=t0, final=final),
        grid=(b, nt),
        in_specs=[pl.BlockSpec((1, HALO, d), prev), pl.BlockSpec((1, TM, d), cur), pl.BlockSpec((1, HALO, d), nxt),
                  pl.BlockSpec((1, HALO, d), prev), pl.BlockSpec((1, TM, d), cur), pl.BlockSpec((1, HALO, d), nxt),
                  pl.BlockSpec((1, 1, 6, d), lambda i, t: (i, jnp.minimum(t + t0, 1), 0, 0)),
                  _resident(g.shape), _resident(wo.shape), _resident(wa.shape), _resident(wv.shape),
                  _resident(cw.shape), _resident(cb.shape), _resident(wout.shape), _resident(gf.shape)],
        out_specs=pl.BlockSpec((1, TM, d), lambda i, t: (i, t, 0)),
        out_shape=jax.ShapeDtypeStruct((b, out_rows, d), F32),
        compiler_params=_params(2),
        name="channel",
    )(xs, xs, xs, o, o, o, modl, g, wo, wa, wv, cw, cb, wout, gf)


def _axial_angles(rows, rot_dim):
    row = jnp.repeat(jnp.arange(rows), GRID_W).astype(F32)
    col = jnp.tile(jnp.arange(GRID_W), rows).astype(F32)
    n_freq = rot_dim // 4
    inv = ROPE_BASE ** (-jnp.arange(n_freq, dtype=F32) / n_freq)
    return jnp.concatenate([row[:, None] * inv, col[:, None] * inv], axis=-1)


def _with_ctx_rows(cos, sin, ctx_len, q_scale):
    cos = jnp.concatenate([jnp.ones((ctx_len, LANES), F32), cos], axis=0)
    sin = jnp.concatenate([jnp.zeros((ctx_len, LANES), F32), sin], axis=0)
    return jnp.stack([cos * q_scale, sin * q_scale, cos, sin])


def _tables_a(seq, ctx_len):
    ang = _axial_angles(seq // GRID_W, HEAD_DIM)
    cos, sin = jnp.cos(ang), jnp.sin(ang)
    reps = LANES // HEAD_DIM
    cos_l = jnp.tile(cos, (1, 2 * reps))
    sin_l = jnp.tile(jnp.concatenate([-sin, sin], axis=1), (1, reps))
    return _with_ctx_rows(cos_l, sin_l, ctx_len, HEAD_DIM ** -0.5)


def _tables_b(seq, ctx_len):
    ang = _axial_angles(seq // GRID_W, B_ROPE)
    cos, sin = jnp.cos(ang), jnp.sin(ang)
    pad = LANES - B_NOPE - B_ROPE
    cos_l = jnp.concatenate([jnp.ones((seq, B_NOPE), F32), cos, cos, jnp.ones((seq, pad), F32)], axis=1)
    sin_l = jnp.concatenate([jnp.zeros((seq, B_NOPE), F32), -sin, sin, jnp.zeros((seq, pad), F32)], axis=1)
    return _with_ctx_rows(cos_l, sin_l, ctx_len, (B_NOPE + B_ROPE) ** -0.5)


def _deinterleave(n):
    return jnp.concatenate([jnp.arange(0, n, 2), jnp.arange(1, n, 2)])


def _weights_a(wqkv):
    d = wqkv.shape[0]
    kw = A_KV_HEADS * HEAD_DIM
    qw = wqkv.shape[1] - 2 * kw
    perm = _deinterleave(HEAD_DIM)
    wq = wqkv[:, :qw].reshape(d, -1, HEAD_DIM)[:, :, perm].reshape(d, qw)
    wk = wqkv[:, qw:qw + kw].reshape(d, A_KV_HEADS, HEAD_DIM)[:, :, perm]
    wv = wqkv[:, qw + kw:].reshape(d, A_KV_HEADS, HEAD_DIM)
    dup = lambda w: jnp.concatenate([w, w], axis=2).reshape(d, 2 * kw)
    return jnp.concatenate([wq, dup(wk), dup(wv)], axis=1).astype(BF16), qw, 2 * kw


def _weights_b(wdown, wuq, wuk, q_lora, kv_lora):
    d = wdown.shape[0]
    perm = _deinterleave(B_ROPE)
    pad = LANES - B_NOPE - B_ROPE
    w_rope = wdown[:, q_lora + kv_lora:][:, perm]
    w_rope = jnp.concatenate([jnp.zeros((d, B_NOPE), F32), w_rope, jnp.zeros((d, pad), F32)], axis=1)
    wd = jnp.concatenate([wdown[:, :q_lora + kv_lora], w_rope], axis=1)
    uq = wuq.reshape(q_lora, B_HEADS, B_NOPE + B_ROPE)
    uq = jnp.concatenate([uq[:, :, :B_NOPE], uq[:, :, B_NOPE:][:, :, perm],
                          jnp.zeros((q_lora, B_HEADS, pad), F32)], axis=2)
    uk = wuk.reshape(kv_lora, B_HEADS, B_NOPE)
    uk = jnp.concatenate([uk, jnp.zeros((kv_lora, B_HEADS, LANES - B_NOPE), F32)], axis=2)
    return (wd.astype(BF16), uq.reshape(q_lora, B_HEADS * LANES).astype(BF16),
            uk.reshape(kv_lora, B_HEADS * LANES).astype(BF16))


def kernel(x, c, ctx, c_ctx, mod_w, mod_b, norm1_g, norm2_g, a_wqkv, a_wo, a_sink, b_wdown, b_qnorm_g, b_wuq,
           b_kvnorm_g, b_wuk, b_wuv, b_wo, f_win, f_conv_w, f_conv_b, f_wout, final_g):
    bsz, seq, d = x.shape
    ctx_len = ctx.shape[1]
    depth = mod_w.shape[0]
    d_ff = f_wout.shape[1]
    assert ctx_len == TM and seq % TM == 0 and seq % TK_MLA == 0 and seq % GRID_W == 0

    cond_rows = -(-(bsz + 1) // 8) * 8
    cond = jnp.concatenate([c, c_ctx[None], jnp.zeros((cond_rows - bsz - 1, d), F32)], axis=0)
    mod = _modulation(cond, mod_w, mod_b)

    tab_a = _tables_a(seq, ctx_len)
    tab_b = _tables_b(seq, ctx_len)
    xs = jnp.concatenate([ctx, x], axis=1)
    row = lambda v: v.reshape(1, -1)

    for i in range(depth):
        last = i == depth - 1
        lat = mod[i, :bsz].reshape(bsz, 6, d)
        cmod = jnp.broadcast_to(mod[i, bsz].reshape(1, 6, d), (bsz, 6, d))
        modl = jnp.stack([cmod, lat], axis=1)
        j = i // 2
        if i % 2 == 0:
            w, qw, kw = _weights_a(a_wqkv[j])
            q, k, v = _proj_a(xs, modl, row(norm1_g[i]), w, tab_a, qw, kw)
            o = _attn_a(q, k, v, a_sink[j], seq, not last)
            wo = a_wo[j]
        else:
            q_lora, kv_lora = b_qnorm_g.shape[1], b_kvnorm_g.shape[1]
            wd, wuq, wuk = _weights_b(b_wdown[j], b_wuq[j], b_wuk[j], q_lora, kv_lora)
            q, k, v = _proj_b(xs, modl, row(norm1_g[i]), wd, row(b_qnorm_g[j]), row(b_kvnorm_g[j]),
                              wuq, wuk, b_wuv[j].astype(BF16), tab_b)
            o = _attn_b(q, k, v, seq, not last)
            wo = b_wo[j]
        xs = _channel(xs, o, modl, row(norm2_g[i]), wo.astype(BF16), f_win[i][:, :d_ff].astype(BF16),
                      f_win[i][:, d_ff:].astype(BF16), f_conv_w[i], row(f_conv_b[i]), f_wout[i].astype(BF16),
                      row(final_g), seq, last)
    return xs
```

```python
import functools

import jax
import jax.numpy as jnp
from jax import lax
from jax.experimental import pallas as pl
from jax.experimental.pallas import tpu as pltpu

GRID_W = 64
HEAD_DIM = 64
A_KV_HEADS = 4
WINDOW = 128
B_HEADS = 16
B_NOPE = 64
B_ROPE = 32
B_V = 64
ROPE_BASE = 10000.0
EPS = 1e-6
NEG = -1e30
LOG2E = 1.4426950408889634

LANES = 128
TM = 256
HALO = 16
TK_MLA = 1024
FF_CHUNK = 768
VMEM_LIMIT = 56 * 1024 * 1024

F32 = jnp.float32
BF16 = jnp.bfloat16
_NT = (((1,), (1,)), ((), ()))


def _dot(a, b):
    return jnp.dot(a, b, preferred_element_type=F32)


def _dot_nt(a, b):
    return lax.dot_general(a, b, _NT, preferred_element_type=F32)


def _rms(xf, g):
    ms = jnp.mean(xf * xf, axis=-1, keepdims=True)
    return xf * lax.rsqrt(ms + EPS) * g


def _norm_mod(xf, g, shift, scale):
    return _rms(xf, g) * (1.0 + scale) + shift


def _params(n_axes):
    return pltpu.CompilerParams(dimension_semantics=("parallel",) * n_axes,
                                vmem_limit_bytes=VMEM_LIMIT)


def _resident(shape):
    nd = len(shape)
    return pl.BlockSpec(shape, lambda *_: (0,) * nd, pipeline_mode=pl.Buffered(1))


def _mod_kernel(c_ref, w_ref, b_ref, o_ref):
    c = c_ref[...]
    silu = c * (1.0 / (1.0 + jnp.exp(-c)))
    o_ref[0] = _dot(silu.astype(BF16), w_ref[0].astype(BF16)) + b_ref[0]


def _modulation(cond, mod_w, mod_b):
    depth, d, d6 = mod_w.shape
    rows = cond.shape[0]
    return pl.pallas_call(
        _mod_kernel,
        grid=(depth, d6 // d),
        in_specs=[pl.BlockSpec((rows, d), lambda i, j: (0, 0)),
                  pl.BlockSpec((1, d, d), lambda i, j: (i, 0, j)),
                  pl.BlockSpec((1, 1, d), lambda i, j: (i, 0, j))],
        out_specs=pl.BlockSpec((1, rows, d), lambda i, j: (i, 0, j)),
        out_shape=jax.ShapeDtypeStruct((depth, rows, d6), F32),
        compiler_params=_params(2),
        name="modulation",
    )(cond, mod_w, mod_b.reshape(depth, 1, d6))


def _rope_block(xb, cos, sin_signed, first_half, half):
    swapped = jnp.where(first_half, pltpu.roll(xb, LANES - half, 1), pltpu.roll(xb, half, 1))
    return xb * cos + swapped * sin_signed


def _proj_a_kernel(x_ref, mod_ref, g_ref, w_ref, tab_ref, q_ref, k_ref, v_ref, *, qw, kw):
    m = mod_ref[0, 0]
    h = _norm_mod(x_ref[0], g_ref[...], m[0:1], m[1:2]).astype(BF16)
    qkv = _dot(h, w_ref[...])
    lane = lax.broadcasted_iota(jnp.int32, (TM, LANES), 1)
    first = (lane % HEAD_DIM) < (HEAD_DIM // 2)
    cq, sq, ck, sk = tab_ref[0], tab_ref[1], tab_ref[2], tab_ref[3]
    for j in range(qw // LANES):
        blk = qkv[:, j * LANES:(j + 1) * LANES]
        q_ref[0, :, j * LANES:(j + 1) * LANES] = _rope_block(blk, cq, sq, first, HEAD_DIM // 2).astype(BF16)
    for j in range(kw // LANES):
        blk = qkv[:, qw + j * LANES:qw + (j + 1) * LANES]
        k_ref[0, :, j * LANES:(j + 1) * LANES] = _rope_block(blk, ck, sk, first, HEAD_DIM // 2).astype(BF16)
    v_ref[0] = qkv[:, qw + kw:].astype(BF16)


def _proj_a(xs, modl, g, w, tab, qw, kw):
    b, s_tot, d = xs.shape
    nt = s_tot // TM
    out = lambda n: jax.ShapeDtypeStruct((b, s_tot, n), BF16)
    ospec = lambda n: pl.BlockSpec((1, TM, n), lambda i, t: (i, t, 0))
    return pl.pallas_call(
        functools.partial(_proj_a_kernel, qw=qw, kw=kw),
        grid=(b, nt),
        in_specs=[pl.BlockSpec((1, TM, d), lambda i, t: (i, t, 0)),
                  pl.BlockSpec((1, 1, 6, d), lambda i, t: (i, jnp.minimum(t, 1), 0, 0)),
                  _resident(g.shape), _resident(w.shape),
                  pl.BlockSpec((4, TM, LANES), lambda i, t: (0, t, 0))],
        out_specs=[ospec(qw), ospec(kw), ospec(kw)],
        out_shape=[out(qw), out(kw), out(kw)],
        compiler_params=_params(2),
        name="proj_gqa",
    )(xs, modl, g, w, tab)


def _proj_b_kernel(x_ref, mod_ref, g_ref, wd_ref, gq_ref, gkv_ref, wuq_ref, wuk_ref, wuv_ref, tab_ref,
                   q_ref, k_ref, v_ref, *, q_lora, kv_lora):
    m = mod_ref[0, 0]
    h = _norm_mod(x_ref[0], g_ref[...], m[0:1], m[1:2]).astype(BF16)
    d = _dot(h, wd_ref[...])
    cq = _rms(d[:, :q_lora], gq_ref[...]).astype(BF16)
    ckv = _rms(d[:, q_lora:q_lora + kv_lora], gkv_ref[...]).astype(BF16)
    lane = lax.broadcasted_iota(jnp.int32, (TM, LANES), 1)
    first = lane < (B_NOPE + B_ROPE // 2)
    tq_c, tq_s, tk_c, tk_s = tab_ref[0], tab_ref[1], tab_ref[2], tab_ref[3]
    k_rope = _rope_block(d[:, q_lora + kv_lora:], tk_c, tk_s, first, B_ROPE // 2)
    q = _dot(cq, wuq_ref[...])
    k = _dot(ckv, wuk_ref[...])
    for j in range(B_HEADS):
        sl = slice(j * LANES, (j + 1) * LANES)
        q_ref[0, :, sl] = _rope_block(q[:, sl], tq_c, tq_s, first, B_ROPE // 2).astype(BF16)
        k_ref[0, :, sl] = (k[:, sl] + k_rope).astype(BF16)
    vlane = lax.broadcasted_iota(jnp.int32, (1, wuv_ref.shape[1]), 1) % LANES
    v_ref[0] = (_dot(ckv, wuv_ref[...]) + jnp.where(vlane == B_V, 1.0, 0.0)).astype(BF16)


def _proj_b(xs, modl, g, wd, gq, gkv, wuq, wuk, wuv, tab):
    b, s_tot, d = xs.shape
    nt = s_tot // TM
    out = lambda n: jax.ShapeDtypeStruct((b, s_tot, n), BF16)
    ospec = lambda n: pl.BlockSpec((1, TM, n), lambda i, t: (i, t, 0))
    nq, nv = wuq.shape[1], wuv.shape[1]
    return pl.pallas_call(
        functools.partial(_proj_b_kernel, q_lora=gq.shape[1], kv_lora=gkv.shape[1]),
        grid=(b, nt),
        in_specs=[pl.BlockSpec((1, TM, d), lambda i, t: (i, t, 0)),
                  pl.BlockSpec((1, 1, 6, d), lambda i, t: (i, jnp.minimum(t, 1), 0, 0)),
                  _resident(g.shape), _resident(wd.shape), _resident(gq.shape), _resident(gkv.shape),
                  _resident(wuq.shape), _resident(wuk.shape), _resident(wuv.shape),
                  pl.BlockSpec((4, TM, LANES), lambda i, t: (0, t, 0))],
        out_specs=[ospec(nq), ospec(nq), ospec(nv)],
        out_shape=[out(nq), out(nq), out(nv)],
        compiler_params=_params(2),
        name="proj_mla",
    )(xs, modl, g, wd, gq, gkv, wuq, wuk, wuv, tab)


def _attn_a_kernel(sink_ref, q_ref, kc_ref, kp_ref, kt_ref, kn_ref, vc_ref, vp_ref, vt_ref, vn_ref, o_ref,
                   *, seq, ctx_tiles, t0):
    t = pl.program_id(1) + t0
    nwin = TM + 2 * WINDOW
    row = lax.broadcasted_iota(jnp.int32, (TM, nwin), 0)
    col = lax.broadcasted_iota(jnp.int32, (TM, nwin), 1)
    kpos = (t - ctx_tiles) * TM - WINDOW + col
    rel = col - row
    valid = (rel >= 0) & (rel <= 2 * WINDOW) & (kpos >= 0) & (kpos < seq) & (t >= ctx_tiles)
    bias = jnp.where(valid, 0.0, NEG).astype(F32)
    lane = lax.broadcasted_iota(jnp.int32, (1, LANES), 1)
    lo = lane < HEAD_DIM
    zero = jnp.zeros((), BF16)
    for j in range(A_KV_HEADS):
        sl = slice(j * LANES, (j + 1) * LANES)
        k_ctx = kc_ref[0, :, sl]
        k_win = jnp.concatenate([kp_ref[0, :, sl], kt_ref[0, :, sl], kn_ref[0, :, sl]], axis=0)
        v_ctx = vc_ref[0, :, sl]
        v_win = jnp.concatenate([vp_ref[0, :, sl], vt_ref[0, :, sl], vn_ref[0, :, sl]], axis=0)
        v_half = [(jnp.where(lo, v_ctx, zero), jnp.where(lo, v_win, zero)),
                  (jnp.where(lo, zero, v_ctx), jnp.where(lo, zero, v_win))]
        for pp in range(2):
            pair = 2 * j + pp
            psl = slice(pair * LANES, (pair + 1) * LANES)
            qp = q_ref[0, :, psl]
            out = None
            for e in range(2):
                qe = jnp.where(lo, qp, zero) if e == 0 else jnp.where(lo, zero, qp)
                sink = sink_ref[2 * pair + e] * LOG2E
                s_ctx = _dot_nt(qe, k_ctx)
                s_win = _dot_nt(qe, k_win) + bias
                mx = jnp.maximum(jnp.maximum(jnp.max(s_ctx, axis=1, keepdims=True),
                                             jnp.max(s_win, axis=1, keepdims=True)), sink)
                p_ctx = jnp.exp2(s_ctx - mx)
                p_win = jnp.exp2(s_win - mx)
                den = (jnp.sum(p_ctx, axis=1, keepdims=True) + jnp.sum(p_win, axis=1, keepdims=True)
                       + jnp.exp2(sink - mx))
                pv = _dot(p_ctx.astype(BF16), v_half[e][0]) + _dot(p_win.astype(BF16), v_half[e][1])
                pv = pv * (1.0 / den)
                out = pv if out is None else out + pv
            o_ref[0, :, psl] = out.astype(BF16)


def _attn_a(q, k, v, sink, seq, with_ctx):
    b, s_tot, qw = q.shape
    kw = k.shape[2]
    t0 = 0 if with_ctx else (s_tot - seq) // TM
    nt = s_tot // TM - t0
    hb = TM // WINDOW
    n_half = s_tot // WINDOW
    ctx = pl.BlockSpec((1, TM, kw), lambda i, t: (i, 0, 0))
    prev = pl.BlockSpec((1, WINDOW, kw), lambda i, t: (i, jnp.maximum((t + t0) * hb - 1, 0), 0))
    cur = pl.BlockSpec((1, TM, kw), lambda i, t: (i, t + t0, 0))
    nxt = pl.BlockSpec((1, WINDOW, kw), lambda i, t: (i, jnp.minimum((t + t0 + 1) * hb, n_half - 1), 0))
    return pl.pallas_call(
        functools.partial(_attn_a_kernel, seq=seq, ctx_tiles=(s_tot - seq) // TM, t0=t0),
        grid=(b, nt),
        in_specs=[pl.BlockSpec(memory_space=pltpu.SMEM),
                  pl.BlockSpec((1, TM, qw), lambda i, t: (i, t + t0, 0)),
                  ctx, prev, cur, nxt, ctx, prev, cur, nxt],
        out_specs=pl.BlockSpec((1, TM, qw), lambda i, t: (i, t + t0, 0)),
        out_shape=jax.ShapeDtypeStruct((b, s_tot, qw), BF16),
        compiler_params=_params(2),
        name="attn_gqa",
    )(sink, q, k, k, k, k, v, v, v, v)


def _attn_b_kernel(q_ref, k_ref, v_ref, o_ref, s_ref, *, ctx_len, seq, t0):
    t = pl.program_id(2) + t0
    lane = lax.broadcasted_iota(jnp.int32, (1, LANES), 1)
    lo = lane < B_V
    q = q_ref[0]
    qs = (q[:, :LANES], q[:, LANES:])

    def attend(chunks):
        run = [jnp.full((TM, LANES), NEG, F32)] * 2
        for k0, tk in chunks:
            for e in range(2):
                s = _dot_nt(qs[e], k_ref[0, k0:k0 + tk, e * LANES:(e + 1) * LANES])
                s_ref[e, :, k0:k0 + tk] = s
                for j in range(tk // LANES):
                    run[e] = jnp.maximum(run[e], s[:, j * LANES:(j + 1) * LANES])
        mx = [jnp.max(run[e], axis=1, keepdims=True) for e in range(2)]
        acc = [None, None]
        for k0, tk in chunks:
            for e in range(2):
                p = jnp.exp2(s_ref[e, :, k0:k0 + tk] - mx[e]).astype(BF16)
                c = _dot(p, v_ref[0, k0:k0 + tk, e * LANES:(e + 1) * LANES])
                acc[e] = c if acc[e] is None else acc[e] + c
        out = [acc[e] * (1.0 / acc[e][:, B_V:B_V + 1]) for e in range(2)]
        o_ref[0] = jnp.where(lo, out[0], pltpu.roll(out[1], B_V, 1)).astype(BF16)

    ctx_chunk = [(0, ctx_len)]
    if t0 * TM < ctx_len:
        pl.when(t * TM < ctx_len)(lambda: attend(ctx_chunk))
    lat_chunks = [(ctx_len + i * TK_MLA, TK_MLA) for i in range(seq // TK_MLA)]
    pl.when(t * TM >= ctx_len)(lambda: attend(ctx_chunk + lat_chunks))


def _attn_b(q, k, v, seq, with_ctx):
    b, s_tot, qw = q.shape
    ctx_len = s_tot - seq
    t0 = 0 if with_ctx else ctx_len // TM
    nt = s_tot // TM - t0
    pair = 2 * LANES
    return pl.pallas_call(
        functools.partial(_attn_b_kernel, ctx_len=ctx_len, seq=seq, t0=t0),
        grid=(b, qw // pair, nt),
        in_specs=[pl.BlockSpec((1, TM, pair), lambda i, h, t: (i, t + t0, h)),
                  pl.BlockSpec((1, s_tot, pair), lambda i, h, t: (i, 0, h)),
                  pl.BlockSpec((1, s_tot, pair), lambda i, h, t: (i, 0, h))],
        out_specs=pl.BlockSpec((1, TM, 2 * B_V), lambda i, h, t: (i, t + t0, h)),
        out_shape=jax.ShapeDtypeStruct((b, s_tot, B_HEADS * B_V), BF16),
        scratch_shapes=[pltpu.VMEM((2, TM, s_tot), F32)],
        compiler_params=_params(3),
        name="attn_mla",
    )(q, k, v)


def _channel_kernel(xp_ref, xt_ref, xn_ref, op_ref, ot_ref, on_ref, mod_ref, g_ref, wo_ref, wa_ref, wv_ref,
                    cw_ref, cb_ref, wout_ref, gf_ref, y_ref, *, ctx_tiles, n_tiles, t0, final):
    t = pl.program_id(1) + t0
    ext = TM + 2 * HALO
    m = mod_ref[0, 0]
    x_ext = jnp.concatenate([xp_ref[0], xt_ref[0], xn_ref[0]], axis=0)
    o_ext = jnp.concatenate([op_ref[0], ot_ref[0], on_ref[0]], axis=0)
    x1 = x_ext + m[2:3] * _dot(o_ext, wo_ref[...])
    h2 = _norm_mod(x1, g_ref[...], m[3:4], m[4:5])
    prev_ok = (t != 0) & (t != ctx_tiles)
    next_ok = (t != ctx_tiles - 1) & (t != n_tiles - 1)
    row = lax.broadcasted_iota(jnp.int32, (ext, 1), 0)
    keep = ((row >= HALO) | prev_ok) & ((row < HALO + TM) | next_ok)
    h2 = jnp.where(keep, h2, 0.0).astype(BF16)
    h2_mid = h2[HALO:HALO + TM]
    cw = cw_ref[...]
    cb = cb_ref[...]
    d_ff = wa_ref.shape[1]
    y = None
    for c0 in range(0, d_ff, FF_CHUNK):
        cs = slice(c0, min(c0 + FF_CHUNK, d_ff))
        a = _dot(h2, wa_ref[:, cs])
        a_prev = pltpu.roll(a, 1, 0)[HALO:HALO + TM]
        a_next = pltpu.roll(a, ext - 1, 0)[HALO:HALO + TM]
        conv = a_prev * cw[0:1, cs] + a[HALO:HALO + TM] * cw[1:2, cs] + a_next * cw[2:3, cs] + cb[:, cs]
        gate = conv * (1.0 / (1.0 + jnp.exp(-conv)))
        hid = (gate * _dot(h2_mid, wv_ref[:, cs])).astype(BF16)
        part = _dot(hid, wout_ref[cs, :])
        y = part if y is None else y + part
    x2 = x1[HALO:HALO + TM] + m[5:6] * y
    y_ref[0] = _rms(x2, gf_ref[...]) if final else x2


def _channel(xs, o, modl, g, wo, wa, wv, cw, cb, wout, gf, seq, final):
    b, s_tot, d = xs.shape
    n_tiles = s_tot // TM
    ctx_tiles = (s_tot - seq) // TM
    t0 = ctx_tiles if final else 0
    nt = n_tiles - t0
    hb = TM // HALO
    n_halo = s_tot // HALO
    prev = lambda i, t: (i, jnp.maximum((t + t0) * hb - 1, t0 * hb), 0)
    cur = lambda i, t: (i, t + t0, 0)
    nxt = lambda i, t: (i, jnp.minimum((t + t0 + 1) * hb, n_halo - 1), 0)
    out_rows = seq if final else s_tot
    return pl.pallas_call(
        functools.partial(_channel_kernel, ctx_tiles=ctx_tiles, n_tiles=n_tiles, t0=t0, final=final),
        grid=(b, nt),
        in_specs=[pl.BlockSpec((1, HALO, d), prev), pl.BlockSpec((1, TM, d), cur), pl.BlockSpec((1, HALO, d), nxt),
                  pl.BlockSpec((1, HALO, d), prev), pl.BlockSpec((1, TM, d), cur), pl.BlockSpec((1, HALO, d), nxt),
                  pl.BlockSpec((1, 1, 6, d), lambda i, t: (i, jnp.minimum(t + t0, 1), 0, 0)),
                  _resident(g.shape), _resident(wo.shape), _resident(wa.shape), _resident(wv.shape),
                  _resident(cw.shape), _resident(cb.shape), _resident(wout.shape), _resident(gf.shape)],
        out_specs=pl.BlockSpec((1, TM, d), lambda i, t: (i, t, 0)),
        out_shape=jax.ShapeDtypeStruct((b, out_rows, d), F32),
        compiler_params=_params(2),
        name="channel",
    )(xs, xs, xs, o, o, o, modl, g, wo, wa, wv, cw, cb, wout, gf)


def _axial_angles(rows, rot_dim):
    row = jnp.repeat(jnp.arange(rows), GRID_W).astype(F32)
    col = jnp.tile(jnp.arange(GRID_W), rows).astype(F32)
    n_freq = rot_dim // 4
    inv = ROPE_BASE ** (-jnp.arange(n_freq, dtype=F32) / n_freq)
    return jnp.concatenate([row[:, None] * inv, col[:, None] * inv], axis=-1)


def _with_ctx_rows(cos, sin, ctx_len, q_scale):
    cos = jnp.concatenate([jnp.ones((ctx_len, LANES), F32), cos], axis=0)
    sin = jnp.concatenate([jnp.zeros((ctx_len, LANES), F32), sin], axis=0)
    return jnp.stack([cos * q_scale, sin * q_scale, cos, sin])


def _tables_a(seq, ctx_len):
    ang = _axial_angles(seq // GRID_W, HEAD_DIM)
    cos, sin = jnp.cos(ang), jnp.sin(ang)
    reps = LANES // HEAD_DIM
    cos_l = jnp.tile(cos, (1, 2 * reps))
    sin_l = jnp.tile(jnp.concatenate([-sin, sin], axis=1), (1, reps))
    return _with_ctx_rows(cos_l, sin_l, ctx_len, LOG2E * HEAD_DIM ** -0.5)


def _tables_b(seq, ctx_len):
    ang = _axial_angles(seq // GRID_W, B_ROPE)
    cos, sin = jnp.cos(ang), jnp.sin(ang)
    pad = LANES - B_NOPE - B_ROPE
    cos_l = jnp.concatenate([jnp.ones((seq, B_NOPE), F32), cos, cos, jnp.ones((seq, pad), F32)], axis=1)
    sin_l = jnp.concatenate([jnp.zeros((seq, B_NOPE), F32), -sin, sin, jnp.zeros((seq, pad), F32)], axis=1)
    return _with_ctx_rows(cos_l, sin_l, ctx_len, LOG2E * (B_NOPE + B_ROPE) ** -0.5)


def _deinterleave(n):
    return jnp.concatenate([jnp.arange(0, n, 2), jnp.arange(1, n, 2)])


def _weights_a(wqkv):
    d = wqkv.shape[0]
    kw = A_KV_HEADS * HEAD_DIM
    qw = wqkv.shape[1] - 2 * kw
    perm = _deinterleave(HEAD_DIM)
    wq = wqkv[:, :qw].reshape(d, -1, HEAD_DIM)[:, :, perm].reshape(d, qw)
    wk = wqkv[:, qw:qw + kw].reshape(d, A_KV_HEADS, HEAD_DIM)[:, :, perm]
    wv = wqkv[:, qw + kw:].reshape(d, A_KV_HEADS, HEAD_DIM)
    dup = lambda w: jnp.concatenate([w, w], axis=2).reshape(d, 2 * kw)
    return jnp.concatenate([wq, dup(wk), dup(wv)], axis=1).astype(BF16), qw, 2 * kw


def _weights_b(wdown, wuq, wuk, wuv, q_lora, kv_lora):
    d = wdown.shape[0]
    perm = _deinterleave(B_ROPE)
    pad = LANES - B_NOPE - B_ROPE
    w_rope = wdown[:, q_lora + kv_lora:][:, perm]
    w_rope = jnp.concatenate([jnp.zeros((d, B_NOPE), F32), w_rope, jnp.zeros((d, pad), F32)], axis=1)
    wd = jnp.concatenate([wdown[:, :q_lora + kv_lora], w_rope], axis=1)
    uq = wuq.reshape(q_lora, B_HEADS, B_NOPE + B_ROPE)
    uq = jnp.concatenate([uq[:, :, :B_NOPE], uq[:, :, B_NOPE:][:, :, perm],
                          jnp.zeros((q_lora, B_HEADS, pad), F32)], axis=2)
    uk = wuk.reshape(kv_lora, B_HEADS, B_NOPE)
    uk = jnp.concatenate([uk, jnp.zeros((kv_lora, B_HEADS, LANES - B_NOPE), F32)], axis=2)
    uv = wuv.reshape(kv_lora, B_HEADS, B_V)
    uv = jnp.concatenate([uv, jnp.zeros((kv_lora, B_HEADS, LANES - B_V), F32)], axis=2)
    flat = lambda w: w.reshape(w.shape[0], B_HEADS * LANES).astype(BF16)
    return wd.astype(BF16), flat(uq), flat(uk), flat(uv)


def kernel(x, c, ctx, c_ctx, mod_w, mod_b, norm1_g, norm2_g, a_wqkv, a_wo, a_sink, b_wdown, b_qnorm_g, b_wuq,
           b_kvnorm_g, b_wuk, b_wuv, b_wo, f_win, f_conv_w, f_conv_b, f_wout, final_g):
    bsz, seq, d = x.shape
    ctx_len = ctx.shape[1]
    depth = mod_w.shape[0]
    d_ff = f_wout.shape[1]
    assert ctx_len == TM and seq % TM == 0 and seq % TK_MLA == 0 and seq % GRID_W == 0

    cond_rows = -(-(bsz + 1) // 8) * 8
    cond = jnp.concatenate([c, c_ctx[None], jnp.zeros((cond_rows - bsz - 1, d), F32)], axis=0)
    mod = _modulation(cond, mod_w, mod_b)

    tab_a = _tables_a(seq, ctx_len)
    tab_b = _tables_b(seq, ctx_len)
    xs = jnp.concatenate([ctx, x], axis=1)
    row = lambda v: v.reshape(1, -1)

    for i in range(depth):
        last = i == depth - 1
        lat = mod[i, :bsz].reshape(bsz, 6, d)
        cmod = jnp.broadcast_to(mod[i, bsz].reshape(1, 6, d), (bsz, 6, d))
        modl = jnp.stack([cmod, lat], axis=1)
        j = i // 2
        if i % 2 == 0:
            w, qw, kw = _weights_a(a_wqkv[j])
            q, k, v = _proj_a(xs, modl, row(norm1_g[i]), w, tab_a, qw, kw)
            o = _attn_a(q, k, v, a_sink[j], seq, not last)
            wo = a_wo[j]
        else:
            q_lora, kv_lora = b_qnorm_g.shape[1], b_kvnorm_g.shape[1]
            wd, wuq, wuk, wuv = _weights_b(b_wdown[j], b_wuq[j], b_wuk[j], b_wuv[j], q_lora, kv_lora)
            q, k, v = _proj_b(xs, modl, row(norm1_g[i]), wd, row(b_qnorm_g[j]), row(b_kvnorm_g[j]),
                              wuq, wuk, wuv, tab_b)
            o = _attn_b(q, k, v, seq, not last)
            wo = b_wo[j]
        xs = _channel(xs, o, modl, row(norm2_g[i]), wo.astype(BF16), f_win[i][:, :d_ff].astype(BF16),
                      f_win[i][:, d_ff:].astype(BF16), f_conv_w[i], row(f_conv_b[i]), f_wout[i].astype(BF16),
                      row(final_g), seq, last)
    return xs
```

```python
import functools

import jax
import jax.numpy as jnp
from jax import lax
from jax.experimental import pallas as pl
from jax.experimental.pallas import tpu as pltpu

GRID_W = 64
HEAD_DIM = 64
A_KV_HEADS = 4
WINDOW = 128
B_HEADS = 16
B_NOPE = 64
B_ROPE = 32
B_V = 64
ROPE_BASE = 10000.0
EPS = 1e-6
NEG = -1e30
LOG2E = 1.4426950408889634

LANES = 128
TM = 256
HALO = 16
TK_MLA = 1024
MLA_PAIRS = 2
MAX_FOLD = 64
FF_CHUNK = 768
VMEM_LIMIT = 56 * 1024 * 1024

F32 = jnp.float32
BF16 = jnp.bfloat16
_NT = (((1,), (1,)), ((), ()))


def _dot(a, b):
    return jnp.dot(a, b, preferred_element_type=F32)


def _dot_nt(a, b):
    return lax.dot_general(a, b, _NT, preferred_element_type=F32)


def _rms(xf, g):
    ms = jnp.mean(xf * xf, axis=-1, keepdims=True)
    return xf * lax.rsqrt(ms + EPS) * g


def _norm_mod(xf, g, shift, scale):
    return _rms(xf, g) * (1.0 + scale) + shift


def _params(n_axes):
    return pltpu.CompilerParams(dimension_semantics=("parallel",) * n_axes,
                                vmem_limit_bytes=VMEM_LIMIT)


def _resident(shape):
    nd = len(shape)
    return pl.BlockSpec(shape, lambda *_: (0,) * nd, pipeline_mode=pl.Buffered(1))


def _mod_kernel(c_ref, w_ref, b_ref, o_ref):
    c = c_ref[...]
    silu = c * (1.0 / (1.0 + jnp.exp(-c)))
    o_ref[0] = _dot(silu.astype(BF16), w_ref[0].astype(BF16)) + b_ref[0]


def _modulation(cond, mod_w, mod_b):
    depth, d, d6 = mod_w.shape
    rows = cond.shape[0]
    return pl.pallas_call(
        _mod_kernel,
        grid=(depth, d6 // d),
        in_specs=[pl.BlockSpec((rows, d), lambda i, j: (0, 0)),
                  pl.BlockSpec((1, d, d), lambda i, j: (i, 0, j)),
                  pl.BlockSpec((1, 1, d), lambda i, j: (i, 0, j))],
        out_specs=pl.BlockSpec((1, rows, d), lambda i, j: (i, 0, j)),
        out_shape=jax.ShapeDtypeStruct((depth, rows, d6), F32),
        compiler_params=_params(2),
        name="modulation",
    )(cond, mod_w, mod_b.reshape(depth, 1, d6))


def _rope_block(xb, cos, sin_signed, first_half, half):
    swapped = jnp.where(first_half, pltpu.roll(xb, LANES - half, 1), pltpu.roll(xb, half, 1))
    return xb * cos + swapped * sin_signed


def _proj_a_kernel(x_ref, mod_ref, g_ref, w_ref, wvt_ref, tab_ref, q_ref, k_ref, vt_ref, *, qw, kw):
    m = mod_ref[0, 0]
    h = _norm_mod(x_ref[0], g_ref[...], m[0:1], m[1:2]).astype(BF16)
    qk = _dot(h, w_ref[...])
    lane = lax.broadcasted_iota(jnp.int32, (TM, LANES), 1)
    first = (lane % HEAD_DIM) < (HEAD_DIM // 2)
    cq, sq, ck, sk = tab_ref[0], tab_ref[1], tab_ref[2], tab_ref[3]
    for j in range(qw // LANES):
        blk = qk[:, j * LANES:(j + 1) * LANES]
        q_ref[0, :, j * LANES:(j + 1) * LANES] = _rope_block(blk, cq, sq, first, HEAD_DIM // 2).astype(BF16)
    for j in range(kw // LANES):
        blk = qk[:, qw + j * LANES:qw + (j + 1) * LANES]
        k_ref[0, :, j * LANES:(j + 1) * LANES] = _rope_block(blk, ck, sk, first, HEAD_DIM // 2).astype(BF16)
    vrow = lax.broadcasted_iota(jnp.int32, (wvt_ref.shape[0], 1), 0) % LANES
    vt_ref[0] = (_dot_nt(wvt_ref[...], h) + jnp.where(vrow == HEAD_DIM, 1.0, 0.0)).astype(BF16)


def _proj_a(xs, modl, g, w, wvt, tab, qw, kw):
    b, s_tot, d = xs.shape
    nt = s_tot // TM
    nv = wvt.shape[0]
    out = lambda n: jax.ShapeDtypeStruct((b, s_tot, n), BF16)
    ospec = lambda n: pl.BlockSpec((1, TM, n), lambda i, t: (i, t, 0))
    return pl.pallas_call(
        functools.partial(_proj_a_kernel, qw=qw, kw=kw),
        grid=(b, nt),
        in_specs=[pl.BlockSpec((1, TM, d), lambda i, t: (i, t, 0)),
                  pl.BlockSpec((1, 1, 6, d), lambda i, t: (i, jnp.minimum(t, 1), 0, 0)),
                  _resident(g.shape), _resident(w.shape), _resident(wvt.shape),
                  pl.BlockSpec((4, TM, LANES), lambda i, t: (0, t, 0))],
        out_specs=[ospec(qw), ospec(kw), pl.BlockSpec((1, nv, TM), lambda i, t: (i, 0, t))],
        out_shape=[out(qw), out(kw), jax.ShapeDtypeStruct((b, nv, s_tot), BF16)],
        compiler_params=_params(2),
        name="proj_gqa",
    )(xs, modl, g, w, wvt, tab)


def _proj_b_kernel(x_ref, mod_ref, g_ref, wd_ref, gq_ref, gkv_ref, wuq_ref, wuk_ref, wuvt_ref, tab_ref,
                   q_ref, k_ref, vt_ref, *, q_lora, kv_lora):
    m = mod_ref[0, 0]
    h = _norm_mod(x_ref[0], g_ref[...], m[0:1], m[1:2]).astype(BF16)
    d = _dot(h, wd_ref[...])
    cq = _rms(d[:, :q_lora], gq_ref[...]).astype(BF16)
    ckv = _rms(d[:, q_lora:q_lora + kv_lora], gkv_ref[...]).astype(BF16)
    lane = lax.broadcasted_iota(jnp.int32, (TM, LANES), 1)
    first = lane < (B_NOPE + B_ROPE // 2)
    tq_c, tq_s, tk_c, tk_s = tab_ref[0], tab_ref[1], tab_ref[2], tab_ref[3]
    k_rope = _rope_block(d[:, q_lora + kv_lora:], tk_c, tk_s, first, B_ROPE // 2)
    q = _dot(cq, wuq_ref[...])
    k = _dot(ckv, wuk_ref[...])
    for j in range(B_HEADS):
        sl = slice(j * LANES, (j + 1) * LANES)
        q_ref[0, :, sl] = _rope_block(q[:, sl], tq_c, tq_s, first, B_ROPE // 2).astype(BF16)
        k_ref[0, :, sl] = (k[:, sl] + k_rope).astype(BF16)
    vrow = lax.broadcasted_iota(jnp.int32, (wuvt_ref.shape[0], 1), 0) % LANES
    vt_ref[0] = (_dot_nt(wuvt_ref[...], ckv) + jnp.where(vrow == B_V, 1.0, 0.0)).astype(BF16)


def _proj_b(xs, modl, g, wd, gq, gkv, wuq, wuk, wuvt, tab):
    b, s_tot, d = xs.shape
    nt = s_tot // TM
    out = lambda n: jax.ShapeDtypeStruct((b, s_tot, n), BF16)
    ospec = lambda n: pl.BlockSpec((1, TM, n), lambda i, t: (i, t, 0))
    nq, nv = wuq.shape[1], wuvt.shape[0]
    return pl.pallas_call(
        functools.partial(_proj_b_kernel, q_lora=gq.shape[1], kv_lora=gkv.shape[1]),
        grid=(b, nt),
        in_specs=[pl.BlockSpec((1, TM, d), lambda i, t: (i, t, 0)),
                  pl.BlockSpec((1, 1, 6, d), lambda i, t: (i, jnp.minimum(t, 1), 0, 0)),
                  _resident(g.shape), _resident(wd.shape), _resident(gq.shape), _resident(gkv.shape),
                  _resident(wuq.shape), _resident(wuk.shape), _resident(wuvt.shape),
                  pl.BlockSpec((4, TM, LANES), lambda i, t: (0, t, 0))],
        out_specs=[ospec(nq), ospec(nq), pl.BlockSpec((1, nv, TM), lambda i, t: (i, 0, t))],
        out_shape=[out(nq), out(nq), jax.ShapeDtypeStruct((b, nv, s_tot), BF16)],
        compiler_params=_params(2),
        name="proj_mla",
    )(xs, modl, g, wd, gq, gkv, wuq, wuk, wuvt, tab)


def _attn_a_kernel(sink_ref, q_ref, kc_ref, kp_ref, kt_ref, kn_ref, vc_ref, vp_ref, vt_ref, vn_ref, o_ref,
                   *, ctx_tiles, n_tiles, t0):
    t = pl.program_id(1) + t0

    def bias(rows, lo_rel, hi_rel, ok):
        key = lax.broadcasted_iota(jnp.int32, (rows, TM), 0)
        qry = lax.broadcasted_iota(jnp.int32, (rows, TM), 1)
        rel = qry - key
        return jnp.where((rel >= lo_rel) & (rel <= hi_rel) & ok, 0.0, NEG).astype(F32)

    latent = t >= ctx_tiles
    b_prev = bias(WINDOW, -WINDOW, 0, t > ctx_tiles)
    b_cur = bias(TM, -WINDOW, WINDOW, latent)
    b_next = bias(WINDOW, WINDOW, TM, latent & (t < n_tiles - 1))
    lane = lax.broadcasted_iota(jnp.int32, (1, LANES), 1)
    lo = lane < HEAD_DIM
    zero = jnp.zeros((), BF16)

    def col_max(s):
        run = s[:MAX_FOLD]
        for i in range(1, s.shape[0] // MAX_FOLD):
            run = jnp.maximum(run, s[i * MAX_FOLD:(i + 1) * MAX_FOLD])
        return jnp.max(run, axis=0, keepdims=True)

    b_win = jnp.concatenate([b_prev, b_cur, b_next], axis=0)
    n_ctx = kc_ref.shape[1]
    n_heads = q_ref.shape[2] // HEAD_DIM
    group = n_heads // A_KV_HEADS

    def keys_of(j):
        sl = slice(j * LANES, (j + 1) * LANES)
        k = jnp.concatenate([kc_ref[0, :, sl], kp_ref[0, :, sl], kt_ref[0, :, sl], kn_ref[0, :, sl]], axis=0)
        v = jnp.concatenate([vc_ref[0, sl, :], vp_ref[0, sl, :], vt_ref[0, sl, :], vn_ref[0, sl, :]], axis=1)
        return k, v

    kv = [keys_of(j) for j in range(A_KV_HEADS)]

    def score(h):
        qp = q_ref[0, :, (h // 2) * LANES:(h // 2 + 1) * LANES]
        qe = jnp.where(lo, qp, zero) if h % 2 == 0 else jnp.where(lo, zero, qp)
        s = _dot_nt(kv[h // group][0], qe)
        s_ctx, s_win = s[:n_ctx], s[n_ctx:] + b_win
        sink = sink_ref[h] * LOG2E
        mx = jnp.maximum(jnp.maximum(col_max(s_ctx), col_max(s_win)), sink)
        return s_ctx, s_win, mx, sink

    def probs(st):
        s_ctx, s_win, mx, sink = st
        p = jnp.concatenate([jnp.exp2(s_ctx - mx), jnp.exp2(s_win - mx)], axis=0).astype(BF16)
        return p, jnp.exp2(sink - mx)

    def values(h, pr):
        p, p_sink = pr
        acc = _dot(kv[h // group][1], p)
        return acc[:HEAD_DIM] * (1.0 / (acc[HEAD_DIM:HEAD_DIM + 1] + p_sink))

    scored, weighted, outs = {}, {}, {}
    for step in range(n_heads + 2):
        if step < n_heads:
            scored[step] = score(step)
        if 0 <= step - 1 < n_heads:
            weighted[step - 1] = probs(scored.pop(step - 1))
        h = step - 2
        if 0 <= h < n_heads:
            outs[h] = values(h, weighted.pop(h))
            if h % 2 == 1:
                pair_t = jnp.concatenate([outs.pop(h - 1), outs.pop(h)], axis=0)
                o_ref[0, :, (h // 2) * LANES:(h // 2 + 1) * LANES] = pair_t.T.astype(BF16)


def _attn_a(q, k, vt, sink, seq, with_ctx):
    b, s_tot, qw = q.shape
    kw, nv = k.shape[2], vt.shape[1]
    ctx_tiles = (s_tot - seq) // TM
    t0 = 0 if with_ctx else ctx_tiles
    n_tiles = s_tot // TM
    nt = n_tiles - t0
    hb = TM // WINDOW
    n_half = s_tot // WINDOW
    prev = lambda t: jnp.maximum((t + t0) * hb - 1, 0)
    nxt = lambda t: jnp.minimum((t + t0 + 1) * hb, n_half - 1)
    k_specs = [pl.BlockSpec((1, TM, kw), lambda i, t: (i, 0, 0)),
               pl.BlockSpec((1, WINDOW, kw), lambda i, t: (i, prev(t), 0)),
               pl.BlockSpec((1, TM, kw), lambda i, t: (i, t + t0, 0)),
               pl.BlockSpec((1, WINDOW, kw), lambda i, t: (i, nxt(t), 0))]
    v_specs = [pl.BlockSpec((1, nv, TM), lambda i, t: (i, 0, 0)),
               pl.BlockSpec((1, nv, WINDOW), lambda i, t: (i, 0, prev(t))),
               pl.BlockSpec((1, nv, TM), lambda i, t: (i, 0, t + t0)),
               pl.BlockSpec((1, nv, WINDOW), lambda i, t: (i, 0, nxt(t)))]
    return pl.pallas_call(
        functools.partial(_attn_a_kernel, ctx_tiles=ctx_tiles, n_tiles=n_tiles, t0=t0),
        grid=(b, nt),
        in_specs=[pl.BlockSpec(memory_space=pltpu.SMEM),
                  pl.BlockSpec((1, TM, qw), lambda i, t: (i, t + t0, 0))] + k_specs + v_specs,
        out_specs=pl.BlockSpec((1, TM, qw), lambda i, t: (i, t + t0, 0)),
        out_shape=jax.ShapeDtypeStruct((b, s_tot, qw), BF16),
        compiler_params=_params(2),
        name="attn_gqa",
    )(sink, q, k, k, k, k, vt, vt, vt, vt)


def _attn_b_kernel(q_ref, k_ref, vt_ref, o_ref, s_ref, *, ctx_len, seq, t0):
    t = pl.program_id(2) + t0
    n_pairs = q_ref.shape[2] // (2 * LANES)

    def score_chunk(h, k0, tk, run):
        hs = slice(h * LANES, (h + 1) * LANES)
        s = _dot_nt(k_ref[0, k0:k0 + tk, hs], q_ref[0, :, hs])
        s_ref[h, k0:k0 + tk, :] = s
        for i in range(tk // MAX_FOLD):
            run = jnp.maximum(run, s[i * MAX_FOLD:(i + 1) * MAX_FOLD])
        return run

    def value_chunk(h, k0, tk, mx, acc):
        p = jnp.exp2(s_ref[h, k0:k0 + tk, :] - mx).astype(BF16)
        c = _dot(vt_ref[0, h * LANES:(h + 1) * LANES, k0:k0 + tk], p)
        return c if acc is None else acc + c

    def attend(chunks):
        init = jnp.full((MAX_FOLD, TM), NEG, F32)
        mx = None
        for pr in range(n_pairs + 1):
            run, acc = [init, init], [None, None]
            for k0, tk in chunks:
                for e in range(2):
                    if pr < n_pairs:
                        run[e] = score_chunk(2 * pr + e, k0, tk, run[e])
                    if pr > 0:
                        acc[e] = value_chunk(2 * (pr - 1) + e, k0, tk, mx[e], acc[e])
            if pr > 0:
                out_t = jnp.concatenate([a[:B_V] * (1.0 / a[B_V:B_V + 1]) for a in acc], axis=0)
                o_ref[0, :, (pr - 1) * LANES:pr * LANES] = out_t.T.astype(BF16)
            mx = [jnp.max(r, axis=0, keepdims=True) for r in run]

    ctx_chunk = [(0, ctx_len)]
    pl.when(t * TM < ctx_len)(lambda: attend(ctx_chunk))
    lat_chunks = [(ctx_len + i * TK_MLA, TK_MLA) for i in range(seq // TK_MLA)]
    pl.when(t * TM >= ctx_len)(lambda: attend(ctx_chunk + lat_chunks))


def _attn_b(q, k, vt, seq, with_ctx):
    b, s_tot, qw = q.shape
    ctx_len = s_tot - seq
    t0 = 0 if with_ctx else ctx_len // TM
    nt = s_tot // TM - t0
    heads = 2 * MLA_PAIRS
    wide = heads * LANES
    return pl.pallas_call(
        functools.partial(_attn_b_kernel, ctx_len=ctx_len, seq=seq, t0=t0),
        grid=(b, qw // wide, nt),
        in_specs=[pl.BlockSpec((1, TM, wide), lambda i, h, t: (i, t + t0, h)),
                  pl.BlockSpec((1, s_tot, wide), lambda i, h, t: (i, 0, h)),
                  pl.BlockSpec((1, wide, s_tot), lambda i, h, t: (i, h, 0))],
        out_specs=pl.BlockSpec((1, TM, heads * B_V), lambda i, h, t: (i, t + t0, h)),
        out_shape=jax.ShapeDtypeStruct((b, s_tot, B_HEADS * B_V), BF16),
        scratch_shapes=[pltpu.VMEM((heads, s_tot, TM), F32)],
        compiler_params=_params(3),
        name="attn_mla",
    )(q, k, vt)


def _channel_kernel(xp_ref, xt_ref, xn_ref, op_ref, ot_ref, on_ref, mod_ref, g_ref, wo_ref, wa_ref, wv_ref,
                    cw_ref, cb_ref, wout_ref, gf_ref, y_ref, *, ctx_tiles, n_tiles, t0, final):
    t = pl.program_id(1) + t0
    ext = TM + 2 * HALO
    m = mod_ref[0, 0]
    x_ext = jnp.concatenate([xp_ref[0], xt_ref[0], xn_ref[0]], axis=0)
    o_ext = jnp.concatenate([op_ref[0], ot_ref[0], on_ref[0]], axis=0)
    x1 = x_ext + m[2:3] * _dot(o_ext, wo_ref[...])
    h2 = _norm_mod(x1, g_ref[...], m[3:4], m[4:5])
    prev_ok = (t != 0) & (t != ctx_tiles)
    next_ok = (t != ctx_tiles - 1) & (t != n_tiles - 1)
    row = lax.broadcasted_iota(jnp.int32, (ext, 1), 0)
    keep = ((row >= HALO) | prev_ok) & ((row < HALO + TM) | next_ok)
    h2 = jnp.where(keep, h2, 0.0).astype(BF16)
    h2_mid = h2[HALO:HALO + TM]
    cw = cw_ref[...]
    cb = cb_ref[...]
    d_ff = wa_ref.shape[1]
    y = None
    for c0 in range(0, d_ff, FF_CHUNK):
        cs = slice(c0, min(c0 + FF_CHUNK, d_ff))
        a = _dot(h2, wa_ref[:, cs])
        a_prev = pltpu.roll(a, 1, 0)[HALO:HALO + TM]
        a_next = pltpu.roll(a, ext - 1, 0)[HALO:HALO + TM]
        conv = a_prev * cw[0:1, cs] + a[HALO:HALO + TM] * cw[1:2, cs] + a_next * cw[2:3, cs] + cb[:, cs]
        gate = conv * (1.0 / (1.0 + jnp.exp(-conv)))
        hid = (gate * _dot(h2_mid, wv_ref[:, cs])).astype(BF16)
        part = _dot(hid, wout_ref[cs, :])
        y = part if y is None else y + part
    x2 = x1[HALO:HALO + TM] + m[5:6] * y
    y_ref[0] = _rms(x2, gf_ref[...]) if final else x2


def _channel(xs, o, modl, g, wo, wa, wv, cw, cb, wout, gf, seq, final):
    b, s_tot, d = xs.shape
    n_tiles = s_tot // TM
    ctx_tiles = (s_tot - seq) // TM
    t0 = ctx_tiles if final else 0
    nt = n_tiles - t0
    hb = TM // HALO
    n_halo = s_tot // HALO
    prev = lambda i, t: (i, jnp.maximum((t + t0) * hb - 1, t0 * hb), 0)
    cur = lambda i, t: (i, t + t0, 0)
    nxt = lambda i, t: (i, jnp.minimum((t + t0 + 1) * hb, n_halo - 1), 0)
    out_rows = seq if final else s_tot
    return pl.pallas_call(
        functools.partial(_channel_kernel, ctx_tiles=ctx_tiles, n_tiles=n_tiles, t0=t0, final=final),
        grid=(b, nt),
        in_specs=[pl.BlockSpec((1, HALO, d), prev), pl.BlockSpec((1, TM, d), cur), pl.BlockSpec((1, HALO, d), nxt),
                  pl.BlockSpec((1, HALO, d), prev), pl.BlockSpec((1, TM, d), cur), pl.BlockSpec((1, HALO, d), nxt),
                  pl.BlockSpec((1, 1, 6, d), lambda i, t: (i, jnp.minimum(t + t0, 1), 0, 0)),
                  _resident(g.shape), _resident(wo.shape), _resident(wa.shape), _resident(wv.shape),
                  _resident(cw.shape), _resident(cb.shape), _resident(wout.shape), _resident(gf.shape)],
        out_specs=pl.BlockSpec((1, TM, d), lambda i, t: (i, t, 0)),
        out_shape=jax.ShapeDtypeStruct((b, out_rows, d), F32),
        compiler_params=_params(2),
        name="channel",
    )(xs, xs, xs, o, o, o, modl, g, wo, wa, wv, cw, cb, wout, gf)


def _axial_angles(rows, rot_dim):
    row = jnp.repeat(jnp.arange(rows), GRID_W).astype(F32)
    col = jnp.tile(jnp.arange(GRID_W), rows).astype(F32)
    n_freq = rot_dim // 4
    inv = ROPE_BASE ** (-jnp.arange(n_freq, dtype=F32) / n_freq)
    return jnp.concatenate([row[:, None] * inv, col[:, None] * inv], axis=-1)


def _with_ctx_rows(cos, sin, ctx_len, q_scale):
    cos = jnp.concatenate([jnp.ones((ctx_len, LANES), F32), cos], axis=0)
    sin = jnp.concatenate([jnp.zeros((ctx_len, LANES), F32), sin], axis=0)
    return jnp.stack([cos * q_scale, sin * q_scale, cos, sin])


def _tables_a(seq, ctx_len):
    ang = _axial_angles(seq // GRID_W, HEAD_DIM)
    cos, sin = jnp.cos(ang), jnp.sin(ang)
    reps = LANES // HEAD_DIM
    cos_l = jnp.tile(cos, (1, 2 * reps))
    sin_l = jnp.tile(jnp.concatenate([-sin, sin], axis=1), (1, reps))
    return _with_ctx_rows(cos_l, sin_l, ctx_len, LOG2E * HEAD_DIM ** -0.5)


def _tables_b(seq, ctx_len):
    ang = _axial_angles(seq // GRID_W, B_ROPE)
    cos, sin = jnp.cos(ang), jnp.sin(ang)
    pad = LANES - B_NOPE - B_ROPE
    cos_l = jnp.concatenate([jnp.ones((seq, B_NOPE), F32), cos, cos, jnp.ones((seq, pad), F32)], axis=1)
    sin_l = jnp.concatenate([jnp.zeros((seq, B_NOPE), F32), -sin, sin, jnp.zeros((seq, pad), F32)], axis=1)
    return _with_ctx_rows(cos_l, sin_l, ctx_len, LOG2E * (B_NOPE + B_ROPE) ** -0.5)


def _deinterleave(n):
    return jnp.concatenate([jnp.arange(0, n, 2), jnp.arange(1, n, 2)])


def _weights_a(wqkv):
    d = wqkv.shape[0]
    kw = A_KV_HEADS * HEAD_DIM
    qw = wqkv.shape[1] - 2 * kw
    perm = _deinterleave(HEAD_DIM)
    wq = wqkv[:, :qw].reshape(d, -1, HEAD_DIM)[:, :, perm].reshape(d, qw)
    wk = wqkv[:, qw:qw + kw].reshape(d, A_KV_HEADS, HEAD_DIM)[:, :, perm]
    wkk = jnp.concatenate([wk, wk], axis=2).reshape(d, 2 * kw)
    wv = wqkv[:, qw + kw:].reshape(d, A_KV_HEADS, HEAD_DIM)
    wv = jnp.concatenate([wv, jnp.zeros((d, A_KV_HEADS, LANES - HEAD_DIM), F32)], axis=2)
    wvt = wv.reshape(d, A_KV_HEADS * LANES).T
    return jnp.concatenate([wq, wkk], axis=1).astype(BF16), wvt.astype(BF16), qw, 2 * kw


def _weights_b(wdown, wuq, wuk, wuv, q_lora, kv_lora):
    d = wdown.shape[0]
    perm = _deinterleave(B_ROPE)
    pad = LANES - B_NOPE - B_ROPE
    w_rope = wdown[:, q_lora + kv_lora:][:, perm]
    w_rope = jnp.concatenate([jnp.zeros((d, B_NOPE), F32), w_rope, jnp.zeros((d, pad), F32)], axis=1)
    wd = jnp.concatenate([wdown[:, :q_lora + kv_lora], w_rope], axis=1)
    uq = wuq.reshape(q_lora, B_HEADS, B_NOPE + B_ROPE)
    uq = jnp.concatenate([uq[:, :, :B_NOPE], uq[:, :, B_NOPE:][:, :, perm],
                          jnp.zeros((q_lora, B_HEADS, pad), F32)], axis=2)
    uk = wuk.reshape(kv_lora, B_HEADS, B_NOPE)
    uk = jnp.concatenate([uk, jnp.zeros((kv_lora, B_HEADS, LANES - B_NOPE), F32)], axis=2)
    uv = wuv.reshape(kv_lora, B_HEADS, B_V)
    uv = jnp.concatenate([uv, jnp.zeros((kv_lora, B_HEADS, LANES - B_V), F32)], axis=2)
    flat = lambda w: w.reshape(w.shape[0], B_HEADS * LANES).astype(BF16)
    return wd.astype(BF16), flat(uq), flat(uk), flat(uv).T


def kernel(x, c, ctx, c_ctx, mod_w, mod_b, norm1_g, norm2_g, a_wqkv, a_wo, a_sink, b_wdown, b_qnorm_g, b_wuq,
           b_kvnorm_g, b_wuk, b_wuv, b_wo, f_win, f_conv_w, f_conv_b, f_wout, final_g):
    bsz, seq, d = x.shape
    ctx_len = ctx.shape[1]
    depth = mod_w.shape[0]
    d_ff = f_wout.shape[1]
    assert ctx_len == TM and seq % TM == 0 and seq % TK_MLA == 0 and seq % GRID_W == 0

    cond_rows = -(-(bsz + 1) // 8) * 8
    cond = jnp.concatenate([c, c_ctx[None], jnp.zeros((cond_rows - bsz - 1, d), F32)], axis=0)
    mod = _modulation(cond, mod_w, mod_b)

    tab_a = _tables_a(seq, ctx_len)
    tab_b = _tables_b(seq, ctx_len)
    xs = jnp.concatenate([ctx, x], axis=1)
    row = lambda v: v.reshape(1, -1)

    for i in range(depth):
        last = i == depth - 1
        lat = mod[i, :bsz].reshape(bsz, 6, d)
        cmod = jnp.broadcast_to(mod[i, bsz].reshape(1, 6, d), (bsz, 6, d))
        modl = jnp.stack([cmod, lat], axis=1)
        j = i // 2
        if i % 2 == 0:
            w, wvt, qw, kw = _weights_a(a_wqkv[j])
            q, k, vt = _proj_a(xs, modl, row(norm1_g[i]), w, wvt, tab_a, qw, kw)
            o = _attn_a(q, k, vt, a_sink[j], seq, not last)
            wo = a_wo[j]
        else:
            q_lora, kv_lora = b_qnorm_g.shape[1], b_kvnorm_g.shape[1]
            wd, wuq, wuk, wuvt = _weights_b(b_wdown[j], b_wuq[j], b_wuk[j], b_wuv[j], q_lora, kv_lora)
            q, k, vt = _proj_b(xs, modl, row(norm1_g[i]), wd, row(b_qnorm_g[j]), row(b_kvnorm_g[j]),
                               wuq, wuk, wuvt, tab_b)
            o = _attn_b(q, k, vt, seq, not last)
            wo = b_wo[j]
        xs = _channel(xs, o, modl, row(norm2_g[i]), wo.astype(BF16), f_win[i][:, :d_ff].astype(BF16),
                      f_win[i][:, d_ff:].astype(BF16), f_conv_w[i], row(f_conv_b[i]), f_wout[i].astype(BF16),
                      row(final_g), seq, last)
    return xs
```

```python
import functools

import jax
import jax.numpy as jnp
from jax import lax
from jax.experimental import pallas as pl
from jax.experimental.pallas import tpu as pltpu

GRID_W = 64
HEAD_DIM = 64
A_KV_HEADS = 4
WINDOW = 128
B_HEADS = 16
B_NOPE = 64
B_ROPE = 32
B_V = 64
ROPE_BASE = 10000.0
EPS = 1e-6
NEG = -1e30
LOG2E = 1.4426950408889634

LANES = 128
TM = 256
HALO = 16
TK_MLA = 1024
MLA_PAIRS = 4
MAX_FOLD = 64
FF_CHUNK = 768
VMEM_LIMIT = 56 * 1024 * 1024

F32 = jnp.float32
BF16 = jnp.bfloat16
_NT = (((1,), (1,)), ((), ()))


def _dot(a, b):
    return jnp.dot(a, b, preferred_element_type=F32)


def _dot_nt(a, b):
    return lax.dot_general(a, b, _NT, preferred_element_type=F32)


def _rms(xf, g):
    ms = jnp.mean(xf * xf, axis=-1, keepdims=True)
    return xf * lax.rsqrt(ms + EPS) * g


def _norm_mod(xf, g, shift, scale):
    return _rms(xf, g) * (1.0 + scale) + shift


def _params(n_axes):
    return pltpu.CompilerParams(dimension_semantics=("parallel",) * n_axes,
                                vmem_limit_bytes=VMEM_LIMIT)


def _resident(shape):
    nd = len(shape)
    return pl.BlockSpec(shape, lambda *_: (0,) * nd, pipeline_mode=pl.Buffered(1))


def _mod_kernel(c_ref, w_ref, b_ref, o_ref):
    c = c_ref[...]
    silu = c * (1.0 / (1.0 + jnp.exp(-c)))
    o_ref[0] = _dot(silu.astype(BF16), w_ref[0].astype(BF16)) + b_ref[0]


def _modulation(cond, mod_w, mod_b):
    depth, d, d6 = mod_w.shape
    rows = cond.shape[0]
    return pl.pallas_call(
        _mod_kernel,
        grid=(depth, d6 // d),
        in_specs=[pl.BlockSpec((rows, d), lambda i, j: (0, 0)),
                  pl.BlockSpec((1, d, d), lambda i, j: (i, 0, j)),
                  pl.BlockSpec((1, 1, d), lambda i, j: (i, 0, j))],
        out_specs=pl.BlockSpec((1, rows, d), lambda i, j: (i, 0, j)),
        out_shape=jax.ShapeDtypeStruct((depth, rows, d6), F32),
        compiler_params=_params(2),
        name="modulation",
    )(cond, mod_w, mod_b.reshape(depth, 1, d6))


def _rope_block(xb, cos, sin_signed, first_half, half):
    swapped = jnp.where(first_half, pltpu.roll(xb, LANES - half, 1), pltpu.roll(xb, half, 1))
    return xb * cos + swapped * sin_signed


def _proj_a_kernel(x_ref, mod_ref, g_ref, w_ref, wvt_ref, tab_ref, q_ref, k_ref, vt_ref, *, qw, kw):
    m = mod_ref[0, 0]
    h = _norm_mod(x_ref[0], g_ref[...], m[0:1], m[1:2]).astype(BF16)
    qk = _dot(h, w_ref[...])
    lane = lax.broadcasted_iota(jnp.int32, (TM, LANES), 1)
    first = (lane % HEAD_DIM) < (HEAD_DIM // 2)
    cq, sq, ck, sk = tab_ref[0], tab_ref[1], tab_ref[2], tab_ref[3]
    for j in range(qw // LANES):
        blk = qk[:, j * LANES:(j + 1) * LANES]
        q_ref[0, :, j * LANES:(j + 1) * LANES] = _rope_block(blk, cq, sq, first, HEAD_DIM // 2).astype(BF16)
    for j in range(kw // LANES):
        blk = qk[:, qw + j * LANES:qw + (j + 1) * LANES]
        k_ref[0, :, j * LANES:(j + 1) * LANES] = _rope_block(blk, ck, sk, first, HEAD_DIM // 2).astype(BF16)
    vrow = lax.broadcasted_iota(jnp.int32, (wvt_ref.shape[0], 1), 0) % LANES
    vt_ref[0] = (_dot_nt(wvt_ref[...], h) + jnp.where(vrow == HEAD_DIM, 1.0, 0.0)).astype(BF16)


def _proj_a(xs, modl, g, w, wvt, tab, qw, kw):
    b, s_tot, d = xs.shape
    nt = s_tot // TM
    nv = wvt.shape[0]
    out = lambda n: jax.ShapeDtypeStruct((b, s_tot, n), BF16)
    ospec = lambda n: pl.BlockSpec((1, TM, n), lambda i, t: (i, t, 0))
    return pl.pallas_call(
        functools.partial(_proj_a_kernel, qw=qw, kw=kw),
        grid=(b, nt),
        in_specs=[pl.BlockSpec((1, TM, d), lambda i, t: (i, t, 0)),
                  pl.BlockSpec((1, 1, 6, d), lambda i, t: (i, jnp.minimum(t, 1), 0, 0)),
                  _resident(g.shape), _resident(w.shape), _resident(wvt.shape),
                  pl.BlockSpec((4, TM, LANES), lambda i, t: (0, t, 0))],
        out_specs=[ospec(qw), ospec(kw), pl.BlockSpec((1, nv, TM), lambda i, t: (i, 0, t))],
        out_shape=[out(qw), out(kw), jax.ShapeDtypeStruct((b, nv, s_tot), BF16)],
        compiler_params=_params(2),
        name="proj_gqa",
    )(xs, modl, g, w, wvt, tab)


def _proj_b_kernel(x_ref, mod_ref, g_ref, wd_ref, gq_ref, gkv_ref, wuq_ref, wuk_ref, wuvt_ref, tab_ref,
                   q_ref, k_ref, vt_ref, *, q_lora, kv_lora):
    m = mod_ref[0, 0]
    h = _norm_mod(x_ref[0], g_ref[...], m[0:1], m[1:2]).astype(BF16)
    d = _dot(h, wd_ref[...])
    cq = _rms(d[:, :q_lora], gq_ref[...]).astype(BF16)
    ckv = _rms(d[:, q_lora:q_lora + kv_lora], gkv_ref[...]).astype(BF16)
    lane = lax.broadcasted_iota(jnp.int32, (TM, LANES), 1)
    first = lane < (B_NOPE + B_ROPE // 2)
    tq_c, tq_s, tk_c, tk_s = tab_ref[0], tab_ref[1], tab_ref[2], tab_ref[3]
    k_rope = _rope_block(d[:, q_lora + kv_lora:], tk_c, tk_s, first, B_ROPE // 2)
    q = _dot(cq, wuq_ref[...])
    k = _dot(ckv, wuk_ref[...])
    for j in range(B_HEADS):
        sl = slice(j * LANES, (j + 1) * LANES)
        q_ref[0, :, sl] = _rope_block(q[:, sl], tq_c, tq_s, first, B_ROPE // 2).astype(BF16)
        k_ref[0, :, sl] = (k[:, sl] + k_rope).astype(BF16)
    vrow = lax.broadcasted_iota(jnp.int32, (wuvt_ref.shape[0], 1), 0) % LANES
    vt_ref[0] = (_dot_nt(wuvt_ref[...], ckv) + jnp.where(vrow == B_V, 1.0, 0.0)).astype(BF16)


def _proj_b(xs, modl, g, wd, gq, gkv, wuq, wuk, wuvt, tab):
    b, s_tot, d = xs.shape
    nt = s_tot // TM
    out = lambda n: jax.ShapeDtypeStruct((b, s_tot, n), BF16)
    ospec = lambda n: pl.BlockSpec((1, TM, n), lambda i, t: (i, t, 0))
    nq, nv = wuq.shape[1], wuvt.shape[0]
    return pl.pallas_call(
        functools.partial(_proj_b_kernel, q_lora=gq.shape[1], kv_lora=gkv.shape[1]),
        grid=(b, nt),
        in_specs=[pl.BlockSpec((1, TM, d), lambda i, t: (i, t, 0)),
                  pl.BlockSpec((1, 1, 6, d), lambda i, t: (i, jnp.minimum(t, 1), 0, 0)),
                  _resident(g.shape), _resident(wd.shape), _resident(gq.shape), _resident(gkv.shape),
                  _resident(wuq.shape), _resident(wuk.shape), _resident(wuvt.shape),
                  pl.BlockSpec((4, TM, LANES), lambda i, t: (0, t, 0))],
        out_specs=[ospec(nq), ospec(nq), pl.BlockSpec((1, nv, TM), lambda i, t: (i, 0, t))],
        out_shape=[out(nq), out(nq), jax.ShapeDtypeStruct((b, nv, s_tot), BF16)],
        compiler_params=_params(2),
        name="proj_mla",
    )(xs, modl, g, wd, gq, gkv, wuq, wuk, wuvt, tab)


def _attn_a_kernel(sink_ref, q_ref, kc_ref, kp_ref, kt_ref, kn_ref, vc_ref, vp_ref, vt_ref, vn_ref, o_ref,
                   *, ctx_tiles, n_tiles, t0):
    t = pl.program_id(1) + t0

    def bias(rows, lo_rel, hi_rel, ok):
        key = lax.broadcasted_iota(jnp.int32, (rows, TM), 0)
        qry = lax.broadcasted_iota(jnp.int32, (rows, TM), 1)
        rel = qry - key
        return jnp.where((rel >= lo_rel) & (rel <= hi_rel) & ok, 0.0, NEG).astype(F32)

    latent = t >= ctx_tiles
    b_prev = bias(WINDOW, -WINDOW, 0, t > ctx_tiles)
    b_cur = bias(TM, -WINDOW, WINDOW, latent)
    b_next = bias(WINDOW, WINDOW, TM, latent & (t < n_tiles - 1))
    lane = lax.broadcasted_iota(jnp.int32, (1, LANES), 1)
    lo = lane < HEAD_DIM
    zero = jnp.zeros((), BF16)

    def col_max(s):
        run = s[:MAX_FOLD]
        for i in range(1, s.shape[0] // MAX_FOLD):
            run = jnp.maximum(run, s[i * MAX_FOLD:(i + 1) * MAX_FOLD])
        return jnp.max(run, axis=0, keepdims=True)

    b_win = jnp.concatenate([b_prev, b_cur, b_next], axis=0)
    n_ctx = kc_ref.shape[1]
    n_heads = q_ref.shape[2] // HEAD_DIM
    group = n_heads // A_KV_HEADS

    def keys_of(j):
        sl = slice(j * LANES, (j + 1) * LANES)
        k = jnp.concatenate([kc_ref[0, :, sl], kp_ref[0, :, sl], kt_ref[0, :, sl], kn_ref[0, :, sl]], axis=0)
        v = jnp.concatenate([vc_ref[0, sl, :], vp_ref[0, sl, :], vt_ref[0, sl, :], vn_ref[0, sl, :]], axis=1)
        return k, v

    kv = [keys_of(j) for j in range(A_KV_HEADS)]

    def score(h):
        qp = q_ref[0, :, (h // 2) * LANES:(h // 2 + 1) * LANES]
        qe = jnp.where(lo, qp, zero) if h % 2 == 0 else jnp.where(lo, zero, qp)
        s = _dot_nt(kv[h // group][0], qe)
        s_ctx, s_win = s[:n_ctx], s[n_ctx:] + b_win
        sink = sink_ref[h] * LOG2E
        mx = jnp.maximum(jnp.maximum(col_max(s_ctx), col_max(s_win)), sink)
        return s_ctx, s_win, mx, sink

    def probs(st):
        s_ctx, s_win, mx, sink = st
        p = jnp.concatenate([jnp.exp2(s_ctx - mx), jnp.exp2(s_win - mx)], axis=0).astype(BF16)
        return p, jnp.exp2(sink - mx)

    def values(h, pr):
        p, p_sink = pr
        acc = _dot(kv[h // group][1], p)
        return acc[:HEAD_DIM] * (1.0 / (acc[HEAD_DIM:HEAD_DIM + 1] + p_sink))

    scored, weighted, outs = {}, {}, {}
    for step in range(n_heads + 2):
        if step < n_heads:
            scored[step] = score(step)
        if 0 <= step - 1 < n_heads:
            weighted[step - 1] = probs(scored.pop(step - 1))
        h = step - 2
        if 0 <= h < n_heads:
            outs[h] = values(h, weighted.pop(h))
            if h % 2 == 1:
                pair_t = jnp.concatenate([outs.pop(h - 1), outs.pop(h)], axis=0)
                o_ref[0, :, (h // 2) * LANES:(h // 2 + 1) * LANES] = pair_t.T.astype(BF16)


def _attn_a(q, k, vt, sink, seq, with_ctx):
    b, s_tot, qw = q.shape
    kw, nv = k.shape[2], vt.shape[1]
    ctx_tiles = (s_tot - seq) // TM
    t0 = 0 if with_ctx else ctx_tiles
    n_tiles = s_tot // TM
    nt = n_tiles - t0
    hb = TM // WINDOW
    n_half = s_tot // WINDOW
    prev = lambda t: jnp.maximum((t + t0) * hb - 1, 0)
    nxt = lambda t: jnp.minimum((t + t0 + 1) * hb, n_half - 1)
    k_specs = [pl.BlockSpec((1, TM, kw), lambda i, t: (i, 0, 0)),
               pl.BlockSpec((1, WINDOW, kw), lambda i, t: (i, prev(t), 0)),
               pl.BlockSpec((1, TM, kw), lambda i, t: (i, t + t0, 0)),
               pl.BlockSpec((1, WINDOW, kw), lambda i, t: (i, nxt(t), 0))]
    v_specs = [pl.BlockSpec((1, nv, TM), lambda i, t: (i, 0, 0)),
               pl.BlockSpec((1, nv, WINDOW), lambda i, t: (i, 0, prev(t))),
               pl.BlockSpec((1, nv, TM), lambda i, t: (i, 0, t + t0)),
               pl.BlockSpec((1, nv, WINDOW), lambda i, t: (i, 0, nxt(t)))]
    return pl.pallas_call(
        functools.partial(_attn_a_kernel, ctx_tiles=ctx_tiles, n_tiles=n_tiles, t0=t0),
        grid=(b, nt),
        in_specs=[pl.BlockSpec(memory_space=pltpu.SMEM),
                  pl.BlockSpec((1, TM, qw), lambda i, t: (i, t + t0, 0))] + k_specs + v_specs,
        out_specs=pl.BlockSpec((1, TM, qw), lambda i, t: (i, t + t0, 0)),
        out_shape=jax.ShapeDtypeStruct((b, s_tot, qw), BF16),
        compiler_params=_params(2),
        name="attn_gqa",
    )(sink, q, k, k, k, k, vt, vt, vt, vt)


def _attn_b_kernel(q_ref, k_ref, vt_ref, o_ref, s_ref, *, ctx_len, seq, t0):
    t = pl.program_id(2) + t0
    n_pairs = q_ref.shape[2] // (2 * LANES)
    slot = lambda h: h % s_ref.shape[0]

    def score_chunk(h, k0, tk, run):
        hs = slice(h * LANES, (h + 1) * LANES)
        s = _dot_nt(k_ref[0, k0:k0 + tk, hs], q_ref[0, :, hs])
        s_ref[slot(h), k0:k0 + tk, :] = s
        for i in range(tk // MAX_FOLD):
            run = jnp.maximum(run, s[i * MAX_FOLD:(i + 1) * MAX_FOLD])
        return run

    def value_chunk(h, k0, tk, mx, acc):
        p = jnp.exp2(s_ref[slot(h), k0:k0 + tk, :] - mx).astype(BF16)
        c = _dot(vt_ref[0, h * LANES:(h + 1) * LANES, k0:k0 + tk], p)
        return c if acc is None else acc + c

    def attend(chunks):
        init = jnp.full((MAX_FOLD, TM), NEG, F32)
        mx = None
        for pr in range(n_pairs + 1):
            run, acc = [init, init], [None, None]
            for k0, tk in chunks:
                for e in range(2):
                    if pr < n_pairs:
                        run[e] = score_chunk(2 * pr + e, k0, tk, run[e])
                    if pr > 0:
                        acc[e] = value_chunk(2 * (pr - 1) + e, k0, tk, mx[e], acc[e])
            if pr > 0:
                out_t = jnp.concatenate([a[:B_V] * (1.0 / a[B_V:B_V + 1]) for a in acc], axis=0)
                o_ref[0, :, (pr - 1) * LANES:pr * LANES] = out_t.T.astype(BF16)
            mx = [jnp.max(r, axis=0, keepdims=True) for r in run]

    ctx_chunk = [(0, ctx_len)]
    pl.when(t * TM < ctx_len)(lambda: attend(ctx_chunk))
    lat_chunks = [(ctx_len + i * TK_MLA, TK_MLA) for i in range(seq // TK_MLA)]
    pl.when(t * TM >= ctx_len)(lambda: attend(ctx_chunk + lat_chunks))


def _attn_b(q, k, vt, seq, with_ctx):
    b, s_tot, qw = q.shape
    ctx_len = s_tot - seq
    t0 = 0 if with_ctx else ctx_len // TM
    nt = s_tot // TM - t0
    heads = 2 * MLA_PAIRS
    wide = heads * LANES
    return pl.pallas_call(
        functools.partial(_attn_b_kernel, ctx_len=ctx_len, seq=seq, t0=t0),
        grid=(b, qw // wide, nt),
        in_specs=[pl.BlockSpec((1, TM, wide), lambda i, h, t: (i, t + t0, h)),
                  pl.BlockSpec((1, s_tot, wide), lambda i, h, t: (i, 0, h), pipeline_mode=pl.Buffered(1)),
                  pl.BlockSpec((1, wide, s_tot), lambda i, h, t: (i, h, 0), pipeline_mode=pl.Buffered(1))],
        out_specs=pl.BlockSpec((1, TM, heads * B_V), lambda i, h, t: (i, t + t0, h)),
        out_shape=jax.ShapeDtypeStruct((b, s_tot, B_HEADS * B_V), BF16),
        scratch_shapes=[pltpu.VMEM((min(heads, 4), s_tot, TM), F32)],
        compiler_params=_params(3),
        name="attn_mla",
    )(q, k, vt)


def _channel_kernel(xp_ref, xt_ref, xn_ref, op_ref, ot_ref, on_ref, mod_ref, g_ref, wo_ref, wa_ref, wv_ref,
                    cw_ref, cb_ref, wout_ref, gf_ref, y_ref, *, ctx_tiles, n_tiles, t0, final):
    t = pl.program_id(1) + t0
    ext = TM + 2 * HALO
    m = mod_ref[0, 0]
    x_ext = jnp.concatenate([xp_ref[0], xt_ref[0], xn_ref[0]], axis=0)
    o_ext = jnp.concatenate([op_ref[0], ot_ref[0], on_ref[0]], axis=0)
    x1 = x_ext + m[2:3] * _dot(o_ext, wo_ref[...])
    h2 = _norm_mod(x1, g_ref[...], m[3:4], m[4:5])
    prev_ok = (t != 0) & (t != ctx_tiles)
    next_ok = (t != ctx_tiles - 1) & (t != n_tiles - 1)
    row = lax.broadcasted_iota(jnp.int32, (ext, 1), 0)
    keep = ((row >= HALO) | prev_ok) & ((row < HALO + TM) | next_ok)
    h2 = jnp.where(keep, h2, 0.0).astype(BF16)
    h2_mid = h2[HALO:HALO + TM]
    cw = cw_ref[...]
    cb = cb_ref[...]
    d_ff = wa_ref.shape[1]
    y = None
    for c0 in range(0, d_ff, FF_CHUNK):
        cs = slice(c0, min(c0 + FF_CHUNK, d_ff))
        a = _dot(h2, wa_ref[:, cs])
        a_prev = pltpu.roll(a, 1, 0)[HALO:HALO + TM]
        a_next = pltpu.roll(a, ext - 1, 0)[HALO:HALO + TM]
        conv = a_prev * cw[0:1, cs] + a[HALO:HALO + TM] * cw[1:2, cs] + a_next * cw[2:3, cs] + cb[:, cs]
        gate = conv * (1.0 / (1.0 + jnp.exp(-conv)))
        hid = (gate * _dot(h2_mid, wv_ref[:, cs])).astype(BF16)
        part = _dot(hid, wout_ref[cs, :])
        y = part if y is None else y + part
    x2 = x1[HALO:HALO + TM] + m[5:6] * y
    y_ref[0] = _rms(x2, gf_ref[...]) if final else x2


def _channel(xs, o, modl, g, wo, wa, wv, cw, cb, wout, gf, seq, final):
    b, s_tot, d = xs.shape
    n_tiles = s_tot // TM
    ctx_tiles = (s_tot - seq) // TM
    t0 = ctx_tiles if final else 0
    nt = n_tiles - t0
    hb = TM // HALO
    n_halo = s_tot // HALO
    prev = lambda i, t: (i, jnp.maximum((t + t0) * hb - 1, t0 * hb), 0)
    cur = lambda i, t: (i, t + t0, 0)
    nxt = lambda i, t: (i, jnp.minimum((t + t0 + 1) * hb, n_halo - 1), 0)
    out_rows = seq if final else s_tot
    return pl.pallas_call(
        functools.partial(_channel_kernel, ctx_tiles=ctx_tiles, n_tiles=n_tiles, t0=t0, final=final),
        grid=(b, nt),
        in_specs=[pl.BlockSpec((1, HALO, d), prev), pl.BlockSpec((1, TM, d), cur), pl.BlockSpec((1, HALO, d), nxt),
                  pl.BlockSpec((1, HALO, d), prev), pl.BlockSpec((1, TM, d), cur), pl.BlockSpec((1, HALO, d), nxt),
                  pl.BlockSpec((1, 1, 6, d), lambda i, t: (i, jnp.minimum(t + t0, 1), 0, 0)),
                  _resident(g.shape), _resident(wo.shape), _resident(wa.shape), _resident(wv.shape),
                  _resident(cw.shape), _resident(cb.shape), _resident(wout.shape), _resident(gf.shape)],
        out_specs=pl.BlockSpec((1, TM, d), lambda i, t: (i, t, 0)),
        out_shape=jax.ShapeDtypeStruct((b, out_rows, d), F32),
        compiler_params=_params(2),
        name="channel",
    )(xs, xs, xs, o, o, o, modl, g, wo, wa, wv, cw, cb, wout, gf)


def _axial_angles(rows, rot_dim):
    row = jnp.repeat(jnp.arange(rows), GRID_W).astype(F32)
    col = jnp.tile(jnp.arange(GRID_W), rows).astype(F32)
    n_freq = rot_dim // 4
    inv = ROPE_BASE ** (-jnp.arange(n_freq, dtype=F32) / n_freq)
    return jnp.concatenate([row[:, None] * inv, col[:, None] * inv], axis=-1)


def _with_ctx_rows(cos, sin, ctx_len, q_scale):
    cos = jnp.concatenate([jnp.ones((ctx_len, LANES), F32), cos], axis=0)
    sin = jnp.concatenate([jnp.zeros((ctx_len, LANES), F32), sin], axis=0)
    return jnp.stack([cos * q_scale, sin * q_scale, cos, sin])


def _tables_a(seq, ctx_len):
    ang = _axial_angles(seq // GRID_W, HEAD_DIM)
    cos, sin = jnp.cos(ang), jnp.sin(ang)
    reps = LANES // HEAD_DIM
    cos_l = jnp.tile(cos, (1, 2 * reps))
    sin_l = jnp.tile(jnp.concatenate([-sin, sin], axis=1), (1, reps))
    return _with_ctx_rows(cos_l, sin_l, ctx_len, LOG2E * HEAD_DIM ** -0.5)


def _tables_b(seq, ctx_len):
    ang = _axial_angles(seq // GRID_W, B_ROPE)
    cos, sin = jnp.cos(ang), jnp.sin(ang)
    pad = LANES - B_NOPE - B_ROPE
    cos_l = jnp.concatenate([jnp.ones((seq, B_NOPE), F32), cos, cos, jnp.ones((seq, pad), F32)], axis=1)
    sin_l = jnp.concatenate([jnp.zeros((seq, B_NOPE), F32), -sin, sin, jnp.zeros((seq, pad), F32)], axis=1)
    return _with_ctx_rows(cos_l, sin_l, ctx_len, LOG2E * (B_NOPE + B_ROPE) ** -0.5)


def _deinterleave(n):
    return jnp.concatenate([jnp.arange(0, n, 2), jnp.arange(1, n, 2)])


def _weights_a(wqkv):
    d = wqkv.shape[0]
    kw = A_KV_HEADS * HEAD_DIM
    qw = wqkv.shape[1] - 2 * kw
    perm = _deinterleave(HEAD_DIM)
    wq = wqkv[:, :qw].reshape(d, -1, HEAD_DIM)[:, :, perm].reshape(d, qw)
    wk = wqkv[:, qw:qw + kw].reshape(d, A_KV_HEADS, HEAD_DIM)[:, :, perm]
    wkk = jnp.concatenate([wk, wk], axis=2).reshape(d, 2 * kw)
    wv = wqkv[:, qw + kw:].reshape(d, A_KV_HEADS, HEAD_DIM)
    wv = jnp.concatenate([wv, jnp.zeros((d, A_KV_HEADS, LANES - HEAD_DIM), F32)], axis=2)
    wvt = wv.reshape(d, A_KV_HEADS * LANES).T
    return jnp.concatenate([wq, wkk], axis=1).astype(BF16), wvt.astype(BF16), qw, 2 * kw


def _weights_b(wdown, wuq, wuk, wuv, q_lora, kv_lora):
    d = wdown.shape[0]
    perm = _deinterleave(B_ROPE)
    pad = LANES - B_NOPE - B_ROPE
    w_rope = wdown[:, q_lora + kv_lora:][:, perm]
    w_rope = jnp.concatenate([jnp.zeros((d, B_NOPE), F32), w_rope, jnp.zeros((d, pad), F32)], axis=1)
    wd = jnp.concatenate([wdown[:, :q_lora + kv_lora], w_rope], axis=1)
    uq = wuq.reshape(q_lora, B_HEADS, B_NOPE + B_ROPE)
    uq = jnp.concatenate([uq[:, :, :B_NOPE], uq[:, :, B_NOPE:][:, :, perm],
                          jnp.zeros((q_lora, B_HEADS, pad), F32)], axis=2)
    uk = wuk.reshape(kv_lora, B_HEADS, B_NOPE)
    uk = jnp.concatenate([uk, jnp.zeros((kv_lora, B_HEADS, LANES - B_NOPE), F32)], axis=2)
    uv = wuv.reshape(kv_lora, B_HEADS, B_V)
    uv = jnp.concatenate([uv, jnp.zeros((kv_lora, B_HEADS, LANES - B_V), F32)], axis=2)
    flat = lambda w: w.reshape(w.shape[0], B_HEADS * LANES).astype(BF16)
    return wd.astype(BF16), flat(uq), flat(uk), flat(uv).T


def kernel(x, c, ctx, c_ctx, mod_w, mod_b, norm1_g, norm2_g, a_wqkv, a_wo, a_sink, b_wdown, b_qnorm_g, b_wuq,
           b_kvnorm_g, b_wuk, b_wuv, b_wo, f_win, f_conv_w, f_conv_b, f_wout, final_g):
    bsz, seq, d = x.shape
    ctx_len = ctx.shape[1]
    depth = mod_w.shape[0]
    d_ff = f_wout.shape[1]
    assert ctx_len == TM and seq % TM == 0 and seq % TK_MLA == 0 and seq % GRID_W == 0

    cond_rows = -(-(bsz + 1) // 8) * 8
    cond = jnp.concatenate([c, c_ctx[None], jnp.zeros((cond_rows - bsz - 1, d), F32)], axis=0)
    mod = _modulation(cond, mod_w, mod_b)

    tab_a = _tables_a(seq, ctx_len)
    tab_b = _tables_b(seq, ctx_len)
    xs = jnp.concatenate([ctx, x], axis=1)
    row = lambda v: v.reshape(1, -1)

    for i in range(depth):
        last = i == depth - 1
        lat = mod[i, :bsz].reshape(bsz, 6, d)
        cmod = jnp.broadcast_to(mod[i, bsz].reshape(1, 6, d), (bsz, 6, d))
        modl = jnp.stack([cmod, lat], axis=1)
        j = i // 2
        if i % 2 == 0:
            w, wvt, qw, kw = _weights_a(a_wqkv[j])
            q, k, vt = _proj_a(xs, modl, row(norm1_g[i]), w, wvt, tab_a, qw, kw)
            o = _attn_a(q, k, vt, a_sink[j], seq, not last)
            wo = a_wo[j]
        else:
            q_lora, kv_lora = b_qnorm_g.shape[1], b_kvnorm_g.shape[1]
            wd, wuq, wuk, wuvt = _weights_b(b_wdown[j], b_wuq[j], b_wuk[j], b_wuv[j], q_lora, kv_lora)
            q, k, vt = _proj_b(xs, modl, row(norm1_g[i]), wd, row(b_qnorm_g[j]), row(b_kvnorm_g[j]),
                               wuq, wuk, wuvt, tab_b)
            o = _attn_b(q, k, vt, seq, not last)
            wo = b_wo[j]
        xs = _channel(xs, o, modl, row(norm2_g[i]), wo.astype(BF16), f_win[i][:, :d_ff].astype(BF16),
                      f_win[i][:, d_ff:].astype(BF16), f_conv_w[i], row(f_conv_b[i]), f_wout[i].astype(BF16),
                      row(final_g), seq, last)
    return xs
```

```python
import functools

import jax
import jax.numpy as jnp
from jax import lax
from jax.experimental import pallas as pl
from jax.experimental.pallas import tpu as pltpu

GRID_W = 64
HEAD_DIM = 64
A_KV_HEADS = 4
WINDOW = 128
B_HEADS = 16
B_NOPE = 64
B_ROPE = 32
B_V = 64
ROPE_BASE = 10000.0
EPS = 1e-6
NEG = -1e30
LOG2E = 1.4426950408889634

LANES = 128
TM = 256
HALO = 16
TK_MLA = 1024
MLA_PAIRS = 4
GQA_LAG = 3
MAX_FOLD = 64
FF_CHUNK = 768
VMEM_LIMIT = 56 * 1024 * 1024

F32 = jnp.float32
BF16 = jnp.bfloat16
_NT = (((1,), (1,)), ((), ()))


def _dot(a, b):
    return jnp.dot(a, b, preferred_element_type=F32)


def _dot_nt(a, b):
    return lax.dot_general(a, b, _NT, preferred_element_type=F32)


def _rms(xf, g):
    ms = jnp.mean(xf * xf, axis=-1, keepdims=True)
    return xf * lax.rsqrt(ms + EPS) * g


def _norm_mod(xf, g, shift, scale):
    return _rms(xf, g) * (1.0 + scale) + shift


def _params(n_axes):
    return pltpu.CompilerParams(dimension_semantics=("parallel",) * n_axes,
                                vmem_limit_bytes=VMEM_LIMIT)


def _resident(shape):
    nd = len(shape)
    return pl.BlockSpec(shape, lambda *_: (0,) * nd, pipeline_mode=pl.Buffered(1))


def _mod_kernel(c_ref, w_ref, b_ref, o_ref):
    c = c_ref[...]
    silu = c * (1.0 / (1.0 + jnp.exp(-c)))
    o_ref[0] = _dot(silu.astype(BF16), w_ref[0].astype(BF16)) + b_ref[0]


def _modulation(cond, mod_w, mod_b):
    depth, d, d6 = mod_w.shape
    rows = cond.shape[0]
    return pl.pallas_call(
        _mod_kernel,
        grid=(depth, d6 // d),
        in_specs=[pl.BlockSpec((rows, d), lambda i, j: (0, 0)),
                  pl.BlockSpec((1, d, d), lambda i, j: (i, 0, j)),
                  pl.BlockSpec((1, 1, d), lambda i, j: (i, 0, j))],
        out_specs=pl.BlockSpec((1, rows, d), lambda i, j: (i, 0, j)),
        out_shape=jax.ShapeDtypeStruct((depth, rows, d6), F32),
        compiler_params=_params(2),
        name="modulation",
    )(cond, mod_w, mod_b.reshape(depth, 1, d6))


def _rope_block(xb, cos, sin_signed, first_half, half):
    swapped = jnp.where(first_half, pltpu.roll(xb, LANES - half, 1), pltpu.roll(xb, half, 1))
    return xb * cos + swapped * sin_signed


def _proj_a_kernel(x_ref, mod_ref, g_ref, w_ref, wvt_ref, tab_ref, q_ref, k_ref, vt_ref, *, qw, kw):
    m = mod_ref[0, 0]
    h = _norm_mod(x_ref[0], g_ref[...], m[0:1], m[1:2]).astype(BF16)
    qk = _dot(h, w_ref[...])
    lane = lax.broadcasted_iota(jnp.int32, (TM, LANES), 1)
    first = (lane % HEAD_DIM) < (HEAD_DIM // 2)
    cq, sq, ck, sk = tab_ref[0], tab_ref[1], tab_ref[2], tab_ref[3]
    for j in range(qw // LANES):
        blk = qk[:, j * LANES:(j + 1) * LANES]
        q_ref[0, :, j * LANES:(j + 1) * LANES] = _rope_block(blk, cq, sq, first, HEAD_DIM // 2).astype(BF16)
    for j in range(kw // LANES):
        blk = qk[:, qw + j * LANES:qw + (j + 1) * LANES]
        k_ref[0, :, j * LANES:(j + 1) * LANES] = _rope_block(blk, ck, sk, first, HEAD_DIM // 2).astype(BF16)
    vrow = lax.broadcasted_iota(jnp.int32, (wvt_ref.shape[0], 1), 0) % LANES
    vt_ref[0] = (_dot_nt(wvt_ref[...], h) + jnp.where(vrow == HEAD_DIM, 1.0, 0.0)).astype(BF16)


def _proj_a(xs, modl, g, w, wvt, tab, qw, kw):
    b, s_tot, d = xs.shape
    nt = s_tot // TM
    nv = wvt.shape[0]
    out = lambda n: jax.ShapeDtypeStruct((b, s_tot, n), BF16)
    ospec = lambda n: pl.BlockSpec((1, TM, n), lambda i, t: (i, t, 0))
    return pl.pallas_call(
        functools.partial(_proj_a_kernel, qw=qw, kw=kw),
        grid=(b, nt),
        in_specs=[pl.BlockSpec((1, TM, d), lambda i, t: (i, t, 0)),
                  pl.BlockSpec((1, 1, 6, d), lambda i, t: (i, jnp.minimum(t, 1), 0, 0)),
                  _resident(g.shape), _resident(w.shape), _resident(wvt.shape),
                  pl.BlockSpec((4, TM, LANES), lambda i, t: (0, t, 0))],
        out_specs=[ospec(qw), ospec(kw), pl.BlockSpec((1, nv, TM), lambda i, t: (i, 0, t))],
        out_shape=[out(qw), out(kw), jax.ShapeDtypeStruct((b, nv, s_tot), BF16)],
        compiler_params=_params(2),
        name="proj_gqa",
    )(xs, modl, g, w, wvt, tab)


def _proj_b_kernel(x_ref, mod_ref, g_ref, wd_ref, gq_ref, gkv_ref, wuq_ref, wuk_ref, wuvt_ref, tab_ref,
                   q_ref, k_ref, vt_ref, *, q_lora, kv_lora):
    m = mod_ref[0, 0]
    h = _norm_mod(x_ref[0], g_ref[...], m[0:1], m[1:2]).astype(BF16)
    d = _dot(h, wd_ref[...])
    cq = _rms(d[:, :q_lora], gq_ref[...]).astype(BF16)
    ckv = _rms(d[:, q_lora:q_lora + kv_lora], gkv_ref[...]).astype(BF16)
    lane = lax.broadcasted_iota(jnp.int32, (TM, LANES), 1)
    first = lane < (B_NOPE + B_ROPE // 2)
    tq_c, tq_s, tk_c, tk_s = tab_ref[0], tab_ref[1], tab_ref[2], tab_ref[3]
    k_rope = _rope_block(d[:, q_lora + kv_lora:], tk_c, tk_s, first, B_ROPE // 2)
    q = _dot(cq, wuq_ref[...])
    k = _dot(ckv, wuk_ref[...])
    for j in range(B_HEADS):
        sl = slice(j * LANES, (j + 1) * LANES)
        q_ref[0, :, sl] = _rope_block(q[:, sl], tq_c, tq_s, first, B_ROPE // 2).astype(BF16)
        k_ref[0, :, sl] = (k[:, sl] + k_rope).astype(BF16)
    vrow = lax.broadcasted_iota(jnp.int32, (wuvt_ref.shape[0], 1), 0) % LANES
    vt_ref[0] = (_dot_nt(wuvt_ref[...], ckv) + jnp.where(vrow == B_V, 1.0, 0.0)).astype(BF16)


def _proj_b(xs, modl, g, wd, gq, gkv, wuq, wuk, wuvt, tab):
    b, s_tot, d = xs.shape
    nt = s_tot // TM
    out = lambda n: jax.ShapeDtypeStruct((b, s_tot, n), BF16)
    ospec = lambda n: pl.BlockSpec((1, TM, n), lambda i, t: (i, t, 0))
    nq, nv = wuq.shape[1], wuvt.shape[0]
    return pl.pallas_call(
        functools.partial(_proj_b_kernel, q_lora=gq.shape[1], kv_lora=gkv.shape[1]),
        grid=(b, nt),
        in_specs=[pl.BlockSpec((1, TM, d), lambda i, t: (i, t, 0)),
                  pl.BlockSpec((1, 1, 6, d), lambda i, t: (i, jnp.minimum(t, 1), 0, 0)),
                  _resident(g.shape), _resident(wd.shape), _resident(gq.shape), _resident(gkv.shape),
                  _resident(wuq.shape), _resident(wuk.shape), _resident(wuvt.shape),
                  pl.BlockSpec((4, TM, LANES), lambda i, t: (0, t, 0))],
        out_specs=[ospec(nq), ospec(nq), pl.BlockSpec((1, nv, TM), lambda i, t: (i, 0, t))],
        out_shape=[out(nq), out(nq), jax.ShapeDtypeStruct((b, nv, s_tot), BF16)],
        compiler_params=_params(2),
        name="proj_mla",
    )(xs, modl, g, wd, gq, gkv, wuq, wuk, wuvt, tab)


def _attn_a_kernel(sink_ref, q_ref, kc_ref, kp_ref, kt_ref, kn_ref, vc_ref, vp_ref, vt_ref, vn_ref, o_ref,
                   *, ctx_tiles, n_tiles, t0):
    t = pl.program_id(1) + t0

    def bias(rows, lo_rel, hi_rel, ok):
        key = lax.broadcasted_iota(jnp.int32, (rows, TM), 0)
        qry = lax.broadcasted_iota(jnp.int32, (rows, TM), 1)
        rel = qry - key
        return jnp.where((rel >= lo_rel) & (rel <= hi_rel) & ok, 0.0, NEG).astype(F32)

    latent = t >= ctx_tiles
    b_prev = bias(WINDOW, -WINDOW, 0, t > ctx_tiles)
    b_cur = bias(TM, -WINDOW, WINDOW, latent)
    b_next = bias(WINDOW, WINDOW, TM, latent & (t < n_tiles - 1))
    lane = lax.broadcasted_iota(jnp.int32, (1, LANES), 1)
    lo = lane < HEAD_DIM
    zero = jnp.zeros((), BF16)

    def col_max(s):
        run = s[:MAX_FOLD]
        for i in range(1, s.shape[0] // MAX_FOLD):
            run = jnp.maximum(run, s[i * MAX_FOLD:(i + 1) * MAX_FOLD])
        return jnp.max(run, axis=0, keepdims=True)

    b_win = jnp.concatenate([b_prev, b_cur, b_next], axis=0)
    n_ctx = kc_ref.shape[1]
    n_heads = q_ref.shape[2] // HEAD_DIM
    group = n_heads // A_KV_HEADS

    def keys_of(j):
        sl = slice(j * LANES, (j + 1) * LANES)
        k = jnp.concatenate([kc_ref[0, :, sl], kp_ref[0, :, sl], kt_ref[0, :, sl], kn_ref[0, :, sl]], axis=0)
        v = jnp.concatenate([vc_ref[0, sl, :], vp_ref[0, sl, :], vt_ref[0, sl, :], vn_ref[0, sl, :]], axis=1)
        return k, v

    kv = [keys_of(j) for j in range(A_KV_HEADS)]

    def score(h):
        qp = q_ref[0, :, (h // 2) * LANES:(h // 2 + 1) * LANES]
        qe = jnp.where(lo, qp, zero) if h % 2 == 0 else jnp.where(lo, zero, qp)
        s = _dot_nt(kv[h // group][0], qe)
        s_ctx, s_win = s[:n_ctx], s[n_ctx:] + b_win
        sink = sink_ref[h] * LOG2E
        mx = jnp.maximum(jnp.maximum(col_max(s_ctx), col_max(s_win)), sink)
        return s_ctx, s_win, mx, sink

    def probs(st):
        s_ctx, s_win, mx, sink = st
        p = jnp.concatenate([jnp.exp2(s_ctx - mx), jnp.exp2(s_win - mx)], axis=0).astype(BF16)
        return p, jnp.exp2(sink - mx)

    def values(h, pr):
        p, p_sink = pr
        acc = _dot(kv[h // group][1], p)
        return acc[:HEAD_DIM] * (1.0 / (acc[HEAD_DIM:HEAD_DIM + 1] + p_sink))

    scored, weighted, outs = {}, {}, {}
    for step in range(n_heads + 2 * GQA_LAG):
        h = step - 2 * GQA_LAG
        if 0 <= h < n_heads:
            outs[h] = values(h, weighted.pop(h))
            if h % 2 == 1:
                pair_t = jnp.concatenate([outs.pop(h - 1), outs.pop(h)], axis=0)
                o_ref[0, :, (h // 2) * LANES:(h // 2 + 1) * LANES] = pair_t.T.astype(BF16)
        if 0 <= step - GQA_LAG < n_heads:
            weighted[step - GQA_LAG] = probs(scored.pop(step - GQA_LAG))
        if step < n_heads:
            scored[step] = score(step)


def _attn_a(q, k, vt, sink, seq, with_ctx):
    b, s_tot, qw = q.shape
    kw, nv = k.shape[2], vt.shape[1]
    ctx_tiles = (s_tot - seq) // TM
    t0 = 0 if with_ctx else ctx_tiles
    n_tiles = s_tot // TM
    nt = n_tiles - t0
    hb = TM // WINDOW
    n_half = s_tot // WINDOW
    prev = lambda t: jnp.maximum((t + t0) * hb - 1, 0)
    nxt = lambda t: jnp.minimum((t + t0 + 1) * hb, n_half - 1)
    k_specs = [pl.BlockSpec((1, TM, kw), lambda i, t: (i, 0, 0)),
               pl.BlockSpec((1, WINDOW, kw), lambda i, t: (i, prev(t), 0)),
               pl.BlockSpec((1, TM, kw), lambda i, t: (i, t + t0, 0)),
               pl.BlockSpec((1, WINDOW, kw), lambda i, t: (i, nxt(t), 0))]
    v_specs = [pl.BlockSpec((1, nv, TM), lambda i, t: (i, 0, 0)),
               pl.BlockSpec((1, nv, WINDOW), lambda i, t: (i, 0, prev(t))),
               pl.BlockSpec((1, nv, TM), lambda i, t: (i, 0, t + t0)),
               pl.BlockSpec((1, nv, WINDOW), lambda i, t: (i, 0, nxt(t)))]
    return pl.pallas_call(
        functools.partial(_attn_a_kernel, ctx_tiles=ctx_tiles, n_tiles=n_tiles, t0=t0),
        grid=(b, nt),
        in_specs=[pl.BlockSpec(memory_space=pltpu.SMEM),
                  pl.BlockSpec((1, TM, qw), lambda i, t: (i, t + t0, 0))] + k_specs + v_specs,
        out_specs=pl.BlockSpec((1, TM, qw), lambda i, t: (i, t + t0, 0)),
        out_shape=jax.ShapeDtypeStruct((b, s_tot, qw), BF16),
        compiler_params=_params(2),
        name="attn_gqa",
    )(sink, q, k, k, k, k, vt, vt, vt, vt)


def _attn_b_kernel(q_ref, qn_ref, k_ref, vt_ref, o_ref, s_ref, run_ref, *, ctx_len, seq, n_tiles, t0):
    t = pl.program_id(2) + t0
    ctx_tiles = ctx_len // TM
    n_pairs = q_ref.shape[2] // (2 * LANES)
    slot = lambda h: h % s_ref.shape[0]
    assert (2 * n_pairs) % s_ref.shape[0] == 0

    def score_chunk(h, q_src, k0, tk, run):
        hs = slice(h * LANES, (h + 1) * LANES)
        s = _dot_nt(k_ref[0, k0:k0 + tk, hs], q_src[0, :, hs])
        s_ref[slot(h), k0:k0 + tk, :] = s
        for i in range(tk // MAX_FOLD):
            run = jnp.maximum(run, s[i * MAX_FOLD:(i + 1) * MAX_FOLD])
        return run

    def value_chunk(h, k0, tk, mx, acc):
        p = jnp.exp2(s_ref[slot(h), k0:k0 + tk, :] - mx).astype(BF16)
        c = _dot(vt_ref[0, h * LANES:(h + 1) * LANES, k0:k0 + tk], p)
        return c if acc is None else acc + c

    def attend(chunks, first_scored, score_next):
        init = jnp.full((MAX_FOLD, TM), NEG, F32)
        run = [run_ref[0], run_ref[1]] if first_scored else None
        for pr in range(1 if first_scored else 0, n_pairs + 1):
            mx = None if run is None else [jnp.max(r, axis=0, keepdims=True) for r in run]
            run, acc = [init, init], [None, None]
            for k0, tk in chunks:
                for e in range(2):
                    if pr < n_pairs:
                        run[e] = score_chunk(2 * pr + e, q_ref, k0, tk, run[e])
                    elif score_next:
                        run[e] = score_chunk(e, qn_ref, k0, tk, run[e])
                    if pr > 0:
                        acc[e] = value_chunk(2 * (pr - 1) + e, k0, tk, mx[e], acc[e])
            if pr > 0:
                out_t = jnp.concatenate([a[:B_V] * (1.0 / a[B_V:B_V + 1]) for a in acc], axis=0)
                o_ref[0, :, (pr - 1) * LANES:pr * LANES] = out_t.T.astype(BF16)
        if score_next:
            run_ref[0], run_ref[1] = run

    ctx_chunk = [(0, ctx_len)]
    all_chunks = ctx_chunk + [(ctx_len + i * TK_MLA, TK_MLA) for i in range(seq // TK_MLA)]
    pl.when(t < ctx_tiles)(lambda: attend(ctx_chunk, False, False))
    pl.when(t == ctx_tiles)(lambda: attend(all_chunks, False, ctx_tiles < n_tiles - 1))
    pl.when((t > ctx_tiles) & (t < n_tiles - 1))(lambda: attend(all_chunks, True, True))
    pl.when((t > ctx_tiles) & (t == n_tiles - 1))(lambda: attend(all_chunks, True, False))


def _attn_b(q, k, vt, seq, with_ctx):
    b, s_tot, qw = q.shape
    ctx_len = s_tot - seq
    n_tiles = s_tot // TM
    t0 = 0 if with_ctx else ctx_len // TM
    nt = n_tiles - t0
    heads = 2 * MLA_PAIRS
    wide = heads * LANES
    pair = 2 * LANES
    return pl.pallas_call(
        functools.partial(_attn_b_kernel, ctx_len=ctx_len, seq=seq, n_tiles=n_tiles, t0=t0),
        grid=(b, qw // wide, nt),
        in_specs=[pl.BlockSpec((1, TM, wide), lambda i, h, t: (i, t + t0, h)),
                  pl.BlockSpec((1, TM, pair), lambda i, h, t: (i, jnp.minimum(t + t0 + 1, n_tiles - 1), h * MLA_PAIRS)),
                  pl.BlockSpec((1, s_tot, wide), lambda i, h, t: (i, 0, h), pipeline_mode=pl.Buffered(1)),
                  pl.BlockSpec((1, wide, s_tot), lambda i, h, t: (i, h, 0), pipeline_mode=pl.Buffered(1))],
        out_specs=pl.BlockSpec((1, TM, heads * B_V), lambda i, h, t: (i, t + t0, h)),
        out_shape=jax.ShapeDtypeStruct((b, s_tot, B_HEADS * B_V), BF16),
        scratch_shapes=[pltpu.VMEM((min(heads, 4), s_tot, TM), F32), pltpu.VMEM((2, MAX_FOLD, TM), F32)],
        compiler_params=pltpu.CompilerParams(dimension_semantics=("parallel", "parallel", "arbitrary"),
                                             vmem_limit_bytes=VMEM_LIMIT),
        name="attn_mla",
    )(q, q, k, vt)


def _channel_kernel(xp_ref, xt_ref, xn_ref, op_ref, ot_ref, on_ref, mod_ref, g_ref, wo_ref, wa_ref, wv_ref,
                    cw_ref, cb_ref, wout_ref, gf_ref, y_ref, *, ctx_tiles, n_tiles, t0, final):
    t = pl.program_id(1) + t0
    ext = TM + 2 * HALO
    m = mod_ref[0, 0]
    x_ext = jnp.concatenate([xp_ref[0], xt_ref[0], xn_ref[0]], axis=0)
    o_ext = jnp.concatenate([op_ref[0], ot_ref[0], on_ref[0]], axis=0)
    x1 = x_ext + m[2:3] * _dot(o_ext, wo_ref[...])
    h2 = _norm_mod(x1, g_ref[...], m[3:4], m[4:5])
    prev_ok = (t != 0) & (t != ctx_tiles)
    next_ok = (t != ctx_tiles - 1) & (t != n_tiles - 1)
    row = lax.broadcasted_iota(jnp.int32, (ext, 1), 0)
    keep = ((row >= HALO) | prev_ok) & ((row < HALO + TM) | next_ok)
    h2 = jnp.where(keep, h2, 0.0).astype(BF16)
    h2_mid = h2[HALO:HALO + TM]
    cw = cw_ref[...]
    cb = cb_ref[...]
    d_ff = wa_ref.shape[1]
    y = None
    for c0 in range(0, d_ff, FF_CHUNK):
        cs = slice(c0, min(c0 + FF_CHUNK, d_ff))
        a = _dot(h2, wa_ref[:, cs])
        a_prev = pltpu.roll(a, 1, 0)[HALO:HALO + TM]
        a_next = pltpu.roll(a, ext - 1, 0)[HALO:HALO + TM]
        conv = a_prev * cw[0:1, cs] + a[HALO:HALO + TM] * cw[1:2, cs] + a_next * cw[2:3, cs] + cb[:, cs]
        gate = conv * (1.0 / (1.0 + jnp.exp(-conv)))
        hid = (gate * _dot(h2_mid, wv_ref[:, cs])).astype(BF16)
        part = _dot(hid, wout_ref[cs, :])
        y = part if y is None else y + part
    x2 = x1[HALO:HALO + TM] + m[5:6] * y
    y_ref[0] = _rms(x2, gf_ref[...]) if final else x2


def _channel(xs, o, modl, g, wo, wa, wv, cw, cb, wout, gf, seq, final):
    b, s_tot, d = xs.shape
    n_tiles = s_tot // TM
    ctx_tiles = (s_tot - seq) // TM
    t0 = ctx_tiles if final else 0
    nt = n_tiles - t0
    hb = TM // HALO
    n_halo = s_tot // HALO
    prev = lambda i, t: (i, jnp.maximum((t + t0) * hb - 1, t0 * hb), 0)
    cur = lambda i, t: (i, t + t0, 0)
    nxt = lambda i, t: (i, jnp.minimum((t + t0 + 1) * hb, n_halo - 1), 0)
    out_rows = seq if final else s_tot
    return pl.pallas_call(
        functools.partial(_channel_kernel, ctx_tiles=ctx_tiles, n_tiles=n_tiles, t0=t0, final=final),
        grid=(b, nt),
        in_specs=[pl.BlockSpec((1, HALO, d), prev), pl.BlockSpec((1, TM, d), cur), pl.BlockSpec((1, HALO, d), nxt),
                  pl.BlockSpec((1, HALO, d), prev), pl.BlockSpec((1, TM, d), cur), pl.BlockSpec((1, HALO, d), nxt),
                  pl.BlockSpec((1, 1, 6, d), lambda i, t: (i, jnp.minimum(t + t0, 1), 0, 0)),
                  _resident(g.shape), _resident(wo.shape), _resident(wa.shape), _resident(wv.shape),
                  _resident(cw.shape), _resident(cb.shape), _resident(wout.shape), _resident(gf.shape)],
        out_specs=pl.BlockSpec((1, TM, d), lambda i, t: (i, t, 0)),
        out_shape=jax.ShapeDtypeStruct((b, out_rows, d), F32),
        compiler_params=_params(2),
        name="channel",
    )(xs, xs, xs, o, o, o, modl, g, wo, wa, wv, cw, cb, wout, gf)


def _axial_angles(rows, rot_dim):
    row = jnp.repeat(jnp.arange(rows), GRID_W).astype(F32)
    col = jnp.tile(jnp.arange(GRID_W), rows).astype(F32)
    n_freq = rot_dim // 4
    inv = ROPE_BASE ** (-jnp.arange(n_freq, dtype=F32) / n_freq)
    return jnp.concatenate([row[:, None] * inv, col[:, None] * inv], axis=-1)


def _with_ctx_rows(cos, sin, ctx_len, q_scale):
    cos = jnp.concatenate([jnp.ones((ctx_len, LANES), F32), cos], axis=0)
    sin = jnp.concatenate([jnp.zeros((ctx_len, LANES), F32), sin], axis=0)
    return jnp.stack([cos * q_scale, sin * q_scale, cos, sin])


def _tables_a(seq, ctx_len):
    ang = _axial_angles(seq // GRID_W, HEAD_DIM)
    cos, sin = jnp.cos(ang), jnp.sin(ang)
    reps = LANES // HEAD_DIM
    cos_l = jnp.tile(cos, (1, 2 * reps))
    sin_l = jnp.tile(jnp.concatenate([-sin, sin], axis=1), (1, reps))
    return _with_ctx_rows(cos_l, sin_l, ctx_len, LOG2E * HEAD_DIM ** -0.5)


def _tables_b(seq, ctx_len):
    ang = _axial_angles(seq // GRID_W, B_ROPE)
    cos, sin = jnp.cos(ang), jnp.sin(ang)
    pad = LANES - B_NOPE - B_ROPE
    cos_l = jnp.concatenate([jnp.ones((seq, B_NOPE), F32), cos, cos, jnp.ones((seq, pad), F32)], axis=1)
    sin_l = jnp.concatenate([jnp.zeros((seq, B_NOPE), F32), -sin, sin, jnp.zeros((seq, pad), F32)], axis=1)
    return _with_ctx_rows(cos_l, sin_l, ctx_len, LOG2E * (B_NOPE + B_ROPE) ** -0.5)


def _deinterleave(n):
    return jnp.concatenate([jnp.arange(0, n, 2), jnp.arange(1, n, 2)])


def _weights_a(wqkv):
    d = wqkv.shape[0]
    kw = A_KV_HEADS * HEAD_DIM
    qw = wqkv.shape[1] - 2 * kw
    perm = _deinterleave(HEAD_DIM)
    wq = wqkv[:, :qw].reshape(d, -1, HEAD_DIM)[:, :, perm].reshape(d, qw)
    wk = wqkv[:, qw:qw + kw].reshape(d, A_KV_HEADS, HEAD_DIM)[:, :, perm]
    wkk = jnp.concatenate([wk, wk], axis=2).reshape(d, 2 * kw)
    wv = wqkv[:, qw + kw:].reshape(d, A_KV_HEADS, HEAD_DIM)
    wv = jnp.concatenate([wv, jnp.zeros((d, A_KV_HEADS, LANES - HEAD_DIM), F32)], axis=2)
    wvt = wv.reshape(d, A_KV_HEADS * LANES).T
    return jnp.concatenate([wq, wkk], axis=1).astype(BF16), wvt.astype(BF16), qw, 2 * kw


def _weights_b(wdown, wuq, wuk, wuv, q_lora, kv_lora):
    d = wdown.shape[0]
    perm = _deinterleave(B_ROPE)
    pad = LANES - B_NOPE - B_ROPE
    w_rope = wdown[:, q_lora + kv_lora:][:, perm]
    w_rope = jnp.concatenate([jnp.zeros((d, B_NOPE), F32), w_rope, jnp.zeros((d, pad), F32)], axis=1)
    wd = jnp.concatenate([wdown[:, :q_lora + kv_lora], w_rope], axis=1)
    uq = wuq.reshape(q_lora, B_HEADS, B_NOPE + B_ROPE)
    uq = jnp.concatenate([uq[:, :, :B_NOPE], uq[:, :, B_NOPE:][:, :, perm],
                          jnp.zeros((q_lora, B_HEADS, pad), F32)], axis=2)
    uk = wuk.reshape(kv_lora, B_HEADS, B_NOPE)
    uk = jnp.concatenate([uk, jnp.zeros((kv_lora, B_HEADS, LANES - B_NOPE), F32)], axis=2)
    uv = wuv.reshape(kv_lora, B_HEADS, B_V)
    uv = jnp.concatenate([uv, jnp.zeros((kv_lora, B_HEADS, LANES - B_V), F32)], axis=2)
    flat = lambda w: w.reshape(w.shape[0], B_HEADS * LANES).astype(BF16)
    return wd.astype(BF16), flat(uq), flat(uk), flat(uv).T


def kernel(x, c, ctx, c_ctx, mod_w, mod_b, norm1_g, norm2_g, a_wqkv, a_wo, a_sink, b_wdown, b_qnorm_g, b_wuq,
           b_kvnorm_g, b_wuk, b_wuv, b_wo, f_win, f_conv_w, f_conv_b, f_wout, final_g):
    bsz, seq, d = x.shape
    ctx_len = ctx.shape[1]
    depth = mod_w.shape[0]
    d_ff = f_wout.shape[1]
    assert ctx_len == TM and seq % TM == 0 and seq % TK_MLA == 0 and seq % GRID_W == 0

    cond_rows = -(-(bsz + 1) // 8) * 8
    cond = jnp.concatenate([c, c_ctx[None], jnp.zeros((cond_rows - bsz - 1, d), F32)], axis=0)
    mod = _modulation(cond, mod_w, mod_b)

    tab_a = _tables_a(seq, ctx_len)
    tab_b = _tables_b(seq, ctx_len)
    xs = jnp.concatenate([ctx, x], axis=1)
    row = lambda v: v.reshape(1, -1)

    for i in range(depth):
        last = i == depth - 1
        lat = mod[i, :bsz].reshape(bsz, 6, d)
        cmod = jnp.broadcast_to(mod[i, bsz].reshape(1, 6, d), (bsz, 6, d))
        modl = jnp.stack([cmod, lat], axis=1)
        j = i // 2
        if i % 2 == 0:
            w, wvt, qw, kw = _weights_a(a_wqkv[j])
            q, k, vt = _proj_a(xs, modl, row(norm1_g[i]), w, wvt, tab_a, qw, kw)
            o = _attn_a(q, k, vt, a_sink[j], seq, not last)
            wo = a_wo[j]
        else:
            q_lora, kv_lora = b_qnorm_g.shape[1], b_kvnorm_g.shape[1]
            wd, wuq, wuk, wuvt = _weights_b(b_wdown[j], b_wuq[j], b_wuk[j], b_wuv[j], q_lora, kv_lora)
            q, k, vt = _proj_b(xs, modl, row(norm1_g[i]), wd, row(b_qnorm_g[j]), row(b_kvnorm_g[j]),
                               wuq, wuk, wuvt, tab_b)
            o = _attn_b(q, k, vt, seq, not last)
            wo = b_wo[j]
        xs = _channel(xs, o, modl, row(norm2_g[i]), wo.astype(BF16), f_win[i][:, :d_ff].astype(BF16),
                      f_win[i][:, d_ff:].astype(BF16), f_conv_w[i], row(f_conv_b[i]), f_wout[i].astype(BF16),
                      row(final_g), seq, last)
    return xs
```

```python
import functools

import jax
import jax.numpy as jnp
from jax import lax
from jax.experimental import pallas as pl
from jax.experimental.pallas import tpu as pltpu

GRID_W = 64
HEAD_DIM = 64
A_KV_HEADS = 4
WINDOW = 128
B_HEADS = 16
B_NOPE = 64
B_ROPE = 32
B_V = 64
ROPE_BASE = 10000.0
EPS = 1e-6
NEG = -1e30
LOG2E = 1.4426950408889634

LANES = 128
TM = 256
HALO = 16
TK_MLA = 1024
MLA_PAIRS = 4
GQA_LAG = 3
MAX_FOLD = 64
FF_CHUNK = 768
VMEM_LIMIT = 56 * 1024 * 1024

F32 = jnp.float32
BF16 = jnp.bfloat16
_NT = (((1,), (1,)), ((), ()))


def _dot(a, b):
    return jnp.dot(a, b, preferred_element_type=F32)


def _dot_nt(a, b):
    return lax.dot_general(a, b, _NT, preferred_element_type=F32)


def _rms(xf, g):
    ms = jnp.mean(xf * xf, axis=-1, keepdims=True)
    return xf * lax.rsqrt(ms + EPS) * g


def _norm_mod(xf, g, shift, scale):
    return _rms(xf, g) * (1.0 + scale) + shift


def _params(n_axes):
    return pltpu.CompilerParams(dimension_semantics=("parallel",) * n_axes,
                                vmem_limit_bytes=VMEM_LIMIT)


def _resident(shape):
    nd = len(shape)
    return pl.BlockSpec(shape, lambda *_: (0,) * nd, pipeline_mode=pl.Buffered(1))


def _mod_kernel(c_ref, w_ref, b_ref, o_ref):
    c = c_ref[...]
    silu = c * (1.0 / (1.0 + jnp.exp(-c)))
    o_ref[0] = _dot(silu.astype(BF16), w_ref[0].astype(BF16)) + b_ref[0]


def _modulation(cond, mod_w, mod_b):
    depth, d, d6 = mod_w.shape
    rows = cond.shape[0]
    return pl.pallas_call(
        _mod_kernel,
        grid=(depth, d6 // d),
        in_specs=[pl.BlockSpec((rows, d), lambda i, j: (0, 0)),
                  pl.BlockSpec((1, d, d), lambda i, j: (i, 0, j)),
                  pl.BlockSpec((1, 1, d), lambda i, j: (i, 0, j))],
        out_specs=pl.BlockSpec((1, rows, d), lambda i, j: (i, 0, j)),
        out_shape=jax.ShapeDtypeStruct((depth, rows, d6), F32),
        compiler_params=_params(2),
        name="modulation",
    )(cond, mod_w, mod_b.reshape(depth, 1, d6))


def _rope_block(xb, cos, sin_signed, first_half, half):
    swapped = jnp.where(first_half, pltpu.roll(xb, LANES - half, 1), pltpu.roll(xb, half, 1))
    return xb * cos + swapped * sin_signed


def _rope_block_dup(xb, cos, sin_signed, half):
    return xb * cos + pltpu.roll(xb, LANES - half, 1) * sin_signed


def _proj_a_kernel(x_ref, mod_ref, g_ref, w_ref, wvt_ref, tab_ref, q_ref, k_ref, vt_ref, *, qw, kw):
    m = mod_ref[0, 0]
    h = _norm_mod(x_ref[0], g_ref[...], m[0:1], m[1:2]).astype(BF16)
    qk = _dot(h, w_ref[...])
    lane = lax.broadcasted_iota(jnp.int32, (TM, LANES), 1)
    first = (lane % HEAD_DIM) < (HEAD_DIM // 2)
    cq, sq, ck, sk = tab_ref[0], tab_ref[1], tab_ref[2], tab_ref[3]
    for j in range(qw // LANES):
        blk = qk[:, j * LANES:(j + 1) * LANES]
        q_ref[0, :, j * LANES:(j + 1) * LANES] = _rope_block(blk, cq, sq, first, HEAD_DIM // 2).astype(BF16)
    for j in range(kw // LANES):
        blk = qk[:, qw + j * LANES:qw + (j + 1) * LANES]
        k_ref[0, :, j * LANES:(j + 1) * LANES] = _rope_block_dup(blk, ck, sk, HEAD_DIM // 2).astype(BF16)
    vrow = lax.broadcasted_iota(jnp.int32, (wvt_ref.shape[0], 1), 0) % LANES
    vt_ref[0] = (_dot_nt(wvt_ref[...], h) + jnp.where(vrow == HEAD_DIM, 1.0, 0.0)).astype(BF16)


def _proj_a(xs, modl, g, w, wvt, tab, qw, kw):
    b, s_tot, d = xs.shape
    nt = s_tot // TM
    nv = wvt.shape[0]
    out = lambda n: jax.ShapeDtypeStruct((b, s_tot, n), BF16)
    ospec = lambda n: pl.BlockSpec((1, TM, n), lambda i, t: (i, t, 0))
    return pl.pallas_call(
        functools.partial(_proj_a_kernel, qw=qw, kw=kw),
        grid=(b, nt),
        in_specs=[pl.BlockSpec((1, TM, d), lambda i, t: (i, t, 0)),
                  pl.BlockSpec((1, 1, 6, d), lambda i, t: (i, jnp.minimum(t, 1), 0, 0)),
                  _resident(g.shape), _resident(w.shape), _resident(wvt.shape),
                  pl.BlockSpec((4, TM, LANES), lambda i, t: (0, t, 0))],
        out_specs=[ospec(qw), ospec(kw), pl.BlockSpec((1, nv, TM), lambda i, t: (i, 0, t))],
        out_shape=[out(qw), out(kw), jax.ShapeDtypeStruct((b, nv, s_tot), BF16)],
        compiler_params=_params(2),
        name="proj_gqa",
    )(xs, modl, g, w, wvt, tab)


def _proj_b_kernel(x_ref, mod_ref, g_ref, wd_ref, gq_ref, gkv_ref, wuq_ref, wuk_ref, wuvt_ref, tab_ref,
                   q_ref, k_ref, vt_ref, *, q_lora, kv_lora):
    m = mod_ref[0, 0]
    h = _norm_mod(x_ref[0], g_ref[...], m[0:1], m[1:2]).astype(BF16)
    d = _dot(h, wd_ref[...])
    cq = _rms(d[:, :q_lora], gq_ref[...]).astype(BF16)
    ckv = _rms(d[:, q_lora:q_lora + kv_lora], gkv_ref[...]).astype(BF16)
    tq_c, tq_s, tk_c, tk_s = tab_ref[0], tab_ref[1], tab_ref[2], tab_ref[3]
    k_rope = _rope_block_dup(d[:, q_lora + kv_lora:], tk_c, tk_s, B_ROPE // 2)
    q = _dot(cq, wuq_ref[...])
    k = _dot(ckv, wuk_ref[...])
    for j in range(B_HEADS):
        sl = slice(j * LANES, (j + 1) * LANES)
        q_ref[0, :, sl] = _rope_block_dup(q[:, sl], tq_c, tq_s, B_ROPE // 2).astype(BF16)
        k_ref[0, :, sl] = (k[:, sl] + k_rope).astype(BF16)
    vrow = lax.broadcasted_iota(jnp.int32, (wuvt_ref.shape[0], 1), 0) % LANES
    vt_ref[0] = (_dot_nt(wuvt_ref[...], ckv) + jnp.where(vrow == B_V, 1.0, 0.0)).astype(BF16)


def _proj_b(xs, modl, g, wd, gq, gkv, wuq, wuk, wuvt, tab):
    b, s_tot, d = xs.shape
    nt = s_tot // TM
    out = lambda n: jax.ShapeDtypeStruct((b, s_tot, n), BF16)
    ospec = lambda n: pl.BlockSpec((1, TM, n), lambda i, t: (i, t, 0))
    nq, nv = wuq.shape[1], wuvt.shape[0]
    return pl.pallas_call(
        functools.partial(_proj_b_kernel, q_lora=gq.shape[1], kv_lora=gkv.shape[1]),
        grid=(b, nt),
        in_specs=[pl.BlockSpec((1, TM, d), lambda i, t: (i, t, 0)),
                  pl.BlockSpec((1, 1, 6, d), lambda i, t: (i, jnp.minimum(t, 1), 0, 0)),
                  _resident(g.shape), _resident(wd.shape), _resident(gq.shape), _resident(gkv.shape),
                  _resident(wuq.shape), _resident(wuk.shape), _resident(wuvt.shape),
                  pl.BlockSpec((4, TM, LANES), lambda i, t: (0, t, 0))],
        out_specs=[ospec(nq), ospec(nq), pl.BlockSpec((1, nv, TM), lambda i, t: (i, 0, t))],
        out_shape=[out(nq), out(nq), jax.ShapeDtypeStruct((b, nv, s_tot), BF16)],
        compiler_params=_params(2),
        name="proj_mla",
    )(xs, modl, g, wd, gq, gkv, wuq, wuk, wuvt, tab)


def _attn_a_kernel(sink_ref, q_ref, kc_ref, kp_ref, kt_ref, kn_ref, vc_ref, vp_ref, vt_ref, vn_ref, o_ref,
                   *, ctx_tiles, n_tiles, t0):
    t = pl.program_id(1) + t0
    assert TM == 2 * WINDOW
    edge = WINDOW
    n_heads = q_ref.shape[2] // HEAD_DIM
    group = n_heads // A_KV_HEADS
    lane = lax.broadcasted_iota(jnp.int32, (1, LANES), 1)
    lo = lane < HEAD_DIM
    zero = jnp.zeros((), BF16)

    key = lax.broadcasted_iota(jnp.int32, (edge, edge), 0)
    qry = lax.broadcasted_iota(jnp.int32, (edge, edge), 1)
    tri = lambda valid: jnp.where(valid, 0.0, NEG).astype(F32)
    ALL, NONE = "all", "none"
    ctx_layout = [(ALL, ALL)] * (kc_ref.shape[1] // edge)
    win_layout = [(tri((qry <= key) & (t > ctx_tiles)), NONE),
                  (ALL, tri(qry <= key)), (tri(qry >= key), ALL),
                  (NONE, tri((qry >= key) & (t < n_tiles - 1)))]

    def attend(layout, k_of, v_of):
        def score(h):
            qp = q_ref[0, :, (h // 2) * LANES:(h // 2 + 1) * LANES]
            qe = jnp.where(lo, qp, zero) if h % 2 == 0 else jnp.where(lo, zero, qp)
            s = _dot_nt(k_of(h // group), qe)
            sink = sink_ref[h] * LOG2E
            halves = []
            for half in range(2):
                blocks, run = [], None
                for rb, kinds in enumerate(layout):
                    kind = kinds[half]
                    if isinstance(kind, str) and kind == NONE:
                        blocks.append(None)
                        continue
                    blk = s[rb * edge:(rb + 1) * edge, half * edge:(half + 1) * edge]
                    if not isinstance(kind, str):
                        blk = blk + kind
                    blocks.append(blk)
                    for i in range(edge // MAX_FOLD):
                        part = blk[i * MAX_FOLD:(i + 1) * MAX_FOLD]
                        run = part if run is None else jnp.maximum(run, part)
                halves.append((blocks, jnp.maximum(jnp.max(run, axis=0, keepdims=True), sink)))
            return halves, sink

        def probs(st):
            halves, sink = st
            rows = []
            for rb in range(len(layout)):
                parts = [jnp.zeros((edge, edge), BF16) if blocks[rb] is None
                         else jnp.exp2(blocks[rb] - mx).astype(BF16) for blocks, mx in halves]
                rows.append(jnp.concatenate(parts, axis=1))
            p_sink = jnp.concatenate([jnp.exp2(sink - mx) for _, mx in halves], axis=1)
            return jnp.concatenate(rows, axis=0), p_sink

        def values(h, pr):
            p, p_sink = pr
            acc = _dot(v_of(h // group), p)
            return acc[:HEAD_DIM] * (1.0 / (acc[HEAD_DIM:HEAD_DIM + 1] + p_sink))

        scored, weighted, outs = {}, {}, {}
        for step in range(n_heads + 2 * GQA_LAG):
            h = step - 2 * GQA_LAG
            if 0 <= h < n_heads:
                outs[h] = values(h, weighted.pop(h))
                if h % 2 == 1:
                    pair_t = jnp.concatenate([outs.pop(h - 1), outs.pop(h)], axis=0)
                    o_ref[0, :, (h // 2) * LANES:(h // 2 + 1) * LANES] = pair_t.T.astype(BF16)
            if 0 <= step - GQA_LAG < n_heads:
                weighted[step - GQA_LAG] = probs(scored.pop(step - GQA_LAG))
            if step < n_heads:
                scored[step] = score(step)

    def with_kv(body, k_refs, v_refs):
        ks, vs = [], []
        for j in range(A_KV_HEADS):
            sl = slice(j * LANES, (j + 1) * LANES)
            ks.append(jnp.concatenate([r[0, :, sl] for r in k_refs], axis=0))
            vs.append(jnp.concatenate([r[0, sl, :] for r in v_refs], axis=1))
        body(ks.__getitem__, vs.__getitem__)

    @pl.when(t < ctx_tiles)
    def _():
        with_kv(functools.partial(attend, ctx_layout), [kc_ref], [vc_ref])

    @pl.when(t >= ctx_tiles)
    def _():
        with_kv(functools.partial(attend, ctx_layout + win_layout),
                [kc_ref, kp_ref, kt_ref, kn_ref], [vc_ref, vp_ref, vt_ref, vn_ref])


def _attn_a(q, k, vt, sink, seq, with_ctx):
    b, s_tot, qw = q.shape
    kw, nv = k.shape[2], vt.shape[1]
    ctx_tiles = (s_tot - seq) // TM
    t0 = 0 if with_ctx else ctx_tiles
    n_tiles = s_tot // TM
    nt = n_tiles - t0
    hb = TM // WINDOW
    n_half = s_tot // WINDOW
    prev = lambda t: jnp.maximum((t + t0) * hb - 1, 0)
    nxt = lambda t: jnp.minimum((t + t0 + 1) * hb, n_half - 1)
    k_specs = [pl.BlockSpec((1, TM, kw), lambda i, t: (i, 0, 0)),
               pl.BlockSpec((1, WINDOW, kw), lambda i, t: (i, prev(t), 0)),
               pl.BlockSpec((1, TM, kw), lambda i, t: (i, t + t0, 0)),
               pl.BlockSpec((1, WINDOW, kw), lambda i, t: (i, nxt(t), 0))]
    v_specs = [pl.BlockSpec((1, nv, TM), lambda i, t: (i, 0, 0)),
               pl.BlockSpec((1, nv, WINDOW), lambda i, t: (i, 0, prev(t))),
               pl.BlockSpec((1, nv, TM), lambda i, t: (i, 0, t + t0)),
               pl.BlockSpec((1, nv, WINDOW), lambda i, t: (i, 0, nxt(t)))]
    return pl.pallas_call(
        functools.partial(_attn_a_kernel, ctx_tiles=ctx_tiles, n_tiles=n_tiles, t0=t0),
        grid=(b, nt),
        in_specs=[pl.BlockSpec(memory_space=pltpu.SMEM),
                  pl.BlockSpec((1, TM, qw), lambda i, t: (i, t + t0, 0))] + k_specs + v_specs,
        out_specs=pl.BlockSpec((1, TM, qw), lambda i, t: (i, t + t0, 0)),
        out_shape=jax.ShapeDtypeStruct((b, s_tot, qw), BF16),
        compiler_params=_params(2),
        name="attn_gqa",
    )(sink, q, k, k, k, k, vt, vt, vt, vt)


def _attn_b_kernel(q_ref, qn_ref, k_ref, vt_ref, o_ref, s_ref, run_ref, *, ctx_len, seq, n_tiles, t0):
    t = pl.program_id(2) + t0
    ctx_tiles = ctx_len // TM
    n_pairs = q_ref.shape[2] // (2 * LANES)
    slot = lambda h: h % s_ref.shape[0]
    assert (2 * n_pairs) % s_ref.shape[0] == 0

    def score_chunk(h, q_src, k0, tk, run):
        hs = slice(h * LANES, (h + 1) * LANES)
        s = _dot_nt(k_ref[0, k0:k0 + tk, hs], q_src[0, :, hs])
        s_ref[slot(h), k0:k0 + tk, :] = s
        for i in range(tk // MAX_FOLD):
            run = jnp.maximum(run, s[i * MAX_FOLD:(i + 1) * MAX_FOLD])
        return run

    def value_chunk(h, k0, tk, mx, acc):
        p = jnp.exp2(s_ref[slot(h), k0:k0 + tk, :] - mx).astype(BF16)
        c = _dot(vt_ref[0, h * LANES:(h + 1) * LANES, k0:k0 + tk], p)
        return c if acc is None else acc + c

    def attend(chunks, first_scored, score_next):
        init = jnp.full((MAX_FOLD, TM), NEG, F32)
        run = [run_ref[0], run_ref[1]] if first_scored else None
        for pr in range(1 if first_scored else 0, n_pairs + 1):
            mx = None if run is None else [jnp.max(r, axis=0, keepdims=True) for r in run]
            run, acc = [init, init], [None, None]
            for k0, tk in chunks:
                for e in range(2):
                    if pr < n_pairs:
                        run[e] = score_chunk(2 * pr + e, q_ref, k0, tk, run[e])
                    elif score_next:
                        run[e] = score_chunk(e, qn_ref, k0, tk, run[e])
                    if pr > 0:
                        acc[e] = value_chunk(2 * (pr - 1) + e, k0, tk, mx[e], acc[e])
            if pr > 0:
                out_t = jnp.concatenate([a[:B_V] * (1.0 / a[B_V:B_V + 1]) for a in acc], axis=0)
                o_ref[0, :, (pr - 1) * LANES:pr * LANES] = out_t.T.astype(BF16)
        if score_next:
            run_ref[0], run_ref[1] = run

    ctx_chunk = [(0, ctx_len)]
    all_chunks = ctx_chunk + [(ctx_len + i * TK_MLA, TK_MLA) for i in range(seq // TK_MLA)]
    pl.when(t < ctx_tiles)(lambda: attend(ctx_chunk, False, False))
    pl.when(t == ctx_tiles)(lambda: attend(all_chunks, False, ctx_tiles < n_tiles - 1))
    pl.when((t > ctx_tiles) & (t < n_tiles - 1))(lambda: attend(all_chunks, True, True))
    pl.when((t > ctx_tiles) & (t == n_tiles - 1))(lambda: attend(all_chunks, True, False))


def _attn_b(q, k, vt, seq, with_ctx):
    b, s_tot, qw = q.shape
    ctx_len = s_tot - seq
    n_tiles = s_tot // TM
    t0 = 0 if with_ctx else ctx_len // TM
    nt = n_tiles - t0
    heads = 2 * MLA_PAIRS
    wide = heads * LANES
    pair = 2 * LANES
    return pl.pallas_call(
        functools.partial(_attn_b_kernel, ctx_len=ctx_len, seq=seq, n_tiles=n_tiles, t0=t0),
        grid=(b, qw // wide, nt),
        in_specs=[pl.BlockSpec((1, TM, wide), lambda i, h, t: (i, t + t0, h)),
                  pl.BlockSpec((1, TM, pair), lambda i, h, t: (i, jnp.minimum(t + t0 + 1, n_tiles - 1), h * MLA_PAIRS)),
                  pl.BlockSpec((1, s_tot, wide), lambda i, h, t: (i, 0, h), pipeline_mode=pl.Buffered(1)),
                  pl.BlockSpec((1, wide, s_tot), lambda i, h, t: (i, h, 0), pipeline_mode=pl.Buffered(1))],
        out_specs=pl.BlockSpec((1, TM, heads * B_V), lambda i, h, t: (i, t + t0, h)),
        out_shape=jax.ShapeDtypeStruct((b, s_tot, B_HEADS * B_V), BF16),
        scratch_shapes=[pltpu.VMEM((min(heads, 4), s_tot, TM), F32), pltpu.VMEM((2, MAX_FOLD, TM), F32)],
        compiler_params=pltpu.CompilerParams(dimension_semantics=("parallel", "parallel", "arbitrary"),
                                             vmem_limit_bytes=VMEM_LIMIT),
        name="attn_mla",
    )(q, q, k, vt)


def _channel_kernel(xp_ref, xt_ref, xn_ref, op_ref, ot_ref, on_ref, mod_ref, g_ref, wo_ref, wa_ref, wv_ref,
                    cw_ref, cb_ref, wout_ref, gf_ref, y_ref, *, ctx_tiles, n_tiles, t0, final):
    t = pl.program_id(1) + t0
    ext = TM + 2 * HALO
    m = mod_ref[0, 0]
    x_ext = jnp.concatenate([xp_ref[0], xt_ref[0], xn_ref[0]], axis=0)
    o_ext = jnp.concatenate([op_ref[0], ot_ref[0], on_ref[0]], axis=0)
    x1 = x_ext + m[2:3] * _dot(o_ext, wo_ref[...])
    h2 = _norm_mod(x1, g_ref[...], m[3:4], m[4:5])
    prev_ok = (t != 0) & (t != ctx_tiles)
    next_ok = (t != ctx_tiles - 1) & (t != n_tiles - 1)
    row = lax.broadcasted_iota(jnp.int32, (ext, 1), 0)
    keep = ((row >= HALO) | prev_ok) & ((row < HALO + TM) | next_ok)
    h2 = jnp.where(keep, h2, 0.0).astype(BF16)
    h2_mid = h2[HALO:HALO + TM]
    cw = cw_ref[...]
    cb = cb_ref[...]
    d_ff = wa_ref.shape[1]
    y = None
    for c0 in range(0, d_ff, FF_CHUNK):
        cs = slice(c0, min(c0 + FF_CHUNK, d_ff))
        a = _dot(h2, wa_ref[:, cs])
        a_prev = pltpu.roll(a, 1, 0)[HALO:HALO + TM]
        a_next = pltpu.roll(a, ext - 1, 0)[HALO:HALO + TM]
        conv = a_prev * cw[0:1, cs] + a[HALO:HALO + TM] * cw[1:2, cs] + a_next * cw[2:3, cs] + cb[:, cs]
        gate = conv * (1.0 / (1.0 + jnp.exp(-conv)))
        hid = (gate * _dot(h2_mid, wv_ref[:, cs])).astype(BF16)
        part = _dot(hid, wout_ref[cs, :])
        y = part if y is None else y + part
    x2 = x1[HALO:HALO + TM] + m[5:6] * y
    y_ref[0] = _rms(x2, gf_ref[...]) if final else x2


def _channel(xs, o, modl, g, wo, wa, wv, cw, cb, wout, gf, seq, final):
    b, s_tot, d = xs.shape
    n_tiles = s_tot // TM
    ctx_tiles = (s_tot - seq) // TM
    t0 = ctx_tiles if final else 0
    nt = n_tiles - t0
    hb = TM // HALO
    n_halo = s_tot // HALO
    prev = lambda i, t: (i, jnp.maximum((t + t0) * hb - 1, t0 * hb), 0)
    cur = lambda i, t: (i, t + t0, 0)
    nxt = lambda i, t: (i, jnp.minimum((t + t0 + 1) * hb, n_halo - 1), 0)
    out_rows = seq if final else s_tot
    return pl.pallas_call(
        functools.partial(_channel_kernel, ctx_tiles=ctx_tiles, n_tiles=n_tiles, t0=t0, final=final),
        grid=(b, nt),
        in_specs=[pl.BlockSpec((1, HALO, d), prev), pl.BlockSpec((1, TM, d), cur), pl.BlockSpec((1, HALO, d), nxt),
                  pl.BlockSpec((1, HALO, d), prev), pl.BlockSpec((1, TM, d), cur), pl.BlockSpec((1, HALO, d), nxt),
                  pl.BlockSpec((1, 1, 6, d), lambda i, t: (i, jnp.minimum(t + t0, 1), 0, 0)),
                  _resident(g.shape), _resident(wo.shape), _resident(wa.shape), _resident(wv.shape),
                  _resident(cw.shape), _resident(cb.shape), _resident(wout.shape), _resident(gf.shape)],
        out_specs=pl.BlockSpec((1, TM, d), lambda i, t: (i, t, 0)),
        out_shape=jax.ShapeDtypeStruct((b, out_rows, d), F32),
        compiler_params=_params(2),
        name="channel",
    )(xs, xs, xs, o, o, o, modl, g, wo, wa, wv, cw, cb, wout, gf)


def _axial_angles(rows, rot_dim):
    row = jnp.repeat(jnp.arange(rows), GRID_W).astype(F32)
    col = jnp.tile(jnp.arange(GRID_W), rows).astype(F32)
    n_freq = rot_dim // 4
    inv = ROPE_BASE ** (-jnp.arange(n_freq, dtype=F32) / n_freq)
    return jnp.concatenate([row[:, None] * inv, col[:, None] * inv], axis=-1)


def _with_ctx_rows(cos, sin, ctx_cos, ctx_len, q_scale):
    cos = jnp.concatenate([jnp.broadcast_to(ctx_cos, (ctx_len, LANES)), cos], axis=0)
    sin = jnp.concatenate([jnp.zeros((ctx_len, LANES), F32), sin], axis=0)
    return jnp.stack([cos * q_scale, sin * q_scale, cos, sin])


def _tables_a(seq, ctx_len):
    ang = _axial_angles(seq // GRID_W, HEAD_DIM)
    cos, sin = jnp.cos(ang), jnp.sin(ang)
    reps = LANES // HEAD_DIM
    cos_l = jnp.tile(cos, (1, 2 * reps))
    sin_l = jnp.tile(jnp.concatenate([-sin, sin], axis=1), (1, reps))
    return _with_ctx_rows(cos_l, sin_l, jnp.ones((1, LANES), F32), ctx_len, LOG2E * HEAD_DIM ** -0.5)


def _tables_b(seq, ctx_len):
    ang = _axial_angles(seq // GRID_W, B_ROPE)
    cos, sin = jnp.cos(ang), jnp.sin(ang)
    copy = LANES - B_NOPE - B_ROPE
    cos_l = jnp.concatenate([jnp.ones((seq, B_NOPE), F32), cos, cos, jnp.zeros((seq, copy), F32)], axis=1)
    sin_l = jnp.concatenate([jnp.zeros((seq, B_NOPE), F32), -sin, sin, jnp.zeros((seq, copy), F32)], axis=1)
    ctx_cos = jnp.concatenate([jnp.ones((1, B_NOPE + B_ROPE), F32), jnp.zeros((1, copy), F32)], axis=1)
    return _with_ctx_rows(cos_l, sin_l, ctx_cos, ctx_len, LOG2E * (B_NOPE + B_ROPE) ** -0.5)


def _deinterleave(n):
    return jnp.concatenate([jnp.arange(0, n, 2), jnp.arange(1, n, 2)])


def _weights_a(wqkv):
    d = wqkv.shape[0]
    kw = A_KV_HEADS * HEAD_DIM
    qw = wqkv.shape[1] - 2 * kw
    perm = _deinterleave(HEAD_DIM)
    wq = wqkv[:, :qw].reshape(d, -1, HEAD_DIM)[:, :, perm].reshape(d, qw)
    wk = wqkv[:, qw:qw + kw].reshape(d, A_KV_HEADS, HEAD_DIM)[:, :, perm]
    wkk = jnp.concatenate([wk, wk], axis=2).reshape(d, 2 * kw)
    wv = wqkv[:, qw + kw:].reshape(d, A_KV_HEADS, HEAD_DIM)
    wv = jnp.concatenate([wv, jnp.zeros((d, A_KV_HEADS, LANES - HEAD_DIM), F32)], axis=2)
    wvt = wv.reshape(d, A_KV_HEADS * LANES).T
    return jnp.concatenate([wq, wkk], axis=1).astype(BF16), wvt.astype(BF16), qw, 2 * kw


def _weights_b(wdown, wuq, wuk, wuv, q_lora, kv_lora):
    d = wdown.shape[0]
    perm = _deinterleave(B_ROPE)
    assert LANES - B_NOPE - B_ROPE == B_ROPE
    w_rope = wdown[:, q_lora + kv_lora:][:, perm]
    w_rope = jnp.concatenate([jnp.zeros((d, B_NOPE), F32), w_rope, w_rope], axis=1)
    wd = jnp.concatenate([wdown[:, :q_lora + kv_lora], w_rope], axis=1)
    uq = wuq.reshape(q_lora, B_HEADS, B_NOPE + B_ROPE)
    uq_rope = uq[:, :, B_NOPE:][:, :, perm]
    uq = jnp.concatenate([uq[:, :, :B_NOPE], uq_rope, uq_rope], axis=2)
    uk = wuk.reshape(kv_lora, B_HEADS, B_NOPE)
    uk = jnp.concatenate([uk, jnp.zeros((kv_lora, B_HEADS, LANES - B_NOPE), F32)], axis=2)
    uv = wuv.reshape(kv_lora, B_HEADS, B_V)
    uv = jnp.concatenate([uv, jnp.zeros((kv_lora, B_HEADS, LANES - B_V), F32)], axis=2)
    flat = lambda w: w.reshape(w.shape[0], B_HEADS * LANES).astype(BF16)
    return wd.astype(BF16), flat(uq), flat(uk), flat(uv).T


def kernel(x, c, ctx, c_ctx, mod_w, mod_b, norm1_g, norm2_g, a_wqkv, a_wo, a_sink, b_wdown, b_qnorm_g, b_wuq,
           b_kvnorm_g, b_wuk, b_wuv, b_wo, f_win, f_conv_w, f_conv_b, f_wout, final_g):
    bsz, seq, d = x.shape
    ctx_len = ctx.shape[1]
    depth = mod_w.shape[0]
    d_ff = f_wout.shape[1]
    assert ctx_len == TM and seq % TM == 0 and seq % TK_MLA == 0 and seq % GRID_W == 0

    cond_rows = -(-(bsz + 1) // 8) * 8
    cond = jnp.concatenate([c, c_ctx[None], jnp.zeros((cond_rows - bsz - 1, d), F32)], axis=0)
    mod = _modulation(cond, mod_w, mod_b)

    tab_a = _tables_a(seq, ctx_len)
    tab_b = _tables_b(seq, ctx_len)
    xs = jnp.concatenate([ctx, x], axis=1)
    row = lambda v: v.reshape(1, -1)

    for i in range(depth):
        last = i == depth - 1
        lat = mod[i, :bsz].reshape(bsz, 6, d)
        cmod = jnp.broadcast_to(mod[i, bsz].reshape(1, 6, d), (bsz, 6, d))
        modl = jnp.stack([cmod, lat], axis=1)
        j = i // 2
        if i % 2 == 0:
            w, wvt, qw, kw = _weights_a(a_wqkv[j])
            q, k, vt = _proj_a(xs, modl, row(norm1_g[i]), w, wvt, tab_a, qw, kw)
            o = _attn_a(q, k, vt, a_sink[j], seq, not last)
            wo = a_wo[j]
        else:
            q_lora, kv_lora = b_qnorm_g.shape[1], b_kvnorm_g.shape[1]
            wd, wuq, wuk, wuvt = _weights_b(b_wdown[j], b_wuq[j], b_wuk[j], b_wuv[j], q_lora, kv_lora)
            q, k, vt = _proj_b(xs, modl, row(norm1_g[i]), wd, row(b_qnorm_g[j]), row(b_kvnorm_g[j]),
                               wuq, wuk, wuvt, tab_b)
            o = _attn_b(q, k, vt, seq, not last)
            wo = b_wo[j]
        xs = _channel(xs, o, modl, row(norm2_g[i]), wo.astype(BF16), f_win[i][:, :d_ff].astype(BF16),
                      f_win[i][:, d_ff:].astype(BF16), f_conv_w[i], row(f_conv_b[i]), f_wout[i].astype(BF16),
                      row(final_g), seq, last)
    return xs
```

```python
import functools

import jax
import jax.numpy as jnp
from jax import lax
from jax.experimental import pallas as pl
from jax.experimental.pallas import tpu as pltpu

GRID_W = 64
HEAD_DIM = 64
A_KV_HEADS = 4
WINDOW = 128
B_HEADS = 16
B_NOPE = 64
B_ROPE = 32
B_V = 64
ROPE_BASE = 10000.0
EPS = 1e-6
NEG = -1e30
LOG2E = 1.4426950408889634

LANES = 128
TM = 256
HALO = 16
TK_MLA = 1024
MLA_PAIRS = 4
GQA_LAG = 3
MAX_FOLD = 64
FF_CHUNK = 768
VMEM_LIMIT = 56 * 1024 * 1024

F32 = jnp.float32
BF16 = jnp.bfloat16
_NT = (((1,), (1,)), ((), ()))


def _dot(a, b):
    return jnp.dot(a, b, preferred_element_type=F32)


def _dot_nt(a, b):
    return lax.dot_general(a, b, _NT, preferred_element_type=F32)


def _rms(xf, g):
    ms = jnp.mean(xf * xf, axis=-1, keepdims=True)
    return xf * lax.rsqrt(ms + EPS) * g


def _norm_mod(xf, g, shift, scale):
    return _rms(xf, g) * (1.0 + scale) + shift


def _params(n_axes):
    return pltpu.CompilerParams(dimension_semantics=("parallel",) * n_axes,
                                vmem_limit_bytes=VMEM_LIMIT)


def _resident(shape):
    nd = len(shape)
    return pl.BlockSpec(shape, lambda *_: (0,) * nd, pipeline_mode=pl.Buffered(1))


def _mod_kernel(c_ref, w_ref, b_ref, o_ref):
    c = c_ref[...]
    silu = c * (1.0 / (1.0 + jnp.exp(-c)))
    o_ref[0] = _dot(silu.astype(BF16), w_ref[0].astype(BF16)) + b_ref[0]


def _modulation(cond, mod_w, mod_b):
    depth, d, d6 = mod_w.shape
    rows = cond.shape[0]
    return pl.pallas_call(
        _mod_kernel,
        grid=(depth, d6 // d),
        in_specs=[pl.BlockSpec((rows, d), lambda i, j: (0, 0)),
                  pl.BlockSpec((1, d, d), lambda i, j: (i, 0, j)),
                  pl.BlockSpec((1, 1, d), lambda i, j: (i, 0, j))],
        out_specs=pl.BlockSpec((1, rows, d), lambda i, j: (i, 0, j)),
        out_shape=jax.ShapeDtypeStruct((depth, rows, d6), F32),
        compiler_params=_params(2),
        name="modulation",
    )(cond, mod_w, mod_b.reshape(depth, 1, d6))


def _rope_block(xb, cos, sin_signed, first_half, half):
    swapped = jnp.where(first_half, pltpu.roll(xb, LANES - half, 1), pltpu.roll(xb, half, 1))
    return xb * cos + swapped * sin_signed


def _rope_block_dup(xb, cos, sin_signed, half):
    return xb * cos + pltpu.roll(xb, LANES - half, 1) * sin_signed


def _proj_a_kernel(x_ref, mod_ref, g_ref, w_ref, wvt_ref, tab_ref, q_ref, k_ref, vt_ref, *, qw, kw):
    m = mod_ref[0, 0]
    h = _norm_mod(x_ref[0], g_ref[...], m[0:1], m[1:2]).astype(BF16)
    qk = _dot(h, w_ref[...])
    lane = lax.broadcasted_iota(jnp.int32, (TM, LANES), 1)
    first = (lane % HEAD_DIM) < (HEAD_DIM // 2)
    cq, sq, ck, sk = tab_ref[0], tab_ref[1], tab_ref[2], tab_ref[3]
    for j in range(qw // LANES):
        blk = qk[:, j * LANES:(j + 1) * LANES]
        q_ref[0, :, j * LANES:(j + 1) * LANES] = _rope_block(blk, cq, sq, first, HEAD_DIM // 2).astype(BF16)
    for j in range(kw // LANES):
        blk = qk[:, qw + j * LANES:qw + (j + 1) * LANES]
        k_ref[0, :, j * LANES:(j + 1) * LANES] = _rope_block_dup(blk, ck, sk, HEAD_DIM // 2).astype(BF16)
    vrow = lax.broadcasted_iota(jnp.int32, (wvt_ref.shape[0], 1), 0) % LANES
    vt_ref[0] = (_dot_nt(wvt_ref[...], h) + jnp.where(vrow == HEAD_DIM, 1.0, 0.0)).astype(BF16)


def _proj_a_first_kernel(ctx_ref, lat_ref, mod_ref, g_ref, w_ref, wvt_ref, tab_ref, xs_ref, q_ref, k_ref, vt_ref,
                         *, ctx_tiles, **kw):
    xs_ref[0] = jnp.where(pl.program_id(1) < ctx_tiles, ctx_ref[0], lat_ref[0])
    _proj_a_kernel(xs_ref, mod_ref, g_ref, w_ref, wvt_ref, tab_ref, q_ref, k_ref, vt_ref, **kw)


def _proj_a(xs, modl, g, w, wvt, tab, qw, kw):
    first = isinstance(xs, tuple)
    if first:
        ctx, lat = xs
        b, ctx_len, d = ctx.shape
        ctx_tiles = ctx_len // TM
        s_tot = ctx_len + lat.shape[1]
        x_args = (ctx, lat)
        x_specs = [pl.BlockSpec((1, TM, d), lambda i, t: (i, jnp.minimum(t, ctx_tiles - 1), 0)),
                   pl.BlockSpec((1, TM, d), lambda i, t: (i, jnp.maximum(t - ctx_tiles, 0), 0))]
        body = functools.partial(_proj_a_first_kernel, ctx_tiles=ctx_tiles, qw=qw, kw=kw)
    else:
        b, s_tot, d = xs.shape
        x_args = (xs,)
        x_specs = [pl.BlockSpec((1, TM, d), lambda i, t: (i, t, 0))]
        body = functools.partial(_proj_a_kernel, qw=qw, kw=kw)
    nt = s_tot // TM
    nv = wvt.shape[0]
    out = lambda n: jax.ShapeDtypeStruct((b, s_tot, n), BF16)
    ospec = lambda n: pl.BlockSpec((1, TM, n), lambda i, t: (i, t, 0))
    out_specs = [ospec(qw), ospec(kw), pl.BlockSpec((1, nv, TM), lambda i, t: (i, 0, t))]
    out_shape = [out(qw), out(kw), jax.ShapeDtypeStruct((b, nv, s_tot), BF16)]
    if first:
        out_specs.insert(0, ospec(d))
        out_shape.insert(0, jax.ShapeDtypeStruct((b, s_tot, d), F32))
    return pl.pallas_call(
        body,
        grid=(b, nt),
        in_specs=x_specs + [pl.BlockSpec((1, 1, 6, d), lambda i, t: (i, jnp.minimum(t, 1), 0, 0)),
                            _resident(g.shape), _resident(w.shape), _resident(wvt.shape),
                            pl.BlockSpec((4, TM, LANES), lambda i, t: (0, t, 0))],
        out_specs=out_specs,
        out_shape=out_shape,
        compiler_params=_params(2),
        name="proj_gqa",
    )(*x_args, modl, g, w, wvt, tab)


def _proj_b_kernel(x_ref, mod_ref, g_ref, wd_ref, gq_ref, gkv_ref, wuq_ref, wuk_ref, wuvt_ref, tab_ref,
                   q_ref, k_ref, vt_ref, *, q_lora, kv_lora):
    m = mod_ref[0, 0]
    h = _norm_mod(x_ref[0], g_ref[...], m[0:1], m[1:2]).astype(BF16)
    d = _dot(h, wd_ref[...])
    cq = _rms(d[:, :q_lora], gq_ref[...]).astype(BF16)
    ckv = _rms(d[:, q_lora:q_lora + kv_lora], gkv_ref[...]).astype(BF16)
    tq_c, tq_s, tk_c, tk_s = tab_ref[0], tab_ref[1], tab_ref[2], tab_ref[3]
    k_rope = _rope_block_dup(d[:, q_lora + kv_lora:], tk_c, tk_s, B_ROPE // 2)
    q = _dot(cq, wuq_ref[...])
    k = _dot(ckv, wuk_ref[...])
    for j in range(B_HEADS):
        sl = slice(j * LANES, (j + 1) * LANES)
        q_ref[0, :, sl] = _rope_block_dup(q[:, sl], tq_c, tq_s, B_ROPE // 2).astype(BF16)
        k_ref[0, :, sl] = (k[:, sl] + k_rope).astype(BF16)
    vrow = lax.broadcasted_iota(jnp.int32, (wuvt_ref.shape[0], 1), 0) % LANES
    vt_ref[0] = (_dot_nt(wuvt_ref[...], ckv) + jnp.where(vrow == B_V, 1.0, 0.0)).astype(BF16)


def _proj_b(xs, modl, g, wd, gq, gkv, wuq, wuk, wuvt, tab):
    b, s_tot, d = xs.shape
    nt = s_tot // TM
    out = lambda n: jax.ShapeDtypeStruct((b, s_tot, n), BF16)
    ospec = lambda n: pl.BlockSpec((1, TM, n), lambda i, t: (i, t, 0))
    nq, nv = wuq.shape[1], wuvt.shape[0]
    return pl.pallas_call(
        functools.partial(_proj_b_kernel, q_lora=gq.shape[1], kv_lora=gkv.shape[1]),
        grid=(b, nt),
        in_specs=[pl.BlockSpec((1, TM, d), lambda i, t: (i, t, 0)),
                  pl.BlockSpec((1, 1, 6, d), lambda i, t: (i, jnp.minimum(t, 1), 0, 0)),
                  _resident(g.shape), _resident(wd.shape), _resident(gq.shape), _resident(gkv.shape),
                  _resident(wuq.shape), _resident(wuk.shape), _resident(wuvt.shape),
                  pl.BlockSpec((4, TM, LANES), lambda i, t: (0, t, 0))],
        out_specs=[ospec(nq), ospec(nq), pl.BlockSpec((1, nv, TM), lambda i, t: (i, 0, t))],
        out_shape=[out(nq), out(nq), jax.ShapeDtypeStruct((b, nv, s_tot), BF16)],
        compiler_params=_params(2),
        name="proj_mla",
    )(xs, modl, g, wd, gq, gkv, wuq, wuk, wuvt, tab)


def _attn_a_kernel(sink_ref, q_ref, kc_ref, kp_ref, kt_ref, kn_ref, vc_ref, vp_ref, vt_ref, vn_ref, o_ref,
                   *, ctx_tiles, n_tiles, t0):
    t = pl.program_id(1) + t0
    assert TM == 2 * WINDOW
    edge = WINDOW
    n_heads = q_ref.shape[2] // HEAD_DIM
    group = n_heads // A_KV_HEADS
    lane = lax.broadcasted_iota(jnp.int32, (1, LANES), 1)
    lo = lane < HEAD_DIM
    zero = jnp.zeros((), BF16)

    key = lax.broadcasted_iota(jnp.int32, (edge, edge), 0)
    qry = lax.broadcasted_iota(jnp.int32, (edge, edge), 1)
    tri = lambda valid: jnp.where(valid, 0.0, NEG).astype(F32)
    ALL, NONE = "all", "none"
    ctx_layout = [(ALL, ALL)] * (kc_ref.shape[1] // edge)
    win_layout = [(tri((qry <= key) & (t > ctx_tiles)), NONE),
                  (ALL, tri(qry <= key)), (tri(qry >= key), ALL),
                  (NONE, tri((qry >= key) & (t < n_tiles - 1)))]

    def attend(layout, k_of, v_of):
        def score(h):
            qp = q_ref[0, :, (h // 2) * LANES:(h // 2 + 1) * LANES]
            qe = jnp.where(lo, qp, zero) if h % 2 == 0 else jnp.where(lo, zero, qp)
            s = _dot_nt(k_of(h // group), qe)
            sink = sink_ref[h] * LOG2E
            halves = []
            for half in range(2):
                blocks, run = [], None
                for rb, kinds in enumerate(layout):
                    kind = kinds[half]
                    if isinstance(kind, str) and kind == NONE:
                        blocks.append(None)
                        continue
                    blk = s[rb * edge:(rb + 1) * edge, half * edge:(half + 1) * edge]
                    if not isinstance(kind, str):
                        blk = blk + kind
                    blocks.append(blk)
                    for i in range(edge // MAX_FOLD):
                        part = blk[i * MAX_FOLD:(i + 1) * MAX_FOLD]
                        run = part if run is None else jnp.maximum(run, part)
                halves.append((blocks, jnp.maximum(jnp.max(run, axis=0, keepdims=True), sink)))
            return halves, sink

        def probs(st):
            halves, sink = st
            rows = []
            for rb in range(len(layout)):
                parts = [jnp.zeros((edge, edge), BF16) if blocks[rb] is None
                         else jnp.exp2(blocks[rb] - mx).astype(BF16) for blocks, mx in halves]
                rows.append(jnp.concatenate(parts, axis=1))
            p_sink = jnp.concatenate([jnp.exp2(sink - mx) for _, mx in halves], axis=1)
            return jnp.concatenate(rows, axis=0), p_sink

        def values(h, pr):
            p, p_sink = pr
            acc = _dot(v_of(h // group), p)
            return acc[:HEAD_DIM] * (1.0 / (acc[HEAD_DIM:HEAD_DIM + 1] + p_sink))

        scored, weighted, outs = {}, {}, {}
        for step in range(n_heads + 2 * GQA_LAG):
            h = step - 2 * GQA_LAG
            if 0 <= h < n_heads:
                outs[h] = values(h, weighted.pop(h))
                if h % 2 == 1:
                    pair_t = jnp.concatenate([outs.pop(h - 1), outs.pop(h)], axis=0)
                    o_ref[0, :, (h // 2) * LANES:(h // 2 + 1) * LANES] = pair_t.T.astype(BF16)
            if 0 <= step - GQA_LAG < n_heads:
                weighted[step - GQA_LAG] = probs(scored.pop(step - GQA_LAG))
            if step < n_heads:
                scored[step] = score(step)

    def with_kv(body, k_refs, v_refs):
        ks, vs = [], []
        for j in range(A_KV_HEADS):
            sl = slice(j * LANES, (j + 1) * LANES)
            ks.append(jnp.concatenate([r[0, :, sl] for r in k_refs], axis=0))
            vs.append(jnp.concatenate([r[0, sl, :] for r in v_refs], axis=1))
        body(ks.__getitem__, vs.__getitem__)

    @pl.when(t < ctx_tiles)
    def _():
        with_kv(functools.partial(attend, ctx_layout), [kc_ref], [vc_ref])

    @pl.when(t >= ctx_tiles)
    def _():
        with_kv(functools.partial(attend, ctx_layout + win_layout),
                [kc_ref, kp_ref, kt_ref, kn_ref], [vc_ref, vp_ref, vt_ref, vn_ref])


def _attn_a(q, k, vt, sink, seq, with_ctx):
    b, s_tot, qw = q.shape
    kw, nv = k.shape[2], vt.shape[1]
    ctx_tiles = (s_tot - seq) // TM
    t0 = 0 if with_ctx else ctx_tiles
    n_tiles = s_tot // TM
    nt = n_tiles - t0
    hb = TM // WINDOW
    n_half = s_tot // WINDOW
    prev = lambda t: jnp.maximum((t + t0) * hb - 1, 0)
    nxt = lambda t: jnp.minimum((t + t0 + 1) * hb, n_half - 1)
    k_specs = [pl.BlockSpec((1, TM, kw), lambda i, t: (i, 0, 0)),
               pl.BlockSpec((1, WINDOW, kw), lambda i, t: (i, prev(t), 0)),
               pl.BlockSpec((1, TM, kw), lambda i, t: (i, t + t0, 0)),
               pl.BlockSpec((1, WINDOW, kw), lambda i, t: (i, nxt(t), 0))]
    v_specs = [pl.BlockSpec((1, nv, TM), lambda i, t: (i, 0, 0)),
               pl.BlockSpec((1, nv, WINDOW), lambda i, t: (i, 0, prev(t))),
               pl.BlockSpec((1, nv, TM), lambda i, t: (i, 0, t + t0)),
               pl.BlockSpec((1, nv, WINDOW), lambda i, t: (i, 0, nxt(t)))]
    return pl.pallas_call(
        functools.partial(_attn_a_kernel, ctx_tiles=ctx_tiles, n_tiles=n_tiles, t0=t0),
        grid=(b, nt),
        in_specs=[pl.BlockSpec(memory_space=pltpu.SMEM),
                  pl.BlockSpec((1, TM, qw), lambda i, t: (i, t + t0, 0))] + k_specs + v_specs,
        out_specs=pl.BlockSpec((1, TM, qw), lambda i, t: (i, t + t0, 0)),
        out_shape=jax.ShapeDtypeStruct((b, s_tot, qw), BF16),
        compiler_params=_params(2),
        name="attn_gqa",
    )(sink, q, k, k, k, k, vt, vt, vt, vt)


def _attn_b_kernel(q_ref, qn_ref, k_ref, vt_ref, o_ref, s_ref, run_ref, *, ctx_len, seq, n_tiles, t0):
    t = pl.program_id(2) + t0
    ctx_tiles = ctx_len // TM
    n_pairs = q_ref.shape[2] // (2 * LANES)
    slot = lambda h: h % s_ref.shape[0]
    assert (2 * n_pairs) % s_ref.shape[0] == 0

    def score_chunk(h, q_src, k0, tk, run):
        hs = slice(h * LANES, (h + 1) * LANES)
        s = _dot_nt(k_ref[0, k0:k0 + tk, hs], q_src[0, :, hs])
        s_ref[slot(h), k0:k0 + tk, :] = s
        for i in range(tk // MAX_FOLD):
            run = jnp.maximum(run, s[i * MAX_FOLD:(i + 1) * MAX_FOLD])
        return run

    def value_chunk(h, k0, tk, mx, acc):
        p = jnp.exp2(s_ref[slot(h), k0:k0 + tk, :] - mx).astype(BF16)
        c = _dot(vt_ref[0, h * LANES:(h + 1) * LANES, k0:k0 + tk], p)
        return c if acc is None else acc + c

    def attend(chunks, first_scored, score_next):
        init = jnp.full((MAX_FOLD, TM), NEG, F32)
        run = [run_ref[0], run_ref[1]] if first_scored else None
        for pr in range(1 if first_scored else 0, n_pairs + 1):
            mx = None if run is None else [jnp.max(r, axis=0, keepdims=True) for r in run]
            run, acc = [init, init], [None, None]
            for k0, tk in chunks:
                for e in range(2):
                    if pr < n_pairs:
                        run[e] = score_chunk(2 * pr + e, q_ref, k0, tk, run[e])
                    elif score_next:
                        run[e] = score_chunk(e, qn_ref, k0, tk, run[e])
                    if pr > 0:
                        acc[e] = value_chunk(2 * (pr - 1) + e, k0, tk, mx[e], acc[e])
            if pr > 0:
                out_t = jnp.concatenate([a[:B_V] * (1.0 / a[B_V:B_V + 1]) for a in acc], axis=0)
                o_ref[0, :, (pr - 1) * LANES:pr * LANES] = out_t.T.astype(BF16)
        if score_next:
            run_ref[0], run_ref[1] = run

    ctx_chunk = [(0, ctx_len)]
    all_chunks = ctx_chunk + [(ctx_len + i * TK_MLA, TK_MLA) for i in range(seq // TK_MLA)]
    pl.when(t < ctx_tiles)(lambda: attend(ctx_chunk, False, False))
    pl.when(t == ctx_tiles)(lambda: attend(all_chunks, False, ctx_tiles < n_tiles - 1))
    pl.when((t > ctx_tiles) & (t < n_tiles - 1))(lambda: attend(all_chunks, True, True))
    pl.when((t > ctx_tiles) & (t == n_tiles - 1))(lambda: attend(all_chunks, True, False))


def _attn_b(q, k, vt, seq, with_ctx):
    b, s_tot, qw = q.shape
    ctx_len = s_tot - seq
    n_tiles = s_tot // TM
    t0 = 0 if with_ctx else ctx_len // TM
    nt = n_tiles - t0
    heads = 2 * MLA_PAIRS
    wide = heads * LANES
    pair = 2 * LANES
    return pl.pallas_call(
        functools.partial(_attn_b_kernel, ctx_len=ctx_len, seq=seq, n_tiles=n_tiles, t0=t0),
        grid=(b, qw // wide, nt),
        in_specs=[pl.BlockSpec((1, TM, wide), lambda i, h, t: (i, t + t0, h)),
                  pl.BlockSpec((1, TM, pair), lambda i, h, t: (i, jnp.minimum(t + t0 + 1, n_tiles - 1), h * MLA_PAIRS)),
                  pl.BlockSpec((1, s_tot, wide), lambda i, h, t: (i, 0, h), pipeline_mode=pl.Buffered(1)),
                  pl.BlockSpec((1, wide, s_tot), lambda i, h, t: (i, h, 0), pipeline_mode=pl.Buffered(1))],
        out_specs=pl.BlockSpec((1, TM, heads * B_V), lambda i, h, t: (i, t + t0, h)),
        out_shape=jax.ShapeDtypeStruct((b, s_tot, B_HEADS * B_V), BF16),
        scratch_shapes=[pltpu.VMEM((min(heads, 4), s_tot, TM), F32), pltpu.VMEM((2, MAX_FOLD, TM), F32)],
        compiler_params=pltpu.CompilerParams(dimension_semantics=("parallel", "parallel", "arbitrary"),
                                             vmem_limit_bytes=VMEM_LIMIT),
        name="attn_mla",
    )(q, q, k, vt)


def _channel_kernel(xp_ref, xt_ref, xn_ref, op_ref, ot_ref, on_ref, mod_ref, g_ref, wo_ref, wa_ref, wv_ref,
                    cw_ref, cb_ref, wout_ref, gf_ref, y_ref, *, ctx_tiles, n_tiles, t0, final):
    t = pl.program_id(1) + t0
    ext = TM + 2 * HALO
    m = mod_ref[0, 0]
    x_ext = jnp.concatenate([xp_ref[0], xt_ref[0], xn_ref[0]], axis=0)
    o_ext = jnp.concatenate([op_ref[0], ot_ref[0], on_ref[0]], axis=0)
    x1 = x_ext + m[2:3] * _dot(o_ext, wo_ref[...])
    h2 = _norm_mod(x1, g_ref[...], m[3:4], m[4:5])
    prev_ok = (t != 0) & (t != ctx_tiles)
    next_ok = (t != ctx_tiles - 1) & (t != n_tiles - 1)
    row = lax.broadcasted_iota(jnp.int32, (ext, 1), 0)
    keep = ((row >= HALO) | prev_ok) & ((row < HALO + TM) | next_ok)
    h2 = jnp.where(keep, h2, 0.0).astype(BF16)
    h2_mid = h2[HALO:HALO + TM]
    cw = cw_ref[...]
    cb = cb_ref[...]
    d_ff = wa_ref.shape[1]
    y = None
    for c0 in range(0, d_ff, FF_CHUNK):
        cs = slice(c0, min(c0 + FF_CHUNK, d_ff))
        a = _dot(h2, wa_ref[:, cs])
        a_prev = pltpu.roll(a, 1, 0)[HALO:HALO + TM]
        a_next = pltpu.roll(a, ext - 1, 0)[HALO:HALO + TM]
        conv = a_prev * cw[0:1, cs] + a[HALO:HALO + TM] * cw[1:2, cs] + a_next * cw[2:3, cs] + cb[:, cs]
        gate = conv * (1.0 / (1.0 + jnp.exp(-conv)))
        hid = (gate * _dot(h2_mid, wv_ref[:, cs])).astype(BF16)
        part = _dot(hid, wout_ref[cs, :])
        y = part if y is None else y + part
    x2 = x1[HALO:HALO + TM] + m[5:6] * y
    y_ref[0] = _rms(x2, gf_ref[...]) if final else x2


def _channel(xs, o, modl, g, wo, wa, wv, cw, cb, wout, gf, seq, final):
    b, s_tot, d = xs.shape
    n_tiles = s_tot // TM
    ctx_tiles = (s_tot - seq) // TM
    t0 = ctx_tiles if final else 0
    nt = n_tiles - t0
    hb = TM // HALO
    n_halo = s_tot // HALO
    prev = lambda i, t: (i, jnp.maximum((t + t0) * hb - 1, t0 * hb), 0)
    cur = lambda i, t: (i, t + t0, 0)
    nxt = lambda i, t: (i, jnp.minimum((t + t0 + 1) * hb, n_halo - 1), 0)
    out_rows = seq if final else s_tot
    return pl.pallas_call(
        functools.partial(_channel_kernel, ctx_tiles=ctx_tiles, n_tiles=n_tiles, t0=t0, final=final),
        grid=(b, nt),
        in_specs=[pl.BlockSpec((1, HALO, d), prev), pl.BlockSpec((1, TM, d), cur), pl.BlockSpec((1, HALO, d), nxt),
                  pl.BlockSpec((1, HALO, d), prev), pl.BlockSpec((1, TM, d), cur), pl.BlockSpec((1, HALO, d), nxt),
                  pl.BlockSpec((1, 1, 6, d), lambda i, t: (i, jnp.minimum(t + t0, 1), 0, 0)),
                  _resident(g.shape), _resident(wo.shape), _resident(wa.shape), _resident(wv.shape),
                  _resident(cw.shape), _resident(cb.shape), _resident(wout.shape), _resident(gf.shape)],
        out_specs=pl.BlockSpec((1, TM, d), lambda i, t: (i, t, 0)),
        out_shape=jax.ShapeDtypeStruct((b, out_rows, d), F32),
        compiler_params=_params(2),
        name="channel",
    )(xs, xs, xs, o, o, o, modl, g, wo, wa, wv, cw, cb, wout, gf)


def _axial_angles(rows, rot_dim):
    row = jnp.repeat(jnp.arange(rows), GRID_W).astype(F32)
    col = jnp.tile(jnp.arange(GRID_W), rows).astype(F32)
    n_freq = rot_dim // 4
    inv = ROPE_BASE ** (-jnp.arange(n_freq, dtype=F32) / n_freq)
    return jnp.concatenate([row[:, None] * inv, col[:, None] * inv], axis=-1)


def _with_ctx_rows(cos, sin, ctx_cos, ctx_len, q_scale):
    cos = jnp.concatenate([jnp.broadcast_to(ctx_cos, (ctx_len, LANES)), cos], axis=0)
    sin = jnp.concatenate([jnp.zeros((ctx_len, LANES), F32), sin], axis=0)
    return jnp.stack([cos * q_scale, sin * q_scale, cos, sin])


def _tables_a(seq, ctx_len):
    ang = _axial_angles(seq // GRID_W, HEAD_DIM)
    cos, sin = jnp.cos(ang), jnp.sin(ang)
    reps = LANES // HEAD_DIM
    cos_l = jnp.tile(cos, (1, 2 * reps))
    sin_l = jnp.tile(jnp.concatenate([-sin, sin], axis=1), (1, reps))
    return _with_ctx_rows(cos_l, sin_l, jnp.ones((1, LANES), F32), ctx_len, LOG2E * HEAD_DIM ** -0.5)


def _tables_b(seq, ctx_len):
    ang = _axial_angles(seq // GRID_W, B_ROPE)
    cos, sin = jnp.cos(ang), jnp.sin(ang)
    copy = LANES - B_NOPE - B_ROPE
    cos_l = jnp.concatenate([jnp.ones((seq, B_NOPE), F32), cos, cos, jnp.zeros((seq, copy), F32)], axis=1)
    sin_l = jnp.concatenate([jnp.zeros((seq, B_NOPE), F32), -sin, sin, jnp.zeros((seq, copy), F32)], axis=1)
    ctx_cos = jnp.concatenate([jnp.ones((1, B_NOPE + B_ROPE), F32), jnp.zeros((1, copy), F32)], axis=1)
    return _with_ctx_rows(cos_l, sin_l, ctx_cos, ctx_len, LOG2E * (B_NOPE + B_ROPE) ** -0.5)


def _deinterleave(n):
    return jnp.concatenate([jnp.arange(0, n, 2), jnp.arange(1, n, 2)])


def _weights_a(wqkv):
    d = wqkv.shape[0]
    kw = A_KV_HEADS * HEAD_DIM
    qw = wqkv.shape[1] - 2 * kw
    perm = _deinterleave(HEAD_DIM)
    wq = wqkv[:, :qw].reshape(d, -1, HEAD_DIM)[:, :, perm].reshape(d, qw)
    wk = wqkv[:, qw:qw + kw].reshape(d, A_KV_HEADS, HEAD_DIM)[:, :, perm]
    wkk = jnp.concatenate([wk, wk], axis=2).reshape(d, 2 * kw)
    wv = wqkv[:, qw + kw:].reshape(d, A_KV_HEADS, HEAD_DIM)
    wv = jnp.concatenate([wv, jnp.zeros((d, A_KV_HEADS, LANES - HEAD_DIM), F32)], axis=2)
    wvt = wv.reshape(d, A_KV_HEADS * LANES).T
    return jnp.concatenate([wq, wkk], axis=1).astype(BF16), wvt.astype(BF16), qw, 2 * kw


def _weights_b(wdown, wuq, wuk, wuv, q_lora, kv_lora):
    d = wdown.shape[0]
    perm = _deinterleave(B_ROPE)
    assert LANES - B_NOPE - B_ROPE == B_ROPE
    w_rope = wdown[:, q_lora + kv_lora:][:, perm]
    w_rope = jnp.concatenate([jnp.zeros((d, B_NOPE), F32), w_rope, w_rope], axis=1)
    wd = jnp.concatenate([wdown[:, :q_lora + kv_lora], w_rope], axis=1)
    uq = wuq.reshape(q_lora, B_HEADS, B_NOPE + B_ROPE)
    uq_rope = uq[:, :, B_NOPE:][:, :, perm]
    uq = jnp.concatenate([uq[:, :, :B_NOPE], uq_rope, uq_rope], axis=2)
    uk = wuk.reshape(kv_lora, B_HEADS, B_NOPE)
    uk = jnp.concatenate([uk, jnp.zeros((kv_lora, B_HEADS, LANES - B_NOPE), F32)], axis=2)
    uv = wuv.reshape(kv_lora, B_HEADS, B_V)
    uv = jnp.concatenate([uv, jnp.zeros((kv_lora, B_HEADS, LANES - B_V), F32)], axis=2)
    flat = lambda w: w.reshape(w.shape[0], B_HEADS * LANES).astype(BF16)
    return wd.astype(BF16), flat(uq), flat(uk), flat(uv).T


def kernel(x, c, ctx, c_ctx, mod_w, mod_b, norm1_g, norm2_g, a_wqkv, a_wo, a_sink, b_wdown, b_qnorm_g, b_wuq,
           b_kvnorm_g, b_wuk, b_wuv, b_wo, f_win, f_conv_w, f_conv_b, f_wout, final_g):
    bsz, seq, d = x.shape
    ctx_len = ctx.shape[1]
    depth = mod_w.shape[0]
    d_ff = f_wout.shape[1]
    assert ctx_len == TM and seq % TM == 0 and seq % TK_MLA == 0 and seq % GRID_W == 0

    cond_rows = -(-(bsz + 1) // 8) * 8
    cond = jnp.concatenate([c, c_ctx[None], jnp.zeros((cond_rows - bsz - 1, d), F32)], axis=0)
    mod = _modulation(cond, mod_w, mod_b)

    tab_a = _tables_a(seq, ctx_len)
    tab_b = _tables_b(seq, ctx_len)
    xs = None
    row = lambda v: v.reshape(1, -1)

    for i in range(depth):
        last = i == depth - 1
        lat = mod[i, :bsz].reshape(bsz, 6, d)
        cmod = jnp.broadcast_to(mod[i, bsz].reshape(1, 6, d), (bsz, 6, d))
        modl = jnp.stack([cmod, lat], axis=1)
        j = i // 2
        if i % 2 == 0:
            w, wvt, qw, kw = _weights_a(a_wqkv[j])
            if i == 0:
                xs, q, k, vt = _proj_a((ctx, x), modl, row(norm1_g[i]), w, wvt, tab_a, qw, kw)
            else:
                q, k, vt = _proj_a(xs, modl, row(norm1_g[i]), w, wvt, tab_a, qw, kw)
            o = _attn_a(q, k, vt, a_sink[j], seq, not last)
            wo = a_wo[j]
        else:
            q_lora, kv_lora = b_qnorm_g.shape[1], b_kvnorm_g.shape[1]
            wd, wuq, wuk, wuvt = _weights_b(b_wdown[j], b_wuq[j], b_wuk[j], b_wuv[j], q_lora, kv_lora)
            q, k, vt = _proj_b(xs, modl, row(norm1_g[i]), wd, row(b_qnorm_g[j]), row(b_kvnorm_g[j]),
                               wuq, wuk, wuvt, tab_b)
            o = _attn_b(q, k, vt, seq, not last)
            wo = b_wo[j]
        xs = _channel(xs, o, modl, row(norm2_g[i]), wo.astype(BF16), f_win[i][:, :d_ff].astype(BF16),
                      f_win[i][:, d_ff:].astype(BF16), f_conv_w[i], row(f_conv_b[i]), f_wout[i].astype(BF16),
                      row(final_g), seq, last)
    return xs
```

```python
import functools

import jax
import jax.numpy as jnp
from jax import lax
from jax.experimental import pallas as pl
from jax.experimental.pallas import tpu as pltpu

GRID_W = 64
HEAD_DIM = 64
A_KV_HEADS = 4
WINDOW = 128
B_HEADS = 16
B_NOPE = 64
B_ROPE = 32
B_V = 64
ROPE_BASE = 10000.0
EPS = 1e-6
NEG = -1e30
LOG2E = 1.4426950408889634

LANES = 128
V7X_VMEM_BYTES = 64 * 1024 * 1024
VMEM_LIMIT = V7X_VMEM_BYTES * 7 // 8
TM = 256
HALO = 16
TK_MLA = 1024
MLA_PAIRS = 4
GQA_LAG = 3
MAX_FOLD = 64
FF_CHUNK = 768

F32 = jnp.float32
BF16 = jnp.bfloat16
_NT = (((1,), (1,)), ((), ()))


def _dot(a, b):
    return jnp.dot(a, b, preferred_element_type=F32)


def _dot_nt(a, b):
    return lax.dot_general(a, b, _NT, preferred_element_type=F32)


def _rms(xf, g):
    ms = jnp.mean(xf * xf, axis=-1, keepdims=True)
    return xf * lax.rsqrt(ms + EPS) * g


def _norm_mod(xf, g, shift, scale):
    return _rms(xf, g) * (1.0 + scale) + shift


def _params(n_axes):
    return pltpu.CompilerParams(dimension_semantics=("parallel",) * n_axes,
                                vmem_limit_bytes=VMEM_LIMIT)


def _resident(shape):
    nd = len(shape)
    return pl.BlockSpec(shape, lambda *_: (0,) * nd, pipeline_mode=pl.Buffered(1))


def _mod_kernel(c_ref, w_ref, b_ref, o_ref):
    c = c_ref[...]
    silu = c * (1.0 / (1.0 + jnp.exp(-c)))
    o_ref[0] = _dot(silu.astype(BF16), w_ref[0].astype(BF16)) + b_ref[0]


def _modulation(cond, mod_w, mod_b):
    depth, d, d6 = mod_w.shape
    rows = cond.shape[0]
    return pl.pallas_call(
        _mod_kernel,
        grid=(depth, d6 // d),
        in_specs=[pl.BlockSpec((rows, d), lambda i, j: (0, 0)),
                  pl.BlockSpec((1, d, d), lambda i, j: (i, 0, j)),
                  pl.BlockSpec((1, 1, d), lambda i, j: (i, 0, j))],
        out_specs=pl.BlockSpec((1, rows, d), lambda i, j: (i, 0, j)),
        out_shape=jax.ShapeDtypeStruct((depth, rows, d6), F32),
        compiler_params=_params(2),
        name="modulation",
    )(cond, mod_w, mod_b.reshape(depth, 1, d6))


def _rope_block(xb, cos, sin_signed, first_half, half):
    swapped = jnp.where(first_half, pltpu.roll(xb, LANES - half, 1), pltpu.roll(xb, half, 1))
    return xb * cos + swapped * sin_signed


def _rope_block_dup(xb, cos, sin_signed, half):
    return xb * cos + pltpu.roll(xb, LANES - half, 1) * sin_signed


def _proj_a_kernel(x_ref, mod_ref, g_ref, w_ref, wvt_ref, tab_ref, q_ref, k_ref, vt_ref, *, qw, kw):
    m = mod_ref[0, 0]
    h = _norm_mod(x_ref[0], g_ref[...], m[0:1], m[1:2]).astype(BF16)
    qk = _dot(h, w_ref[...])
    lane = lax.broadcasted_iota(jnp.int32, (TM, LANES), 1)
    first = (lane % HEAD_DIM) < (HEAD_DIM // 2)
    cq, sq, ck, sk = tab_ref[0], tab_ref[1], tab_ref[2], tab_ref[3]
    for j in range(qw // LANES):
        blk = qk[:, j * LANES:(j + 1) * LANES]
        q_ref[0, :, j * LANES:(j + 1) * LANES] = _rope_block(blk, cq, sq, first, HEAD_DIM // 2).astype(BF16)
    for j in range(kw // LANES):
        blk = qk[:, qw + j * LANES:qw + (j + 1) * LANES]
        k_ref[0, :, j * LANES:(j + 1) * LANES] = _rope_block_dup(blk, ck, sk, HEAD_DIM // 2).astype(BF16)
    vrow = lax.broadcasted_iota(jnp.int32, (wvt_ref.shape[0], 1), 0) % LANES
    vt_ref[0] = (_dot_nt(wvt_ref[...], h) + jnp.where(vrow == HEAD_DIM, 1.0, 0.0)).astype(BF16)


def _proj_a_first_kernel(ctx_ref, lat_ref, mod_ref, g_ref, w_ref, wvt_ref, tab_ref, xs_ref, q_ref, k_ref, vt_ref,
                         *, ctx_tiles, **kw):
    xs_ref[0] = jnp.where(pl.program_id(1) < ctx_tiles, ctx_ref[0], lat_ref[0])
    _proj_a_kernel(xs_ref, mod_ref, g_ref, w_ref, wvt_ref, tab_ref, q_ref, k_ref, vt_ref, **kw)


def _proj_a(xs, modl, g, w, wvt, tab, qw, kw):
    first = isinstance(xs, tuple)
    if first:
        ctx, lat = xs
        b, ctx_len, d = ctx.shape
        ctx_tiles = ctx_len // TM
        s_tot = ctx_len + lat.shape[1]
        x_args = (ctx, lat)
        x_specs = [pl.BlockSpec((1, TM, d), lambda i, t: (i, jnp.minimum(t, ctx_tiles - 1), 0)),
                   pl.BlockSpec((1, TM, d), lambda i, t: (i, jnp.maximum(t - ctx_tiles, 0), 0))]
        body = functools.partial(_proj_a_first_kernel, ctx_tiles=ctx_tiles, qw=qw, kw=kw)
    else:
        b, s_tot, d = xs.shape
        x_args = (xs,)
        x_specs = [pl.BlockSpec((1, TM, d), lambda i, t: (i, t, 0))]
        body = functools.partial(_proj_a_kernel, qw=qw, kw=kw)
    nt = s_tot // TM
    nv = wvt.shape[0]
    out = lambda n: jax.ShapeDtypeStruct((b, s_tot, n), BF16)
    ospec = lambda n: pl.BlockSpec((1, TM, n), lambda i, t: (i, t, 0))
    out_specs = [ospec(qw), ospec(kw), pl.BlockSpec((1, nv, TM), lambda i, t: (i, 0, t))]
    out_shape = [out(qw), out(kw), jax.ShapeDtypeStruct((b, nv, s_tot), BF16)]
    if first:
        out_specs.insert(0, ospec(d))
        out_shape.insert(0, jax.ShapeDtypeStruct((b, s_tot, d), F32))
    return pl.pallas_call(
        body,
        grid=(b, nt),
        in_specs=x_specs + [pl.BlockSpec((1, 1, 6, d), lambda i, t: (i, jnp.minimum(t, 1), 0, 0)),
                            _resident(g.shape), _resident(w.shape), _resident(wvt.shape),
                            pl.BlockSpec((4, TM, LANES), lambda i, t: (0, t, 0))],
        out_specs=out_specs,
        out_shape=out_shape,
        compiler_params=_params(2),
        name="proj_gqa",
    )(*x_args, modl, g, w, wvt, tab)


def _proj_b_kernel(x_ref, mod_ref, g_ref, wd_ref, gq_ref, gkv_ref, wuq_ref, wuk_ref, wuvt_ref, tab_ref,
                   q_ref, k_ref, vt_ref, *, q_lora, kv_lora):
    m = mod_ref[0, 0]
    h = _norm_mod(x_ref[0], g_ref[...], m[0:1], m[1:2]).astype(BF16)
    d = _dot(h, wd_ref[...])
    cq = _rms(d[:, :q_lora], gq_ref[...]).astype(BF16)
    ckv = _rms(d[:, q_lora:q_lora + kv_lora], gkv_ref[...]).astype(BF16)
    tq_c, tq_s, tk_c, tk_s = tab_ref[0], tab_ref[1], tab_ref[2], tab_ref[3]
    k_rope = _rope_block_dup(d[:, q_lora + kv_lora:], tk_c, tk_s, B_ROPE // 2)
    q = _dot(cq, wuq_ref[...])
    k = _dot(ckv, wuk_ref[...])
    for j in range(B_HEADS):
        sl = slice(j * LANES, (j + 1) * LANES)
        q_ref[0, :, sl] = _rope_block_dup(q[:, sl], tq_c, tq_s, B_ROPE // 2).astype(BF16)
        k_ref[0, :, sl] = (k[:, sl] + k_rope).astype(BF16)
    vrow = lax.broadcasted_iota(jnp.int32, (wuvt_ref.shape[0], 1), 0) % LANES
    vt_ref[0] = (_dot_nt(wuvt_ref[...], ckv) + jnp.where(vrow == B_V, 1.0, 0.0)).astype(BF16)


def _proj_b(xs, modl, g, wd, gq, gkv, wuq, wuk, wuvt, tab):
    b, s_tot, d = xs.shape
    nt = s_tot // TM
    out = lambda n: jax.ShapeDtypeStruct((b, s_tot, n), BF16)
    ospec = lambda n: pl.BlockSpec((1, TM, n), lambda i, t: (i, t, 0))
    nq, nv = wuq.shape[1], wuvt.shape[0]
    return pl.pallas_call(
        functools.partial(_proj_b_kernel, q_lora=gq.shape[1], kv_lora=gkv.shape[1]),
        grid=(b, nt),
        in_specs=[pl.BlockSpec((1, TM, d), lambda i, t: (i, t, 0)),
                  pl.BlockSpec((1, 1, 6, d), lambda i, t: (i, jnp.minimum(t, 1), 0, 0)),
                  _resident(g.shape), _resident(wd.shape), _resident(gq.shape), _resident(gkv.shape),
                  _resident(wuq.shape), _resident(wuk.shape), _resident(wuvt.shape),
                  pl.BlockSpec((4, TM, LANES), lambda i, t: (0, t, 0))],
        out_specs=[ospec(nq), ospec(nq), pl.BlockSpec((1, nv, TM), lambda i, t: (i, 0, t))],
        out_shape=[out(nq), out(nq), jax.ShapeDtypeStruct((b, nv, s_tot), BF16)],
        compiler_params=_params(2),
        name="proj_mla",
    )(xs, modl, g, wd, gq, gkv, wuq, wuk, wuvt, tab)


def _attn_a_kernel(sink_ref, q_ref, kc_ref, kp_ref, kt_ref, kn_ref, vc_ref, vp_ref, vt_ref, vn_ref, o_ref,
                   *, ctx_tiles, n_tiles, ctx_out):
    t = pl.program_id(1)
    assert TM == 2 * WINDOW
    edge = WINDOW
    n_heads = q_ref.shape[2] // HEAD_DIM
    group = n_heads // A_KV_HEADS
    lane = lax.broadcasted_iota(jnp.int32, (1, LANES), 1)
    lo = lane < HEAD_DIM
    zero = jnp.zeros((), BF16)

    key = lax.broadcasted_iota(jnp.int32, (edge, edge), 0)
    qry = lax.broadcasted_iota(jnp.int32, (edge, edge), 1)
    tri = lambda valid: jnp.where(valid, 0.0, NEG).astype(F32)
    ALL, NONE = "all", "none"
    ctx_layout = [(ALL, ALL)] * (kc_ref.shape[1] // edge)
    win_layout = [(tri((qry <= key) & (t > ctx_tiles)), NONE),
                  (ALL, tri(qry <= key)), (tri(qry >= key), ALL),
                  (NONE, tri((qry >= key) & (t < n_tiles - 1)))]

    def attend(layout, k_of, v_of):
        def score(h):
            qp = q_ref[0, :, (h // 2) * LANES:(h // 2 + 1) * LANES]
            qe = jnp.where(lo, qp, zero) if h % 2 == 0 else jnp.where(lo, zero, qp)
            s = _dot_nt(k_of(h // group), qe)
            sink = sink_ref[h] * LOG2E
            halves = []
            for half in range(2):
                blocks, run = [], None
                for rb, kinds in enumerate(layout):
                    kind = kinds[half]
                    if isinstance(kind, str) and kind == NONE:
                        blocks.append(None)
                        continue
                    blk = s[rb * edge:(rb + 1) * edge, half * edge:(half + 1) * edge]
                    if not isinstance(kind, str):
                        blk = blk + kind
                    blocks.append(blk)
                    for i in range(edge // MAX_FOLD):
                        part = blk[i * MAX_FOLD:(i + 1) * MAX_FOLD]
                        run = part if run is None else jnp.maximum(run, part)
                halves.append((blocks, jnp.maximum(jnp.max(run, axis=0, keepdims=True), sink)))
            return halves, sink

        def probs(st):
            halves, sink = st
            rows = []
            for rb in range(len(layout)):
                parts = [jnp.zeros((edge, edge), BF16) if blocks[rb] is None
                         else jnp.exp2(blocks[rb] - mx).astype(BF16) for blocks, mx in halves]
                rows.append(jnp.concatenate(parts, axis=1))
            p_sink = jnp.concatenate([jnp.exp2(sink - mx) for _, mx in halves], axis=1)
            return jnp.concatenate(rows, axis=0), p_sink

        def values(h, pr):
            p, p_sink = pr
            acc = _dot(v_of(h // group), p)
            return acc[:HEAD_DIM] * (1.0 / (acc[HEAD_DIM:HEAD_DIM + 1] + p_sink))

        scored, weighted, outs = {}, {}, {}
        for step in range(n_heads + 2 * GQA_LAG):
            h = step - 2 * GQA_LAG
            if 0 <= h < n_heads:
                outs[h] = values(h, weighted.pop(h))
                if h % 2 == 1:
                    pair_t = jnp.concatenate([outs.pop(h - 1), outs.pop(h)], axis=0)
                    o_ref[0, :, (h // 2) * LANES:(h // 2 + 1) * LANES] = pair_t.T.astype(BF16)
            if 0 <= step - GQA_LAG < n_heads:
                weighted[step - GQA_LAG] = probs(scored.pop(step - GQA_LAG))
            if step < n_heads:
                scored[step] = score(step)

    def with_kv(body, k_refs, v_refs):
        ks, vs = [], []
        for j in range(A_KV_HEADS):
            sl = slice(j * LANES, (j + 1) * LANES)
            ks.append(jnp.concatenate([r[0, :, sl] for r in k_refs], axis=0))
            vs.append(jnp.concatenate([r[0, sl, :] for r in v_refs], axis=1))
        body(ks.__getitem__, vs.__getitem__)

    @pl.when(t < ctx_tiles)
    def _():
        if ctx_out:
            with_kv(functools.partial(attend, ctx_layout), [kc_ref], [vc_ref])
        else:
            o_ref[...] = jnp.zeros(o_ref.shape, o_ref.dtype)

    @pl.when(t >= ctx_tiles)
    def _():
        with_kv(functools.partial(attend, ctx_layout + win_layout),
                [kc_ref, kp_ref, kt_ref, kn_ref], [vc_ref, vp_ref, vt_ref, vn_ref])


def _attn_a(q, k, vt, sink, seq, with_ctx):
    b, s_tot, qw = q.shape
    kw, nv = k.shape[2], vt.shape[1]
    ctx_tiles = (s_tot - seq) // TM
    n_tiles = s_tot // TM
    hb = TM // WINDOW
    n_half = s_tot // WINDOW
    prev = lambda t: jnp.maximum(t * hb - 1, 0)
    nxt = lambda t: jnp.minimum((t + 1) * hb, n_half - 1)
    k_specs = [pl.BlockSpec((1, TM, kw), lambda i, t: (i, 0, 0)),
               pl.BlockSpec((1, WINDOW, kw), lambda i, t: (i, prev(t), 0)),
               pl.BlockSpec((1, TM, kw), lambda i, t: (i, t, 0)),
               pl.BlockSpec((1, WINDOW, kw), lambda i, t: (i, nxt(t), 0))]
    v_specs = [pl.BlockSpec((1, nv, TM), lambda i, t: (i, 0, 0)),
               pl.BlockSpec((1, nv, WINDOW), lambda i, t: (i, 0, prev(t))),
               pl.BlockSpec((1, nv, TM), lambda i, t: (i, 0, t)),
               pl.BlockSpec((1, nv, WINDOW), lambda i, t: (i, 0, nxt(t)))]
    return pl.pallas_call(
        functools.partial(_attn_a_kernel, ctx_tiles=ctx_tiles, n_tiles=n_tiles, ctx_out=with_ctx),
        grid=(b, n_tiles),
        in_specs=[pl.BlockSpec(memory_space=pltpu.SMEM),
                  pl.BlockSpec((1, TM, qw), lambda i, t: (i, t, 0))] + k_specs + v_specs,
        out_specs=pl.BlockSpec((1, TM, qw), lambda i, t: (i, t, 0)),
        out_shape=jax.ShapeDtypeStruct((b, s_tot, qw), BF16),
        compiler_params=_params(2),
        name="attn_gqa",
    )(sink, q, k, k, k, k, vt, vt, vt, vt)


def _attn_b_kernel(q_ref, qn_ref, k_ref, vt_ref, o_ref, s_ref, run_ref, *, ctx_len, seq, n_tiles, ctx_out):
    t = pl.program_id(2)
    ctx_tiles = ctx_len // TM
    n_pairs = q_ref.shape[2] // (2 * LANES)
    slot = lambda h: h % s_ref.shape[0]
    assert (2 * n_pairs) % s_ref.shape[0] == 0

    def score_chunk(h, q_src, k0, tk, run):
        hs = slice(h * LANES, (h + 1) * LANES)
        s = _dot_nt(k_ref[0, k0:k0 + tk, hs], q_src[0, :, hs])
        s_ref[slot(h), k0:k0 + tk, :] = s
        for i in range(tk // MAX_FOLD):
            run = jnp.maximum(run, s[i * MAX_FOLD:(i + 1) * MAX_FOLD])
        return run

    def value_chunk(h, k0, tk, mx, acc):
        p = jnp.exp2(s_ref[slot(h), k0:k0 + tk, :] - mx).astype(BF16)
        c = _dot(vt_ref[0, h * LANES:(h + 1) * LANES, k0:k0 + tk], p)
        return c if acc is None else acc + c

    def attend(chunks, first_scored, score_next):
        init = jnp.full((MAX_FOLD, TM), NEG, F32)
        run = [run_ref[0], run_ref[1]] if first_scored else None
        for pr in range(1 if first_scored else 0, n_pairs + 1):
            mx = None if run is None else [jnp.max(r, axis=0, keepdims=True) for r in run]
            run, acc = [init, init], [None, None]
            for k0, tk in chunks:
                for e in range(2):
                    if pr < n_pairs:
                        run[e] = score_chunk(2 * pr + e, q_ref, k0, tk, run[e])
                    elif score_next:
                        run[e] = score_chunk(e, qn_ref, k0, tk, run[e])
                    if pr > 0:
                        acc[e] = value_chunk(2 * (pr - 1) + e, k0, tk, mx[e], acc[e])
            if pr > 0:
                out_t = jnp.concatenate([a[:B_V] * (1.0 / a[B_V:B_V + 1]) for a in acc], axis=0)
                o_ref[0, :, (pr - 1) * LANES:pr * LANES] = out_t.T.astype(BF16)
        if score_next:
            run_ref[0], run_ref[1] = run

    ctx_chunk = [(0, ctx_len)]
    all_chunks = ctx_chunk + [(ctx_len + i * TK_MLA, TK_MLA) for i in range(seq // TK_MLA)]

    @pl.when(t < ctx_tiles)
    def _():
        if ctx_out:
            attend(ctx_chunk, False, False)
        else:
            o_ref[...] = jnp.zeros(o_ref.shape, o_ref.dtype)

    pl.when(t == ctx_tiles)(lambda: attend(all_chunks, False, ctx_tiles < n_tiles - 1))
    pl.when((t > ctx_tiles) & (t < n_tiles - 1))(lambda: attend(all_chunks, True, True))
    pl.when((t > ctx_tiles) & (t == n_tiles - 1))(lambda: attend(all_chunks, True, False))


def _attn_b(q, k, vt, seq, with_ctx):
    b, s_tot, qw = q.shape
    ctx_len = s_tot - seq
    n_tiles = s_tot // TM
    heads = 2 * MLA_PAIRS
    wide = heads * LANES
    pair = 2 * LANES
    return pl.pallas_call(
        functools.partial(_attn_b_kernel, ctx_len=ctx_len, seq=seq, n_tiles=n_tiles, ctx_out=with_ctx),
        grid=(b, qw // wide, n_tiles),
        in_specs=[pl.BlockSpec((1, TM, wide), lambda i, h, t: (i, t, h)),
                  pl.BlockSpec((1, TM, pair), lambda i, h, t: (i, jnp.minimum(t + 1, n_tiles - 1), h * MLA_PAIRS)),
                  pl.BlockSpec((1, s_tot, wide), lambda i, h, t: (i, 0, h), pipeline_mode=pl.Buffered(1)),
                  pl.BlockSpec((1, wide, s_tot), lambda i, h, t: (i, h, 0), pipeline_mode=pl.Buffered(1))],
        out_specs=pl.BlockSpec((1, TM, heads * B_V), lambda i, h, t: (i, t, h)),
        out_shape=jax.ShapeDtypeStruct((b, s_tot, B_HEADS * B_V), BF16),
        scratch_shapes=[pltpu.VMEM((min(heads, 4), s_tot, TM), F32), pltpu.VMEM((2, MAX_FOLD, TM), F32)],
        compiler_params=pltpu.CompilerParams(dimension_semantics=("parallel", "parallel", "arbitrary"),
                                             vmem_limit_bytes=VMEM_LIMIT),
        name="attn_mla",
    )(q, q, k, vt)


def _channel_kernel(xp_ref, xt_ref, xn_ref, op_ref, ot_ref, on_ref, mod_ref, g_ref, wo_ref, wa_ref, wv_ref,
                    cw_ref, cb_ref, wout_ref, gf_ref, y_ref, *, ctx_tiles, n_tiles, t0, final):
    t = pl.program_id(1) + t0
    ext = TM + 2 * HALO
    m = mod_ref[0, 0]
    x_ext = jnp.concatenate([xp_ref[0], xt_ref[0], xn_ref[0]], axis=0)
    o_ext = jnp.concatenate([op_ref[0], ot_ref[0], on_ref[0]], axis=0)
    x1 = x_ext + m[2:3] * _dot(o_ext, wo_ref[...])
    h2 = _norm_mod(x1, g_ref[...], m[3:4], m[4:5])
    prev_ok = (t != 0) & (t != ctx_tiles)
    next_ok = (t != ctx_tiles - 1) & (t != n_tiles - 1)
    row = lax.broadcasted_iota(jnp.int32, (ext, 1), 0)
    keep = ((row >= HALO) | prev_ok) & ((row < HALO + TM) | next_ok)
    h2 = jnp.where(keep, h2, 0.0).astype(BF16)
    h2_mid = h2[HALO:HALO + TM]
    cw = cw_ref[...]
    cb = cb_ref[...]
    d_ff = wa_ref.shape[1]
    y = None
    for c0 in range(0, d_ff, FF_CHUNK):
        cs = slice(c0, min(c0 + FF_CHUNK, d_ff))
        a = _dot(h2, wa_ref[:, cs])
        a_prev = pltpu.roll(a, 1, 0)[HALO:HALO + TM]
        a_next = pltpu.roll(a, ext - 1, 0)[HALO:HALO + TM]
        conv = a_prev * cw[0:1, cs] + a[HALO:HALO + TM] * cw[1:2, cs] + a_next * cw[2:3, cs] + cb[:, cs]
        gate = conv * (1.0 / (1.0 + jnp.exp(-conv)))
        hid = (gate * _dot(h2_mid, wv_ref[:, cs])).astype(BF16)
        part = _dot(hid, wout_ref[cs, :])
        y = part if y is None else y + part
    x2 = x1[HALO:HALO + TM] + m[5:6] * y
    y_ref[0] = _rms(x2, gf_ref[...]) if final else x2


def _channel(xs, o, modl, g, wo, wa, wv, cw, cb, wout, gf, seq, final):
    b, s_tot, d = xs.shape
    n_tiles = s_tot // TM
    ctx_tiles = (s_tot - seq) // TM
    t0 = ctx_tiles if final else 0
    nt = n_tiles - t0
    hb = TM // HALO
    n_halo = s_tot // HALO
    prev = lambda i, t: (i, jnp.maximum((t + t0) * hb - 1, t0 * hb), 0)
    cur = lambda i, t: (i, t + t0, 0)
    nxt = lambda i, t: (i, jnp.minimum((t + t0 + 1) * hb, n_halo - 1), 0)
    out_rows = seq if final else s_tot
    return pl.pallas_call(
        functools.partial(_channel_kernel, ctx_tiles=ctx_tiles, n_tiles=n_tiles, t0=t0, final=final),
        grid=(b, nt),
        in_specs=[pl.BlockSpec((1, HALO, d), prev), pl.BlockSpec((1, TM, d), cur), pl.BlockSpec((1, HALO, d), nxt),
                  pl.BlockSpec((1, HALO, d), prev), pl.BlockSpec((1, TM, d), cur), pl.BlockSpec((1, HALO, d), nxt),
                  pl.BlockSpec((1, 1, 6, d), lambda i, t: (i, jnp.minimum(t + t0, 1), 0, 0)),
                  _resident(g.shape), _resident(wo.shape), _resident(wa.shape), _resident(wv.shape),
                  _resident(cw.shape), _resident(cb.shape), _resident(wout.shape), _resident(gf.shape)],
        out_specs=pl.BlockSpec((1, TM, d), lambda i, t: (i, t, 0)),
        out_shape=jax.ShapeDtypeStruct((b, out_rows, d), F32),
        compiler_params=_params(2),
        name="channel",
    )(xs, xs, xs, o, o, o, modl, g, wo, wa, wv, cw, cb, wout, gf)


def _axial_angles(rows, rot_dim):
    row = jnp.repeat(jnp.arange(rows), GRID_W).astype(F32)
    col = jnp.tile(jnp.arange(GRID_W), rows).astype(F32)
    n_freq = rot_dim // 4
    inv = ROPE_BASE ** (-jnp.arange(n_freq, dtype=F32) / n_freq)
    return jnp.concatenate([row[:, None] * inv, col[:, None] * inv], axis=-1)


def _with_ctx_rows(cos, sin, ctx_cos, ctx_len, q_scale):
    cos = jnp.concatenate([jnp.broadcast_to(ctx_cos, (ctx_len, LANES)), cos], axis=0)
    sin = jnp.concatenate([jnp.zeros((ctx_len, LANES), F32), sin], axis=0)
    return jnp.stack([cos * q_scale, sin * q_scale, cos, sin])


def _tables_a(seq, ctx_len):
    ang = _axial_angles(seq // GRID_W, HEAD_DIM)
    cos, sin = jnp.cos(ang), jnp.sin(ang)
    reps = LANES // HEAD_DIM
    cos_l = jnp.tile(cos, (1, 2 * reps))
    sin_l = jnp.tile(jnp.concatenate([-sin, sin], axis=1), (1, reps))
    return _with_ctx_rows(cos_l, sin_l, jnp.ones((1, LANES), F32), ctx_len, LOG2E * HEAD_DIM ** -0.5)


def _tables_b(seq, ctx_len):
    ang = _axial_angles(seq // GRID_W, B_ROPE)
    cos, sin = jnp.cos(ang), jnp.sin(ang)
    copy = LANES - B_NOPE - B_ROPE
    cos_l = jnp.concatenate([jnp.ones((seq, B_NOPE), F32), cos, cos, jnp.zeros((seq, copy), F32)], axis=1)
    sin_l = jnp.concatenate([jnp.zeros((seq, B_NOPE), F32), -sin, sin, jnp.zeros((seq, copy), F32)], axis=1)
    ctx_cos = jnp.concatenate([jnp.ones((1, B_NOPE + B_ROPE), F32), jnp.zeros((1, copy), F32)], axis=1)
    return _with_ctx_rows(cos_l, sin_l, ctx_cos, ctx_len, LOG2E * (B_NOPE + B_ROPE) ** -0.5)


def _deinterleave(n):
    return jnp.concatenate([jnp.arange(0, n, 2), jnp.arange(1, n, 2)])


def _weights_a(wqkv):
    d = wqkv.shape[0]
    kw = A_KV_HEADS * HEAD_DIM
    qw = wqkv.shape[1] - 2 * kw
    perm = _deinterleave(HEAD_DIM)
    wq = wqkv[:, :qw].reshape(d, -1, HEAD_DIM)[:, :, perm].reshape(d, qw)
    wk = wqkv[:, qw:qw + kw].reshape(d, A_KV_HEADS, HEAD_DIM)[:, :, perm]
    wkk = jnp.concatenate([wk, wk], axis=2).reshape(d, 2 * kw)
    wv = wqkv[:, qw + kw:].reshape(d, A_KV_HEADS, HEAD_DIM)
    wv = jnp.concatenate([wv, jnp.zeros((d, A_KV_HEADS, LANES - HEAD_DIM), F32)], axis=2)
    wvt = wv.reshape(d, A_KV_HEADS * LANES).T
    return jnp.concatenate([wq, wkk], axis=1).astype(BF16), wvt.astype(BF16), qw, 2 * kw


def _weights_b(wdown, wuq, wuk, wuv, q_lora, kv_lora):
    d = wdown.shape[0]
    perm = _deinterleave(B_ROPE)
    assert LANES - B_NOPE - B_ROPE == B_ROPE
    w_rope = wdown[:, q_lora + kv_lora:][:, perm]
    w_rope = jnp.concatenate([jnp.zeros((d, B_NOPE), F32), w_rope, w_rope], axis=1)
    wd = jnp.concatenate([wdown[:, :q_lora + kv_lora], w_rope], axis=1)
    uq = wuq.reshape(q_lora, B_HEADS, B_NOPE + B_ROPE)
    uq_rope = uq[:, :, B_NOPE:][:, :, perm]
    uq = jnp.concatenate([uq[:, :, :B_NOPE], uq_rope, uq_rope], axis=2)
    uk = wuk.reshape(kv_lora, B_HEADS, B_NOPE)
    uk = jnp.concatenate([uk, jnp.zeros((kv_lora, B_HEADS, LANES - B_NOPE), F32)], axis=2)
    uv = wuv.reshape(kv_lora, B_HEADS, B_V)
    uv = jnp.concatenate([uv, jnp.zeros((kv_lora, B_HEADS, LANES - B_V), F32)], axis=2)
    flat = lambda w: w.reshape(w.shape[0], B_HEADS * LANES).astype(BF16)
    return wd.astype(BF16), flat(uq), flat(uk), flat(uv).T


def kernel(x, c, ctx, c_ctx, mod_w, mod_b, norm1_g, norm2_g, a_wqkv, a_wo, a_sink, b_wdown, b_qnorm_g, b_wuq,
           b_kvnorm_g, b_wuk, b_wuv, b_wo, f_win, f_conv_w, f_conv_b, f_wout, final_g):
    bsz, seq, d = x.shape
    ctx_len = ctx.shape[1]
    depth = mod_w.shape[0]
    d_ff = f_wout.shape[1]
    assert ctx_len == TM and seq % TM == 0 and seq % TK_MLA == 0 and seq % GRID_W == 0

    cond_rows = -(-(bsz + 1) // 8) * 8
    cond = jnp.concatenate([c, c_ctx[None], jnp.zeros((cond_rows - bsz - 1, d), F32)], axis=0)
    mod = _modulation(cond, mod_w, mod_b)

    tab_a = _tables_a(seq, ctx_len)
    tab_b = _tables_b(seq, ctx_len)
    xs = None
    row = lambda v: v.reshape(1, -1)

    for i in range(depth):
        last = i == depth - 1
        lat = mod[i, :bsz].reshape(bsz, 6, d)
        cmod = jnp.broadcast_to(mod[i, bsz].reshape(1, 6, d), (bsz, 6, d))
        modl = jnp.stack([cmod, lat], axis=1)
        j = i // 2
        if i % 2 == 0:
            w, wvt, qw, kw = _weights_a(a_wqkv[j])
            if i == 0:
                xs, q, k, vt = _proj_a((ctx, x), modl, row(norm1_g[i]), w, wvt, tab_a, qw, kw)
            else:
                q, k, vt = _proj_a(xs, modl, row(norm1_g[i]), w, wvt, tab_a, qw, kw)
            o = _attn_a(q, k, vt, a_sink[j], seq, not last)
            wo = a_wo[j]
        else:
            q_lora, kv_lora = b_qnorm_g.shape[1], b_kvnorm_g.shape[1]
            wd, wuq, wuk, wuvt = _weights_b(b_wdown[j], b_wuq[j], b_wuk[j], b_wuv[j], q_lora, kv_lora)
            q, k, vt = _proj_b(xs, modl, row(norm1_g[i]), wd, row(b_qnorm_g[j]), row(b_kvnorm_g[j]),
                               wuq, wuk, wuvt, tab_b)
            o = _attn_b(q, k, vt, seq, not last)
            wo = b_wo[j]
        xs = _channel(xs, o, modl, row(norm2_g[i]), wo.astype(BF16), f_win[i][:, :d_ff].astype(BF16),
                      f_win[i][:, d_ff:].astype(BF16), f_conv_w[i], row(f_conv_b[i]), f_wout[i].astype(BF16),
                      row(final_g), seq, last)
    return xs
```

```python
import functools

import jax
import jax.numpy as jnp
from jax import lax
from jax.experimental import pallas as pl
from jax.experimental.pallas import tpu as pltpu

GRID_W = 64
HEAD_DIM = 64
A_KV_HEADS = 4
WINDOW = 128
B_HEADS = 16
B_NOPE = 64
B_ROPE = 32
B_V = 64
ROPE_BASE = 10000.0
EPS = 1e-6
NEG = -1e30
LOG2E = 1.4426950408889634

LANES = 128
V7X_VMEM_BYTES = 64 * 1024 * 1024
VMEM_LIMIT = V7X_VMEM_BYTES * 7 // 8
TM = 256
HALO = 16
TK_MLA = 1024
MLA_PAIRS = 4
GQA_LAG = 3
MAX_FOLD = 64
TM_CHANNEL = 512
FF_CHUNK = 768

F32 = jnp.float32
BF16 = jnp.bfloat16
_NT = (((1,), (1,)), ((), ()))


def _dot(a, b):
    return jnp.dot(a, b, preferred_element_type=F32)


def _dot_nt(a, b):
    return lax.dot_general(a, b, _NT, preferred_element_type=F32)


def _rms(xf, g):
    ms = jnp.mean(xf * xf, axis=-1, keepdims=True)
    return xf * lax.rsqrt(ms + EPS) * g


def _norm_mod(xf, g, shift, scale):
    return _rms(xf, g) * (1.0 + scale) + shift


def _params(n_axes):
    return pltpu.CompilerParams(dimension_semantics=("parallel",) * n_axes,
                                vmem_limit_bytes=VMEM_LIMIT)


def _resident(shape):
    nd = len(shape)
    return pl.BlockSpec(shape, lambda *_: (0,) * nd, pipeline_mode=pl.Buffered(1))


def _mod_kernel(c_ref, w_ref, b_ref, o_ref):
    c = c_ref[...]
    silu = c * (1.0 / (1.0 + jnp.exp(-c)))
    o_ref[0] = _dot(silu.astype(BF16), w_ref[0].astype(BF16)) + b_ref[0]


def _modulation(cond, mod_w, mod_b):
    depth, d, d6 = mod_w.shape
    rows = cond.shape[0]
    return pl.pallas_call(
        _mod_kernel,
        grid=(depth, d6 // d),
        in_specs=[pl.BlockSpec((rows, d), lambda i, j: (0, 0)),
                  pl.BlockSpec((1, d, d), lambda i, j: (i, 0, j)),
                  pl.BlockSpec((1, 1, d), lambda i, j: (i, 0, j))],
        out_specs=pl.BlockSpec((1, rows, d), lambda i, j: (i, 0, j)),
        out_shape=jax.ShapeDtypeStruct((depth, rows, d6), F32),
        compiler_params=_params(2),
        name="modulation",
    )(cond, mod_w, mod_b.reshape(depth, 1, d6))


def _rope_block(xb, cos, sin_signed, first_half, half):
    swapped = jnp.where(first_half, pltpu.roll(xb, LANES - half, 1), pltpu.roll(xb, half, 1))
    return xb * cos + swapped * sin_signed


def _rope_block_dup(xb, cos, sin_signed, half):
    return xb * cos + pltpu.roll(xb, LANES - half, 1) * sin_signed


def _proj_a_kernel(x_ref, mod_ref, g_ref, w_ref, wvt_ref, tab_ref, q_ref, k_ref, vt_ref, *, qw, kw):
    m = mod_ref[0, 0]
    h = _norm_mod(x_ref[0], g_ref[...], m[0:1], m[1:2]).astype(BF16)
    qk = _dot(h, w_ref[...])
    lane = lax.broadcasted_iota(jnp.int32, (TM, LANES), 1)
    first = (lane % HEAD_DIM) < (HEAD_DIM // 2)
    cq, sq, ck, sk = tab_ref[0], tab_ref[1], tab_ref[2], tab_ref[3]
    for j in range(qw // LANES):
        blk = qk[:, j * LANES:(j + 1) * LANES]
        q_ref[0, :, j * LANES:(j + 1) * LANES] = _rope_block(blk, cq, sq, first, HEAD_DIM // 2).astype(BF16)
    for j in range(kw // LANES):
        blk = qk[:, qw + j * LANES:qw + (j + 1) * LANES]
        k_ref[0, :, j * LANES:(j + 1) * LANES] = _rope_block_dup(blk, ck, sk, HEAD_DIM // 2).astype(BF16)
    vrow = lax.broadcasted_iota(jnp.int32, (wvt_ref.shape[0], 1), 0) % LANES
    vt_ref[0] = (_dot_nt(wvt_ref[...], h) + jnp.where(vrow == HEAD_DIM, 1.0, 0.0)).astype(BF16)


def _proj_a_first_kernel(ctx_ref, lat_ref, mod_ref, g_ref, w_ref, wvt_ref, tab_ref, xs_ref, q_ref, k_ref, vt_ref,
                         *, ctx_tiles, **kw):
    xs_ref[0] = jnp.where(pl.program_id(1) < ctx_tiles, ctx_ref[0], lat_ref[0])
    _proj_a_kernel(xs_ref, mod_ref, g_ref, w_ref, wvt_ref, tab_ref, q_ref, k_ref, vt_ref, **kw)


def _proj_a(xs, modl, g, w, wvt, tab, qw, kw):
    first = isinstance(xs, tuple)
    if first:
        ctx, lat = xs
        b, ctx_len, d = ctx.shape
        ctx_tiles = ctx_len // TM
        s_tot = ctx_len + lat.shape[1]
        x_args = (ctx, lat)
        x_specs = [pl.BlockSpec((1, TM, d), lambda i, t: (i, jnp.minimum(t, ctx_tiles - 1), 0)),
                   pl.BlockSpec((1, TM, d), lambda i, t: (i, jnp.maximum(t - ctx_tiles, 0), 0))]
        body = functools.partial(_proj_a_first_kernel, ctx_tiles=ctx_tiles, qw=qw, kw=kw)
    else:
        b, s_tot, d = xs.shape
        x_args = (xs,)
        x_specs = [pl.BlockSpec((1, TM, d), lambda i, t: (i, t, 0))]
        body = functools.partial(_proj_a_kernel, qw=qw, kw=kw)
    nt = s_tot // TM
    nv = wvt.shape[0]
    out = lambda n: jax.ShapeDtypeStruct((b, s_tot, n), BF16)
    ospec = lambda n: pl.BlockSpec((1, TM, n), lambda i, t: (i, t, 0))
    out_specs = [ospec(qw), ospec(kw), pl.BlockSpec((1, nv, TM), lambda i, t: (i, 0, t))]
    out_shape = [out(qw), out(kw), jax.ShapeDtypeStruct((b, nv, s_tot), BF16)]
    if first:
        out_specs.insert(0, ospec(d))
        out_shape.insert(0, jax.ShapeDtypeStruct((b, s_tot, d), F32))
    return pl.pallas_call(
        body,
        grid=(b, nt),
        in_specs=x_specs + [pl.BlockSpec((1, 1, 6, d), lambda i, t: (i, jnp.minimum(t, 1), 0, 0)),
                            _resident(g.shape), _resident(w.shape), _resident(wvt.shape),
                            pl.BlockSpec((4, TM, LANES), lambda i, t: (0, t, 0))],
        out_specs=out_specs,
        out_shape=out_shape,
        compiler_params=_params(2),
        name="proj_gqa",
    )(*x_args, modl, g, w, wvt, tab)


def _proj_b_kernel(x_ref, mod_ref, g_ref, wd_ref, gq_ref, gkv_ref, wuq_ref, wuk_ref, wuvt_ref, tab_ref,
                   q_ref, k_ref, vt_ref, *, q_lora, kv_lora):
    m = mod_ref[0, 0]
    h = _norm_mod(x_ref[0], g_ref[...], m[0:1], m[1:2]).astype(BF16)
    d = _dot(h, wd_ref[...])
    cq = _rms(d[:, :q_lora], gq_ref[...]).astype(BF16)
    ckv = _rms(d[:, q_lora:q_lora + kv_lora], gkv_ref[...]).astype(BF16)
    tq_c, tq_s, tk_c, tk_s = tab_ref[0], tab_ref[1], tab_ref[2], tab_ref[3]
    k_rope = _rope_block_dup(d[:, q_lora + kv_lora:], tk_c, tk_s, B_ROPE // 2)
    q = _dot(cq, wuq_ref[...])
    k = _dot(ckv, wuk_ref[...])
    for j in range(B_HEADS):
        sl = slice(j * LANES, (j + 1) * LANES)
        q_ref[0, :, sl] = _rope_block_dup(q[:, sl], tq_c, tq_s, B_ROPE // 2).astype(BF16)
        k_ref[0, :, sl] = (k[:, sl] + k_rope).astype(BF16)
    vrow = lax.broadcasted_iota(jnp.int32, (wuvt_ref.shape[0], 1), 0) % LANES
    vt_ref[0] = (_dot_nt(wuvt_ref[...], ckv) + jnp.where(vrow == B_V, 1.0, 0.0)).astype(BF16)


def _proj_b(xs, modl, g, wd, gq, gkv, wuq, wuk, wuvt, tab):
    b, s_tot, d = xs.shape
    nt = s_tot // TM
    out = lambda n: jax.ShapeDtypeStruct((b, s_tot, n), BF16)
    ospec = lambda n: pl.BlockSpec((1, TM, n), lambda i, t: (i, t, 0))
    nq, nv = wuq.shape[1], wuvt.shape[0]
    return pl.pallas_call(
        functools.partial(_proj_b_kernel, q_lora=gq.shape[1], kv_lora=gkv.shape[1]),
        grid=(b, nt),
        in_specs=[pl.BlockSpec((1, TM, d), lambda i, t: (i, t, 0)),
                  pl.BlockSpec((1, 1, 6, d), lambda i, t: (i, jnp.minimum(t, 1), 0, 0)),
                  _resident(g.shape), _resident(wd.shape), _resident(gq.shape), _resident(gkv.shape),
                  _resident(wuq.shape), _resident(wuk.shape), _resident(wuvt.shape),
                  pl.BlockSpec((4, TM, LANES), lambda i, t: (0, t, 0))],
        out_specs=[ospec(nq), ospec(nq), pl.BlockSpec((1, nv, TM), lambda i, t: (i, 0, t))],
        out_shape=[out(nq), out(nq), jax.ShapeDtypeStruct((b, nv, s_tot), BF16)],
        compiler_params=_params(2),
        name="proj_mla",
    )(xs, modl, g, wd, gq, gkv, wuq, wuk, wuvt, tab)


def _attn_a_kernel(sink_ref, q_ref, kc_ref, kp_ref, kt_ref, kn_ref, vc_ref, vp_ref, vt_ref, vn_ref, o_ref,
                   *, ctx_tiles, n_tiles, ctx_out):
    t = pl.program_id(1)
    assert TM == 2 * WINDOW
    edge = WINDOW
    n_heads = q_ref.shape[2] // HEAD_DIM
    group = n_heads // A_KV_HEADS
    lane = lax.broadcasted_iota(jnp.int32, (1, LANES), 1)
    lo = lane < HEAD_DIM
    zero = jnp.zeros((), BF16)

    key = lax.broadcasted_iota(jnp.int32, (edge, edge), 0)
    qry = lax.broadcasted_iota(jnp.int32, (edge, edge), 1)
    tri = lambda valid: jnp.where(valid, 0.0, NEG).astype(F32)
    ALL, NONE = "all", "none"
    ctx_layout = [(ALL, ALL)] * (kc_ref.shape[1] // edge)
    win_layout = [(tri((qry <= key) & (t > ctx_tiles)), NONE),
                  (ALL, tri(qry <= key)), (tri(qry >= key), ALL),
                  (NONE, tri((qry >= key) & (t < n_tiles - 1)))]

    def attend(layout, k_of, v_of):
        def score(h):
            qp = q_ref[0, :, (h // 2) * LANES:(h // 2 + 1) * LANES]
            qe = jnp.where(lo, qp, zero) if h % 2 == 0 else jnp.where(lo, zero, qp)
            s = _dot_nt(k_of(h // group), qe)
            sink = sink_ref[h] * LOG2E
            halves = []
            for half in range(2):
                blocks, run = [], None
                for rb, kinds in enumerate(layout):
                    kind = kinds[half]
                    if isinstance(kind, str) and kind == NONE:
                        blocks.append(None)
                        continue
                    blk = s[rb * edge:(rb + 1) * edge, half * edge:(half + 1) * edge]
                    if not isinstance(kind, str):
                        blk = blk + kind
                    blocks.append(blk)
                    for i in range(edge // MAX_FOLD):
                        part = blk[i * MAX_FOLD:(i + 1) * MAX_FOLD]
                        run = part if run is None else jnp.maximum(run, part)
                halves.append((blocks, jnp.maximum(jnp.max(run, axis=0, keepdims=True), sink)))
            return halves, sink

        def probs(st):
            halves, sink = st
            rows = []
            for rb in range(len(layout)):
                parts = [jnp.zeros((edge, edge), BF16) if blocks[rb] is None
                         else jnp.exp2(blocks[rb] - mx).astype(BF16) for blocks, mx in halves]
                rows.append(jnp.concatenate(parts, axis=1))
            p_sink = jnp.concatenate([jnp.exp2(sink - mx) for _, mx in halves], axis=1)
            return jnp.concatenate(rows, axis=0), p_sink

        def values(h, pr):
            p, p_sink = pr
            acc = _dot(v_of(h // group), p)
            return acc[:HEAD_DIM] * (1.0 / (acc[HEAD_DIM:HEAD_DIM + 1] + p_sink))

        scored, weighted, outs = {}, {}, {}
        for step in range(n_heads + 2 * GQA_LAG):
            h = step - 2 * GQA_LAG
            if 0 <= h < n_heads:
                outs[h] = values(h, weighted.pop(h))
                if h % 2 == 1:
                    pair_t = jnp.concatenate([outs.pop(h - 1), outs.pop(h)], axis=0)
                    o_ref[0, :, (h // 2) * LANES:(h // 2 + 1) * LANES] = pair_t.T.astype(BF16)
            if 0 <= step - GQA_LAG < n_heads:
                weighted[step - GQA_LAG] = probs(scored.pop(step - GQA_LAG))
            if step < n_heads:
                scored[step] = score(step)

    def with_kv(body, k_refs, v_refs):
        ks, vs = [], []
        for j in range(A_KV_HEADS):
            sl = slice(j * LANES, (j + 1) * LANES)
            ks.append(jnp.concatenate([r[0, :, sl] for r in k_refs], axis=0))
            vs.append(jnp.concatenate([r[0, sl, :] for r in v_refs], axis=1))
        body(ks.__getitem__, vs.__getitem__)

    @pl.when(t < ctx_tiles)
    def _():
        if ctx_out:
            with_kv(functools.partial(attend, ctx_layout), [kc_ref], [vc_ref])
        else:
            o_ref[...] = jnp.zeros(o_ref.shape, o_ref.dtype)

    @pl.when(t >= ctx_tiles)
    def _():
        with_kv(functools.partial(attend, ctx_layout + win_layout),
                [kc_ref, kp_ref, kt_ref, kn_ref], [vc_ref, vp_ref, vt_ref, vn_ref])


def _attn_a(q, k, vt, sink, seq, with_ctx):
    b, s_tot, qw = q.shape
    kw, nv = k.shape[2], vt.shape[1]
    ctx_tiles = (s_tot - seq) // TM
    n_tiles = s_tot // TM
    hb = TM // WINDOW
    n_half = s_tot // WINDOW
    prev = lambda t: jnp.maximum(t * hb - 1, 0)
    nxt = lambda t: jnp.minimum((t + 1) * hb, n_half - 1)
    k_specs = [pl.BlockSpec((1, TM, kw), lambda i, t: (i, 0, 0)),
               pl.BlockSpec((1, WINDOW, kw), lambda i, t: (i, prev(t), 0)),
               pl.BlockSpec((1, TM, kw), lambda i, t: (i, t, 0)),
               pl.BlockSpec((1, WINDOW, kw), lambda i, t: (i, nxt(t), 0))]
    v_specs = [pl.BlockSpec((1, nv, TM), lambda i, t: (i, 0, 0)),
               pl.BlockSpec((1, nv, WINDOW), lambda i, t: (i, 0, prev(t))),
               pl.BlockSpec((1, nv, TM), lambda i, t: (i, 0, t)),
               pl.BlockSpec((1, nv, WINDOW), lambda i, t: (i, 0, nxt(t)))]
    return pl.pallas_call(
        functools.partial(_attn_a_kernel, ctx_tiles=ctx_tiles, n_tiles=n_tiles, ctx_out=with_ctx),
        grid=(b, n_tiles),
        in_specs=[pl.BlockSpec(memory_space=pltpu.SMEM),
                  pl.BlockSpec((1, TM, qw), lambda i, t: (i, t, 0))] + k_specs + v_specs,
        out_specs=pl.BlockSpec((1, TM, qw), lambda i, t: (i, t, 0)),
        out_shape=jax.ShapeDtypeStruct((b, s_tot, qw), BF16),
        compiler_params=_params(2),
        name="attn_gqa",
    )(sink, q, k, k, k, k, vt, vt, vt, vt)


def _attn_b_kernel(q_ref, qn_ref, k_ref, vt_ref, o_ref, s_ref, run_ref, *, ctx_len, seq, n_tiles, ctx_out):
    t = pl.program_id(2)
    ctx_tiles = ctx_len // TM
    n_pairs = q_ref.shape[2] // (2 * LANES)
    slot = lambda h: h % s_ref.shape[0]
    assert (2 * n_pairs) % s_ref.shape[0] == 0

    def score_chunk(h, q_src, k0, tk, run):
        hs = slice(h * LANES, (h + 1) * LANES)
        s = _dot_nt(k_ref[0, k0:k0 + tk, hs], q_src[0, :, hs])
        s_ref[slot(h), k0:k0 + tk, :] = s
        for i in range(tk // MAX_FOLD):
            run = jnp.maximum(run, s[i * MAX_FOLD:(i + 1) * MAX_FOLD])
        return run

    def value_chunk(h, k0, tk, mx, acc):
        p = jnp.exp2(s_ref[slot(h), k0:k0 + tk, :] - mx).astype(BF16)
        c = _dot(vt_ref[0, h * LANES:(h + 1) * LANES, k0:k0 + tk], p)
        return c if acc is None else acc + c

    def attend(chunks, first_scored, score_next):
        init = jnp.full((MAX_FOLD, TM), NEG, F32)
        run = [run_ref[0], run_ref[1]] if first_scored else None
        for pr in range(1 if first_scored else 0, n_pairs + 1):
            mx = None if run is None else [jnp.max(r, axis=0, keepdims=True) for r in run]
            run, acc = [init, init], [None, None]
            for k0, tk in chunks:
                for e in range(2):
                    if pr < n_pairs:
                        run[e] = score_chunk(2 * pr + e, q_ref, k0, tk, run[e])
                    elif score_next:
                        run[e] = score_chunk(e, qn_ref, k0, tk, run[e])
                    if pr > 0:
                        acc[e] = value_chunk(2 * (pr - 1) + e, k0, tk, mx[e], acc[e])
            if pr > 0:
                out_t = jnp.concatenate([a[:B_V] * (1.0 / a[B_V:B_V + 1]) for a in acc], axis=0)
                o_ref[0, :, (pr - 1) * LANES:pr * LANES] = out_t.T.astype(BF16)
        if score_next:
            run_ref[0], run_ref[1] = run

    ctx_chunk = [(0, ctx_len)]
    all_chunks = ctx_chunk + [(ctx_len + i * TK_MLA, TK_MLA) for i in range(seq // TK_MLA)]

    @pl.when(t < ctx_tiles)
    def _():
        if ctx_out:
            attend(ctx_chunk, False, False)
        else:
            o_ref[...] = jnp.zeros(o_ref.shape, o_ref.dtype)

    pl.when(t == ctx_tiles)(lambda: attend(all_chunks, False, ctx_tiles < n_tiles - 1))
    pl.when((t > ctx_tiles) & (t < n_tiles - 1))(lambda: attend(all_chunks, True, True))
    pl.when((t > ctx_tiles) & (t == n_tiles - 1))(lambda: attend(all_chunks, True, False))


def _attn_b(q, k, vt, seq, with_ctx):
    b, s_tot, qw = q.shape
    ctx_len = s_tot - seq
    n_tiles = s_tot // TM
    heads = 2 * MLA_PAIRS
    wide = heads * LANES
    pair = 2 * LANES
    return pl.pallas_call(
        functools.partial(_attn_b_kernel, ctx_len=ctx_len, seq=seq, n_tiles=n_tiles, ctx_out=with_ctx),
        grid=(b, qw // wide, n_tiles),
        in_specs=[pl.BlockSpec((1, TM, wide), lambda i, h, t: (i, t, h)),
                  pl.BlockSpec((1, TM, pair), lambda i, h, t: (i, jnp.minimum(t + 1, n_tiles - 1), h * MLA_PAIRS)),
                  pl.BlockSpec((1, s_tot, wide), lambda i, h, t: (i, 0, h), pipeline_mode=pl.Buffered(1)),
                  pl.BlockSpec((1, wide, s_tot), lambda i, h, t: (i, h, 0), pipeline_mode=pl.Buffered(1))],
        out_specs=pl.BlockSpec((1, TM, heads * B_V), lambda i, h, t: (i, t, h)),
        out_shape=jax.ShapeDtypeStruct((b, s_tot, B_HEADS * B_V), BF16),
        scratch_shapes=[pltpu.VMEM((min(heads, 4), s_tot, TM), F32), pltpu.VMEM((2, MAX_FOLD, TM), F32)],
        compiler_params=pltpu.CompilerParams(dimension_semantics=("parallel", "parallel", "arbitrary"),
                                             vmem_limit_bytes=VMEM_LIMIT),
        name="attn_mla",
    )(q, q, k, vt)


def _channel_kernel(*refs, rows, seg_tiles, final):
    (xp_ref, xt_ref, xn_ref, op_ref, ot_ref, on_ref, mod_ref, g_ref, wo_ref, wa_ref, wv_ref,
     cw_ref, cb_ref, wout_ref, gf_ref), y_ref = refs[:15], refs[-1]
    t = pl.program_id(1)
    ext = rows + 2 * HALO
    mid = slice(HALO, HALO + rows)
    m = mod_ref[0, 0]
    x_ext = jnp.concatenate([xp_ref[0], xt_ref[0], xn_ref[0]], axis=0)
    o_ext = jnp.concatenate([op_ref[0], ot_ref[0], on_ref[0]], axis=0)
    x1 = x_ext + m[2:3] * _dot(o_ext, wo_ref[...])
    h2 = _norm_mod(x1, g_ref[...], m[3:4], m[4:5])
    row = lax.broadcasted_iota(jnp.int32, (ext, 1), 0)
    keep = ((row >= HALO) | (t > 0)) & ((row < HALO + rows) | (t < seg_tiles - 1))
    h2 = jnp.where(keep, h2, 0.0).astype(BF16)
    h2_mid = h2[mid]
    cw = cw_ref[...]
    cb = cb_ref[...]
    d_ff = wa_ref.shape[1]
    y = None
    for c0 in range(0, d_ff, FF_CHUNK):
        cs = slice(c0, min(c0 + FF_CHUNK, d_ff))
        a = _dot(h2, wa_ref[:, cs])
        a_prev = pltpu.roll(a, 1, 0)[mid]
        a_next = pltpu.roll(a, ext - 1, 0)[mid]
        conv = a_prev * cw[0:1, cs] + a[mid] * cw[1:2, cs] + a_next * cw[2:3, cs] + cb[:, cs]
        gate = conv * (1.0 / (1.0 + jnp.exp(-conv)))
        hid = (gate * _dot(h2_mid, wv_ref[:, cs])).astype(BF16)
        part = _dot(hid, wout_ref[cs, :])
        y = part if y is None else y + part
    x2 = x1[mid] + m[5:6] * y
    y_ref[0] = _rms(x2, gf_ref[...]) if final else x2


def _rows_at(n_rows, width, offset):
    return pl.BlockSpec((pl.Element(1), pl.Element(n_rows), pl.Element(width)),
                        lambda i, t: (i, pl.multiple_of(offset(t), HALO), 0))


def _channel(xs, o, modl, g, wo, wa, wv, cw, cb, wout, gf, seq, final):
    b, s_tot, d = xs.shape
    ctx_len = s_tot - seq
    weights = (g, wo, wa, wv, cw, cb, wout, gf)
    w_specs = [_resident(w.shape) for w in weights]

    def call(rows, base, seg_rows, mod_row, out_spec, out_rows, alias):
        seg_tiles = seg_rows // rows
        prev = lambda t: jnp.maximum(base + t * rows - HALO, 0)
        cur = lambda t: base + t * rows
        nxt = lambda t: jnp.minimum(base + (t + 1) * rows, base + seg_rows - HALO)
        tiles = lambda w: [_rows_at(HALO, w, prev), _rows_at(rows, w, cur), _rows_at(HALO, w, nxt)]
        in_specs = (tiles(d) + tiles(o.shape[2])
                    + [pl.BlockSpec((1, 1, 6, d), lambda i, t: (i, mod_row, 0, 0))] + w_specs)
        args = (xs, xs, xs, o, o, o, modl) + weights
        if alias is not None:
            in_specs.append(pl.BlockSpec(memory_space=pl.ANY))
            args += (alias,)
        return pl.pallas_call(
            functools.partial(_channel_kernel, rows=rows, seg_tiles=seg_tiles, final=final),
            grid=(b, seg_tiles),
            in_specs=in_specs,
            out_specs=out_spec,
            out_shape=jax.ShapeDtypeStruct((b, out_rows, d), F32),
            input_output_aliases={} if alias is None else {len(args) - 1: 0},
            compiler_params=_params(2),
            name="channel",
        )(*args)

    if final:
        return call(TM_CHANNEL, ctx_len, seq, 1, pl.BlockSpec((1, TM_CHANNEL, d), lambda i, t: (i, t, 0)), seq, None)
    y = call(TM_CHANNEL, ctx_len, seq, 1, _rows_at(TM_CHANNEL, d, lambda t: ctx_len + t * TM_CHANNEL), s_tot, None)
    return call(TM, 0, ctx_len, 0, pl.BlockSpec((1, TM, d), lambda i, t: (i, t, 0)), s_tot, y)


def _axial_angles(rows, rot_dim):
    row = jnp.repeat(jnp.arange(rows), GRID_W).astype(F32)
    col = jnp.tile(jnp.arange(GRID_W), rows).astype(F32)
    n_freq = rot_dim // 4
    inv = ROPE_BASE ** (-jnp.arange(n_freq, dtype=F32) / n_freq)
    return jnp.concatenate([row[:, None] * inv, col[:, None] * inv], axis=-1)


def _with_ctx_rows(cos, sin, ctx_cos, ctx_len, q_scale):
    cos = jnp.concatenate([jnp.broadcast_to(ctx_cos, (ctx_len, LANES)), cos], axis=0)
    sin = jnp.concatenate([jnp.zeros((ctx_len, LANES), F32), sin], axis=0)
    return jnp.stack([cos * q_scale, sin * q_scale, cos, sin])


def _tables_a(seq, ctx_len):
    ang = _axial_angles(seq // GRID_W, HEAD_DIM)
    cos, sin = jnp.cos(ang), jnp.sin(ang)
    reps = LANES // HEAD_DIM
    cos_l = jnp.tile(cos, (1, 2 * reps))
    sin_l = jnp.tile(jnp.concatenate([-sin, sin], axis=1), (1, reps))
    return _with_ctx_rows(cos_l, sin_l, jnp.ones((1, LANES), F32), ctx_len, LOG2E * HEAD_DIM ** -0.5)


def _tables_b(seq, ctx_len):
    ang = _axial_angles(seq // GRID_W, B_ROPE)
    cos, sin = jnp.cos(ang), jnp.sin(ang)
    copy = LANES - B_NOPE - B_ROPE
    cos_l = jnp.concatenate([jnp.ones((seq, B_NOPE), F32), cos, cos, jnp.zeros((seq, copy), F32)], axis=1)
    sin_l = jnp.concatenate([jnp.zeros((seq, B_NOPE), F32), -sin, sin, jnp.zeros((seq, copy), F32)], axis=1)
    ctx_cos = jnp.concatenate([jnp.ones((1, B_NOPE + B_ROPE), F32), jnp.zeros((1, copy), F32)], axis=1)
    return _with_ctx_rows(cos_l, sin_l, ctx_cos, ctx_len, LOG2E * (B_NOPE + B_ROPE) ** -0.5)


def _deinterleave(n):
    return jnp.concatenate([jnp.arange(0, n, 2), jnp.arange(1, n, 2)])


def _weights_a(wqkv):
    d = wqkv.shape[0]
    kw = A_KV_HEADS * HEAD_DIM
    qw = wqkv.shape[1] - 2 * kw
    perm = _deinterleave(HEAD_DIM)
    wq = wqkv[:, :qw].reshape(d, -1, HEAD_DIM)[:, :, perm].reshape(d, qw)
    wk = wqkv[:, qw:qw + kw].reshape(d, A_KV_HEADS, HEAD_DIM)[:, :, perm]
    wkk = jnp.concatenate([wk, wk], axis=2).reshape(d, 2 * kw)
    wv = wqkv[:, qw + kw:].reshape(d, A_KV_HEADS, HEAD_DIM)
    wv = jnp.concatenate([wv, jnp.zeros((d, A_KV_HEADS, LANES - HEAD_DIM), F32)], axis=2)
    wvt = wv.reshape(d, A_KV_HEADS * LANES).T
    return jnp.concatenate([wq, wkk], axis=1).astype(BF16), wvt.astype(BF16), qw, 2 * kw


def _weights_b(wdown, wuq, wuk, wuv, q_lora, kv_lora):
    d = wdown.shape[0]
    perm = _deinterleave(B_ROPE)
    assert LANES - B_NOPE - B_ROPE == B_ROPE
    w_rope = wdown[:, q_lora + kv_lora:][:, perm]
    w_rope = jnp.concatenate([jnp.zeros((d, B_NOPE), F32), w_rope, w_rope], axis=1)
    wd = jnp.concatenate([wdown[:, :q_lora + kv_lora], w_rope], axis=1)
    uq = wuq.reshape(q_lora, B_HEADS, B_NOPE + B_ROPE)
    uq_rope = uq[:, :, B_NOPE:][:, :, perm]
    uq = jnp.concatenate([uq[:, :, :B_NOPE], uq_rope, uq_rope], axis=2)
    uk = wuk.reshape(kv_lora, B_HEADS, B_NOPE)
    uk = jnp.concatenate([uk, jnp.zeros((kv_lora, B_HEADS, LANES - B_NOPE), F32)], axis=2)
    uv = wuv.reshape(kv_lora, B_HEADS, B_V)
    uv = jnp.concatenate([uv, jnp.zeros((kv_lora, B_HEADS, LANES - B_V), F32)], axis=2)
    flat = lambda w: w.reshape(w.shape[0], B_HEADS * LANES).astype(BF16)
    return wd.astype(BF16), flat(uq), flat(uk), flat(uv).T


def kernel(x, c, ctx, c_ctx, mod_w, mod_b, norm1_g, norm2_g, a_wqkv, a_wo, a_sink, b_wdown, b_qnorm_g, b_wuq,
           b_kvnorm_g, b_wuk, b_wuv, b_wo, f_win, f_conv_w, f_conv_b, f_wout, final_g):
    bsz, seq, d = x.shape
    ctx_len = ctx.shape[1]
    depth = mod_w.shape[0]
    d_ff = f_wout.shape[1]
    assert ctx_len == TM and seq % TK_MLA == 0 and seq % TM_CHANNEL == 0 and seq % GRID_W == 0

    cond_rows = -(-(bsz + 1) // 8) * 8
    cond = jnp.concatenate([c, c_ctx[None], jnp.zeros((cond_rows - bsz - 1, d), F32)], axis=0)
    mod = _modulation(cond, mod_w, mod_b)

    tab_a = _tables_a(seq, ctx_len)
    tab_b = _tables_b(seq, ctx_len)
    xs = None
    row = lambda v: v.reshape(1, -1)

    for i in range(depth):
        last = i == depth - 1
        lat = mod[i, :bsz].reshape(bsz, 6, d)
        cmod = jnp.broadcast_to(mod[i, bsz].reshape(1, 6, d), (bsz, 6, d))
        modl = jnp.stack([cmod, lat], axis=1)
        j = i // 2
        if i % 2 == 0:
            w, wvt, qw, kw = _weights_a(a_wqkv[j])
            if i == 0:
                xs, q, k, vt = _proj_a((ctx, x), modl, row(norm1_g[i]), w, wvt, tab_a, qw, kw)
            else:
                q, k, vt = _proj_a(xs, modl, row(norm1_g[i]), w, wvt, tab_a, qw, kw)
            o = _attn_a(q, k, vt, a_sink[j], seq, not last)
            wo = a_wo[j]
        else:
            q_lora, kv_lora = b_qnorm_g.shape[1], b_kvnorm_g.shape[1]
            wd, wuq, wuk, wuvt = _weights_b(b_wdown[j], b_wuq[j], b_wuk[j], b_wuv[j], q_lora, kv_lora)
            q, k, vt = _proj_b(xs, modl, row(norm1_g[i]), wd, row(b_qnorm_g[j]), row(b_kvnorm_g[j]),
                               wuq, wuk, wuvt, tab_b)
            o = _attn_b(q, k, vt, seq, not last)
            wo = b_wo[j]
        xs = _channel(xs, o, modl, row(norm2_g[i]), wo.astype(BF16), f_win[i][:, :d_ff].astype(BF16),
                      f_win[i][:, d_ff:].astype(BF16), f_conv_w[i], row(f_conv_b[i]), f_wout[i].astype(BF16),
                      row(final_g), seq, last)
    return xs
```

```python
import functools

import jax
import jax.numpy as jnp
from jax import lax
from jax.experimental import pallas as pl
from jax.experimental.pallas import tpu as pltpu

GRID_W = 64
HEAD_DIM = 64
A_KV_HEADS = 4
WINDOW = 128
B_HEADS = 16
B_NOPE = 64
B_ROPE = 32
B_V = 64
ROPE_BASE = 10000.0
EPS = 1e-6
NEG = -1e30
LOG2E = 1.4426950408889634

LANES = 128
SUBLANES = 8
V7X_VMEM_BYTES = 64 * 1024 * 1024
VMEM_LIMIT = V7X_VMEM_BYTES * 7 // 8
TM = 256
HALO = 16
TK_MLA = 1024
MLA_PAIRS = 4
GQA_LAG = 3
MAX_FOLD = 64
TM_CHANNEL = 512
FF_CHUNK = 768

F32 = jnp.float32
BF16 = jnp.bfloat16
_NT = (((1,), (1,)), ((), ()))


def _dot(a, b):
    return jnp.dot(a, b, preferred_element_type=F32)


def _dot_nt(a, b):
    return lax.dot_general(a, b, _NT, preferred_element_type=F32)


def _rms(xf, g):
    ms = jnp.mean(xf * xf, axis=-1, keepdims=True)
    return xf * lax.rsqrt(ms + EPS) * g


def _norm_mod(xf, g, shift, scale):
    return _rms(xf, g) * (1.0 + scale) + shift


def _params(n_axes):
    return pltpu.CompilerParams(dimension_semantics=("parallel",) * n_axes,
                                vmem_limit_bytes=VMEM_LIMIT)


def _resident(shape):
    nd = len(shape)
    return pl.BlockSpec(shape, lambda *_: (0,) * nd, pipeline_mode=pl.Buffered(1))


def _mod_kernel(c_ref, w_ref, b_ref, o_ref):
    c = c_ref[...]
    silu = c * (1.0 / (1.0 + jnp.exp(-c)))
    o_ref[0] = _dot(silu.astype(BF16), w_ref[0].astype(BF16)) + b_ref[0]


def _modulation(cond, mod_w, mod_b):
    depth, d, d6 = mod_w.shape
    rows = cond.shape[0]
    return pl.pallas_call(
        _mod_kernel,
        grid=(depth, d6 // d),
        in_specs=[pl.BlockSpec((rows, d), lambda i, j: (0, 0)),
                  pl.BlockSpec((1, d, d), lambda i, j: (i, 0, j)),
                  pl.BlockSpec((1, 1, d), lambda i, j: (i, 0, j))],
        out_specs=pl.BlockSpec((1, rows, d), lambda i, j: (i, 0, j)),
        out_shape=jax.ShapeDtypeStruct((depth, rows, d6), F32),
        compiler_params=_params(2),
        name="modulation",
    )(cond, mod_w, mod_b.reshape(depth, 1, d6))


def _rope_block(xb, cos, sin_signed, first_half, half):
    swapped = jnp.where(first_half, pltpu.roll(xb, LANES - half, 1), pltpu.roll(xb, half, 1))
    return xb * cos + swapped * sin_signed


def _rope_block_dup(xb, cos, sin_signed, half):
    return xb * cos + pltpu.roll(xb, LANES - half, 1) * sin_signed


def _proj_a_kernel(x_ref, mod_ref, g_ref, w_ref, wvt_ref, tab_ref, q_ref, k_ref, vt_ref, *, qw, kw):
    m = mod_ref[0, 0]
    h = _norm_mod(x_ref[0], g_ref[...], m[0:1], m[1:2]).astype(BF16)
    qk = _dot(h, w_ref[...])
    lane = lax.broadcasted_iota(jnp.int32, (TM, LANES), 1)
    first = (lane % HEAD_DIM) < (HEAD_DIM // 2)
    cq, sq, ck, sk = tab_ref[0], tab_ref[1], tab_ref[2], tab_ref[3]
    for j in range(qw // LANES):
        blk = qk[:, j * LANES:(j + 1) * LANES]
        q_ref[0, :, j * LANES:(j + 1) * LANES] = _rope_block(blk, cq, sq, first, HEAD_DIM // 2).astype(BF16)
    for j in range(kw // LANES):
        blk = qk[:, qw + j * LANES:qw + (j + 1) * LANES]
        k_ref[0, :, j * LANES:(j + 1) * LANES] = _rope_block_dup(blk, ck, sk, HEAD_DIM // 2).astype(BF16)
    vrow = lax.broadcasted_iota(jnp.int32, (wvt_ref.shape[0], 1), 0) % LANES
    vt_ref[0] = (_dot_nt(wvt_ref[...], h) + jnp.where(vrow == HEAD_DIM, 1.0, 0.0)).astype(BF16)


def _proj_a_first_kernel(ctx_ref, lat_ref, mod_ref, g_ref, w_ref, wvt_ref, tab_ref, xs_ref, q_ref, k_ref, vt_ref,
                         *, ctx_tiles, **kw):
    xs_ref[0] = jnp.where(pl.program_id(1) < ctx_tiles, ctx_ref[0], lat_ref[0])
    _proj_a_kernel(xs_ref, mod_ref, g_ref, w_ref, wvt_ref, tab_ref, q_ref, k_ref, vt_ref, **kw)


def _proj_a(xs, modl, g, w, wvt, tab, qw, kw):
    first = isinstance(xs, tuple)
    if first:
        ctx, lat = xs
        b, ctx_len, d = ctx.shape
        ctx_tiles = ctx_len // TM
        s_tot = ctx_len + lat.shape[1]
        x_args = (ctx, lat)
        x_specs = [pl.BlockSpec((1, TM, d), lambda i, t: (i, jnp.minimum(t, ctx_tiles - 1), 0)),
                   pl.BlockSpec((1, TM, d), lambda i, t: (i, jnp.maximum(t - ctx_tiles, 0), 0))]
        body = functools.partial(_proj_a_first_kernel, ctx_tiles=ctx_tiles, qw=qw, kw=kw)
    else:
        b, s_tot, d = xs.shape
        x_args = (xs,)
        x_specs = [pl.BlockSpec((1, TM, d), lambda i, t: (i, t, 0))]
        body = functools.partial(_proj_a_kernel, qw=qw, kw=kw)
    nt = s_tot // TM
    nv = wvt.shape[0]
    out = lambda n: jax.ShapeDtypeStruct((b, s_tot, n), BF16)
    ospec = lambda n: pl.BlockSpec((1, TM, n), lambda i, t: (i, t, 0))
    out_specs = [ospec(qw), ospec(kw), pl.BlockSpec((1, nv, TM), lambda i, t: (i, 0, t))]
    out_shape = [out(qw), out(kw), jax.ShapeDtypeStruct((b, nv, s_tot), BF16)]
    if first:
        out_specs.insert(0, ospec(d))
        out_shape.insert(0, jax.ShapeDtypeStruct((b, s_tot, d), F32))
    return pl.pallas_call(
        body,
        grid=(b, nt),
        in_specs=x_specs + [pl.BlockSpec((1, 1, 6, d), lambda i, t: (i, jnp.minimum(t, 1), 0, 0)),
                            _resident(g.shape), _resident(w.shape), _resident(wvt.shape),
                            pl.BlockSpec((4, TM, LANES), lambda i, t: (0, t, 0))],
        out_specs=out_specs,
        out_shape=out_shape,
        compiler_params=_params(2),
        name="proj_gqa",
    )(*x_args, modl, g, w, wvt, tab)


def _proj_b_kernel(x_ref, mod_ref, g_ref, wd_ref, gq_ref, gkv_ref, wuq_ref, wuk_ref, wuvt_ref, tab_ref,
                   q_ref, k_ref, vt_ref, *, q_lora, kv_lora):
    m = mod_ref[0, 0]
    h = _norm_mod(x_ref[0], g_ref[...], m[0:1], m[1:2]).astype(BF16)
    d = _dot(h, wd_ref[...])
    cq = _rms(d[:, :q_lora], gq_ref[...]).astype(BF16)
    ckv = _rms(d[:, q_lora:q_lora + kv_lora], gkv_ref[...]).astype(BF16)
    tq_c, tq_s, tk_c, tk_s = tab_ref[0], tab_ref[1], tab_ref[2], tab_ref[3]
    k_rope = _rope_block_dup(d[:, q_lora + kv_lora:], tk_c, tk_s, B_ROPE // 2)
    q = _dot(cq, wuq_ref[...])
    k = _dot(ckv, wuk_ref[...])
    for j in range(B_HEADS):
        sl = slice(j * LANES, (j + 1) * LANES)
        q_ref[0, :, sl] = _rope_block_dup(q[:, sl], tq_c, tq_s, B_ROPE // 2).astype(BF16)
        k_ref[0, :, sl] = (k[:, sl] + k_rope).astype(BF16)
    vrow = lax.broadcasted_iota(jnp.int32, (wuvt_ref.shape[0], 1), 0) % LANES
    vt_ref[0] = (_dot_nt(wuvt_ref[...], ckv) + jnp.where(vrow == B_V, 1.0, 0.0)).astype(BF16)


def _proj_b(xs, modl, g, wd, gq, gkv, wuq, wuk, wuvt, tab):
    b, s_tot, d = xs.shape
    nt = s_tot // TM
    out = lambda n: jax.ShapeDtypeStruct((b, s_tot, n), BF16)
    ospec = lambda n: pl.BlockSpec((1, TM, n), lambda i, t: (i, t, 0))
    nq, nv = wuq.shape[1], wuvt.shape[0]
    return pl.pallas_call(
        functools.partial(_proj_b_kernel, q_lora=gq.shape[1], kv_lora=gkv.shape[1]),
        grid=(b, nt),
        in_specs=[pl.BlockSpec((1, TM, d), lambda i, t: (i, t, 0)),
                  pl.BlockSpec((1, 1, 6, d), lambda i, t: (i, jnp.minimum(t, 1), 0, 0)),
                  _resident(g.shape), _resident(wd.shape), _resident(gq.shape), _resident(gkv.shape),
                  _resident(wuq.shape), _resident(wuk.shape), _resident(wuvt.shape),
                  pl.BlockSpec((4, TM, LANES), lambda i, t: (0, t, 0))],
        out_specs=[ospec(nq), ospec(nq), pl.BlockSpec((1, nv, TM), lambda i, t: (i, 0, t))],
        out_shape=[out(nq), out(nq), jax.ShapeDtypeStruct((b, nv, s_tot), BF16)],
        compiler_params=_params(2),
        name="proj_mla",
    )(xs, modl, g, wd, gq, gkv, wuq, wuk, wuvt, tab)


def _attn_a_kernel(sink_ref, q_ref, kc_ref, kp_ref, kt_ref, kn_ref, vc_ref, vp_ref, vt_ref, vn_ref, o_ref,
                   *, ctx_tiles, n_tiles, ctx_out):
    t = pl.program_id(1)
    assert TM == 2 * WINDOW
    edge = WINDOW
    n_heads = q_ref.shape[2] // HEAD_DIM
    group = n_heads // A_KV_HEADS
    lane = lax.broadcasted_iota(jnp.int32, (1, LANES), 1)
    lo = lane < HEAD_DIM
    zero = jnp.zeros((), BF16)

    key = lax.broadcasted_iota(jnp.int32, (edge, edge), 0)
    qry = lax.broadcasted_iota(jnp.int32, (edge, edge), 1)
    tri = lambda valid: jnp.where(valid, 0.0, NEG).astype(F32)
    ALL, NONE = "all", "none"
    ctx_layout = [(ALL, ALL)] * (kc_ref.shape[1] // edge)
    win_layout = [(tri((qry <= key) & (t > ctx_tiles)), NONE),
                  (ALL, tri(qry <= key)), (tri(qry >= key), ALL),
                  (NONE, tri((qry >= key) & (t < n_tiles - 1)))]

    def attend(layout, k_of, v_of):
        def score(h):
            qp = q_ref[0, :, (h // 2) * LANES:(h // 2 + 1) * LANES]
            qe = jnp.where(lo, qp, zero) if h % 2 == 0 else jnp.where(lo, zero, qp)
            s = _dot_nt(k_of(h // group), qe)
            sink = sink_ref[h] * LOG2E
            halves = []
            for half in range(2):
                blocks, run = [], None
                for rb, kinds in enumerate(layout):
                    kind = kinds[half]
                    if isinstance(kind, str) and kind == NONE:
                        blocks.append(None)
                        continue
                    blk = s[rb * edge:(rb + 1) * edge, half * edge:(half + 1) * edge]
                    if not isinstance(kind, str):
                        blk = blk + kind
                    blocks.append(blk)
                    for i in range(edge // MAX_FOLD):
                        part = blk[i * MAX_FOLD:(i + 1) * MAX_FOLD]
                        run = part if run is None else jnp.maximum(run, part)
                halves.append((blocks, jnp.maximum(jnp.max(run, axis=0, keepdims=True), sink)))
            return halves, sink

        def probs(st):
            halves, sink = st
            rows = []
            for rb in range(len(layout)):
                parts = [jnp.zeros((edge, edge), BF16) if blocks[rb] is None
                         else jnp.exp2(blocks[rb] - mx).astype(BF16) for blocks, mx in halves]
                rows.append(jnp.concatenate(parts, axis=1))
            p_sink = jnp.concatenate([jnp.exp2(sink - mx) for _, mx in halves], axis=1)
            return jnp.concatenate(rows, axis=0), p_sink

        def values(h, pr):
            p, p_sink = pr
            acc = _dot(v_of(h // group), p)
            return acc[:HEAD_DIM] * (1.0 / (acc[HEAD_DIM:HEAD_DIM + 1] + p_sink))

        scored, weighted, outs = {}, {}, {}
        for step in range(n_heads + 2 * GQA_LAG):
            h = step - 2 * GQA_LAG
            if 0 <= h < n_heads:
                outs[h] = values(h, weighted.pop(h))
                if h % 2 == 1:
                    pair_t = jnp.concatenate([outs.pop(h - 1), outs.pop(h)], axis=0)
                    o_ref[0, :, (h // 2) * LANES:(h // 2 + 1) * LANES] = pair_t.T.astype(BF16)
            if 0 <= step - GQA_LAG < n_heads:
                weighted[step - GQA_LAG] = probs(scored.pop(step - GQA_LAG))
            if step < n_heads:
                scored[step] = score(step)

    def with_kv(body, k_refs, v_refs):
        ks, vs = [], []
        for j in range(A_KV_HEADS):
            sl = slice(j * LANES, (j + 1) * LANES)
            ks.append(jnp.concatenate([r[0, :, sl] for r in k_refs], axis=0))
            vs.append(jnp.concatenate([r[0, sl, :] for r in v_refs], axis=1))
        body(ks.__getitem__, vs.__getitem__)

    @pl.when(t < ctx_tiles)
    def _():
        if ctx_out:
            with_kv(functools.partial(attend, ctx_layout), [kc_ref], [vc_ref])
        else:
            o_ref[...] = jnp.zeros(o_ref.shape, o_ref.dtype)

    @pl.when(t >= ctx_tiles)
    def _():
        with_kv(functools.partial(attend, ctx_layout + win_layout),
                [kc_ref, kp_ref, kt_ref, kn_ref], [vc_ref, vp_ref, vt_ref, vn_ref])


def _attn_a(q, k, vt, sink, seq, with_ctx):
    b, s_tot, qw = q.shape
    kw, nv = k.shape[2], vt.shape[1]
    ctx_tiles = (s_tot - seq) // TM
    n_tiles = s_tot // TM
    hb = TM // WINDOW
    n_half = s_tot // WINDOW
    prev = lambda t: jnp.maximum(t * hb - 1, 0)
    nxt = lambda t: jnp.minimum((t + 1) * hb, n_half - 1)
    k_specs = [pl.BlockSpec((1, TM, kw), lambda i, t: (i, 0, 0)),
               pl.BlockSpec((1, WINDOW, kw), lambda i, t: (i, prev(t), 0)),
               pl.BlockSpec((1, TM, kw), lambda i, t: (i, t, 0)),
               pl.BlockSpec((1, WINDOW, kw), lambda i, t: (i, nxt(t), 0))]
    v_specs = [pl.BlockSpec((1, nv, TM), lambda i, t: (i, 0, 0)),
               pl.BlockSpec((1, nv, WINDOW), lambda i, t: (i, 0, prev(t))),
               pl.BlockSpec((1, nv, TM), lambda i, t: (i, 0, t)),
               pl.BlockSpec((1, nv, WINDOW), lambda i, t: (i, 0, nxt(t)))]
    return pl.pallas_call(
        functools.partial(_attn_a_kernel, ctx_tiles=ctx_tiles, n_tiles=n_tiles, ctx_out=with_ctx),
        grid=(b, n_tiles),
        in_specs=[pl.BlockSpec(memory_space=pltpu.SMEM),
                  pl.BlockSpec((1, TM, qw), lambda i, t: (i, t, 0))] + k_specs + v_specs,
        out_specs=pl.BlockSpec((1, TM, qw), lambda i, t: (i, t, 0)),
        out_shape=jax.ShapeDtypeStruct((b, s_tot, qw), BF16),
        compiler_params=_params(2),
        name="attn_gqa",
    )(sink, q, k, k, k, k, vt, vt, vt, vt)


def _attn_b_kernel(q_ref, qn_ref, k_ref, vt_ref, o_ref, s_ref, run_ref, *, ctx_len, seq, n_tiles, ctx_out):
    t = pl.program_id(2)
    ctx_tiles = ctx_len // TM
    n_pairs = q_ref.shape[2] // (2 * LANES)
    slot = lambda h: h % s_ref.shape[0]
    assert (2 * n_pairs) % s_ref.shape[0] == 0

    def score_chunk(h, q_src, k0, tk, run):
        hs = slice(h * LANES, (h + 1) * LANES)
        s = _dot_nt(k_ref[0, k0:k0 + tk, hs], q_src[0, :, hs])
        s_ref[slot(h), k0:k0 + tk, :] = s
        for i in range(tk // MAX_FOLD):
            run = jnp.maximum(run, s[i * MAX_FOLD:(i + 1) * MAX_FOLD])
        return run

    def value_chunk(h, k0, tk, mx, acc):
        p = jnp.exp2(s_ref[slot(h), k0:k0 + tk, :] - mx).astype(BF16)
        c = _dot(vt_ref[0, h * LANES:(h + 1) * LANES, k0:k0 + tk], p)
        return c if acc is None else acc + c

    def attend(chunks, first_scored, score_next):
        init = jnp.full((MAX_FOLD, TM), NEG, F32)
        run = [run_ref[0], run_ref[1]] if first_scored else None
        for pr in range(1 if first_scored else 0, n_pairs + 1):
            mx = None if run is None else [jnp.max(r, axis=0, keepdims=True) for r in run]
            run, acc = [init, init], [None, None]
            for k0, tk in chunks:
                for e in range(2):
                    if pr < n_pairs:
                        run[e] = score_chunk(2 * pr + e, q_ref, k0, tk, run[e])
                    elif score_next:
                        run[e] = score_chunk(e, qn_ref, k0, tk, run[e])
                    if pr > 0:
                        acc[e] = value_chunk(2 * (pr - 1) + e, k0, tk, mx[e], acc[e])
            if pr > 0:
                out_t = jnp.concatenate([a[:B_V] * (1.0 / a[B_V:B_V + 1]) for a in acc], axis=0)
                o_ref[0, :, (pr - 1) * LANES:pr * LANES] = out_t.T.astype(BF16)
        if score_next:
            run_ref[0], run_ref[1] = run

    ctx_chunk = [(0, ctx_len)]
    all_chunks = ctx_chunk + [(ctx_len + i * TK_MLA, TK_MLA) for i in range(seq // TK_MLA)]

    @pl.when(t < ctx_tiles)
    def _():
        if ctx_out:
            attend(ctx_chunk, False, False)
        else:
            o_ref[...] = jnp.zeros(o_ref.shape, o_ref.dtype)

    pl.when(t == ctx_tiles)(lambda: attend(all_chunks, False, ctx_tiles < n_tiles - 1))
    pl.when((t > ctx_tiles) & (t < n_tiles - 1))(lambda: attend(all_chunks, True, True))
    pl.when((t > ctx_tiles) & (t == n_tiles - 1))(lambda: attend(all_chunks, True, False))


def _attn_b(q, k, vt, seq, with_ctx):
    b, s_tot, qw = q.shape
    ctx_len = s_tot - seq
    n_tiles = s_tot // TM
    heads = 2 * MLA_PAIRS
    wide = heads * LANES
    pair = 2 * LANES
    return pl.pallas_call(
        functools.partial(_attn_b_kernel, ctx_len=ctx_len, seq=seq, n_tiles=n_tiles, ctx_out=with_ctx),
        grid=(b, qw // wide, n_tiles),
        in_specs=[pl.BlockSpec((1, TM, wide), lambda i, h, t: (i, t, h)),
                  pl.BlockSpec((1, TM, pair), lambda i, h, t: (i, jnp.minimum(t + 1, n_tiles - 1), h * MLA_PAIRS)),
                  pl.BlockSpec((1, s_tot, wide), lambda i, h, t: (i, 0, h), pipeline_mode=pl.Buffered(1)),
                  pl.BlockSpec((1, wide, s_tot), lambda i, h, t: (i, h, 0), pipeline_mode=pl.Buffered(1))],
        out_specs=pl.BlockSpec((1, TM, heads * B_V), lambda i, h, t: (i, t, h)),
        out_shape=jax.ShapeDtypeStruct((b, s_tot, B_HEADS * B_V), BF16),
        scratch_shapes=[pltpu.VMEM((min(heads, 4), s_tot, TM), F32), pltpu.VMEM((2, MAX_FOLD, TM), F32)],
        compiler_params=pltpu.CompilerParams(dimension_semantics=("parallel", "parallel", "arbitrary"),
                                             vmem_limit_bytes=VMEM_LIMIT),
        name="attn_mla",
    )(q, q, k, vt)


def _channel_kernel(*refs, rows, seg_tiles, final):
    (xp_ref, xt_ref, xn_ref, op_ref, ot_ref, on_ref, mod_ref, g_ref, wo_ref, wa_ref, wv_ref,
     cw_ref, cb_ref, wout_ref, gf_ref), y_ref = refs[:15], refs[-1]
    t = pl.program_id(1)
    ext = rows + 2 * HALO
    mid = slice(HALO, HALO + rows)
    m = mod_ref[0, 0]
    x_ext = jnp.concatenate([xp_ref[0], xt_ref[0], xn_ref[0]], axis=0)
    o_ext = jnp.concatenate([op_ref[0], ot_ref[0], on_ref[0]], axis=0)
    x1 = x_ext + m[2:3] * _dot(o_ext, wo_ref[...])
    h2 = _norm_mod(x1, g_ref[...], m[3:4], m[4:5])
    row = lax.broadcasted_iota(jnp.int32, (ext, 1), 0)
    keep = ((row >= HALO) | (t > 0)) & ((row < HALO + rows) | (t < seg_tiles - 1))
    h2 = jnp.where(keep, h2, 0.0).astype(BF16)
    h2_mid = h2[mid]
    cw = cw_ref[...]
    cb = cb_ref[...]
    d_ff = wa_ref.shape[1]
    y = None
    for c0 in range(0, d_ff, FF_CHUNK):
        cs = slice(c0, min(c0 + FF_CHUNK, d_ff))
        a = _dot(h2, wa_ref[:, cs])
        a_prev = pltpu.roll(a, 1, 0)[mid]
        a_next = pltpu.roll(a, ext - 1, 0)[mid]
        conv = a_prev * cw[0:1, cs] + a[mid] * cw[1:2, cs] + a_next * cw[2:3, cs] + cb[:, cs]
        gate = conv * (1.0 / (1.0 + jnp.exp(-conv)))
        hid = (gate * _dot(h2_mid, wv_ref[:, cs])).astype(BF16)
        part = _dot(hid, wout_ref[cs, :])
        y = part if y is None else y + part
    x2 = x1[mid] + m[5:6] * y
    y_ref[0] = _rms(x2, gf_ref[...]) if final else x2


def _rows_at(n_rows, width, offset):
    return pl.BlockSpec((pl.Element(1), pl.Element(n_rows), pl.Element(width)),
                        lambda i, t: (i, pl.multiple_of(offset(t), HALO), 0))


def _channel(xs, o, modl, g, wo, wa, wv, cw, cb, wout, gf, seq, final):
    b, s_tot, d = xs.shape
    ctx_len = s_tot - seq
    weights = (g, wo, wa, wv, cw, cb, wout, gf)
    w_specs = [_resident(w.shape) for w in weights]

    def call(rows, base, seg_rows, mod_row, out_spec, out_rows, alias):
        seg_tiles = seg_rows // rows
        prev = lambda t: jnp.maximum(base + t * rows - HALO, 0)
        cur = lambda t: base + t * rows
        nxt = lambda t: jnp.minimum(base + (t + 1) * rows, base + seg_rows - HALO)
        tiles = lambda w: [_rows_at(HALO, w, prev), _rows_at(rows, w, cur), _rows_at(HALO, w, nxt)]
        in_specs = (tiles(d) + tiles(o.shape[2])
                    + [pl.BlockSpec((1, 1, 6, d), lambda i, t: (i, mod_row, 0, 0))] + w_specs)
        args = (xs, xs, xs, o, o, o, modl) + weights
        if alias is not None:
            in_specs.append(pl.BlockSpec(memory_space=pl.ANY))
            args += (alias,)
        return pl.pallas_call(
            functools.partial(_channel_kernel, rows=rows, seg_tiles=seg_tiles, final=final),
            grid=(b, seg_tiles),
            in_specs=in_specs,
            out_specs=out_spec,
            out_shape=jax.ShapeDtypeStruct((b, out_rows, d), F32),
            input_output_aliases={} if alias is None else {len(args) - 1: 0},
            compiler_params=_params(2),
            name="channel",
        )(*args)

    if final:
        return call(TM_CHANNEL, ctx_len, seq, 1, pl.BlockSpec((1, TM_CHANNEL, d), lambda i, t: (i, t, 0)), seq, None)
    y = call(TM_CHANNEL, ctx_len, seq, 1, _rows_at(TM_CHANNEL, d, lambda t: ctx_len + t * TM_CHANNEL), s_tot, None)
    return call(TM, 0, ctx_len, 0, pl.BlockSpec((1, TM, d), lambda i, t: (i, t, 0)), s_tot, y)


def _axial_angles(rows, rot_dim):
    row = jnp.repeat(jnp.arange(rows), GRID_W).astype(F32)
    col = jnp.tile(jnp.arange(GRID_W), rows).astype(F32)
    n_freq = rot_dim // 4
    inv = ROPE_BASE ** (-jnp.arange(n_freq, dtype=F32) / n_freq)
    return jnp.concatenate([row[:, None] * inv, col[:, None] * inv], axis=-1)


def _with_ctx_rows(cos, sin, ctx_cos, ctx_len, q_scale):
    cos = jnp.concatenate([jnp.broadcast_to(ctx_cos, (ctx_len, LANES)), cos], axis=0)
    sin = jnp.concatenate([jnp.zeros((ctx_len, LANES), F32), sin], axis=0)
    return jnp.stack([cos * q_scale, sin * q_scale, cos, sin])


def _tables_a(seq, ctx_len):
    ang = _axial_angles(seq // GRID_W, HEAD_DIM)
    cos, sin = jnp.cos(ang), jnp.sin(ang)
    reps = LANES // HEAD_DIM
    cos_l = jnp.tile(cos, (1, 2 * reps))
    sin_l = jnp.tile(jnp.concatenate([-sin, sin], axis=1), (1, reps))
    return _with_ctx_rows(cos_l, sin_l, jnp.ones((1, LANES), F32), ctx_len, LOG2E * HEAD_DIM ** -0.5)


def _tables_b(seq, ctx_len):
    ang = _axial_angles(seq // GRID_W, B_ROPE)
    cos, sin = jnp.cos(ang), jnp.sin(ang)
    copy = LANES - B_NOPE - B_ROPE
    cos_l = jnp.concatenate([jnp.ones((seq, B_NOPE), F32), cos, cos, jnp.zeros((seq, copy), F32)], axis=1)
    sin_l = jnp.concatenate([jnp.zeros((seq, B_NOPE), F32), -sin, sin, jnp.zeros((seq, copy), F32)], axis=1)
    ctx_cos = jnp.concatenate([jnp.ones((1, B_NOPE + B_ROPE), F32), jnp.zeros((1, copy), F32)], axis=1)
    return _with_ctx_rows(cos_l, sin_l, ctx_cos, ctx_len, LOG2E * (B_NOPE + B_ROPE) ** -0.5)


def _deinterleave(n):
    return jnp.concatenate([jnp.arange(0, n, 2), jnp.arange(1, n, 2)])


def _weights_a(wqkv):
    d = wqkv.shape[0]
    kw = A_KV_HEADS * HEAD_DIM
    qw = wqkv.shape[1] - 2 * kw
    perm = _deinterleave(HEAD_DIM)
    wq = wqkv[:, :qw].reshape(d, -1, HEAD_DIM)[:, :, perm].reshape(d, qw)
    wk = wqkv[:, qw:qw + kw].reshape(d, A_KV_HEADS, HEAD_DIM)[:, :, perm]
    wkk = jnp.concatenate([wk, wk], axis=2).reshape(d, 2 * kw)
    wv = wqkv[:, qw + kw:].reshape(d, A_KV_HEADS, HEAD_DIM)
    wv = jnp.concatenate([wv, jnp.zeros((d, A_KV_HEADS, LANES - HEAD_DIM), F32)], axis=2)
    wvt = wv.reshape(d, A_KV_HEADS * LANES).T
    return jnp.concatenate([wq, wkk], axis=1).astype(BF16), wvt.astype(BF16), qw, 2 * kw


def _weights_b(wdown, wuq, wuk, wuv, q_lora, kv_lora):
    d = wdown.shape[0]
    perm = _deinterleave(B_ROPE)
    assert LANES - B_NOPE - B_ROPE == B_ROPE
    w_rope = wdown[:, q_lora + kv_lora:][:, perm]
    w_rope = jnp.concatenate([jnp.zeros((d, B_NOPE), F32), w_rope, w_rope], axis=1)
    wd = jnp.concatenate([wdown[:, :q_lora + kv_lora], w_rope], axis=1)
    uq = wuq.reshape(q_lora, B_HEADS, B_NOPE + B_ROPE)
    uq_rope = uq[:, :, B_NOPE:][:, :, perm]
    uq = jnp.concatenate([uq[:, :, :B_NOPE], uq_rope, uq_rope], axis=2)
    uk = wuk.reshape(kv_lora, B_HEADS, B_NOPE)
    uk = jnp.concatenate([uk, jnp.zeros((kv_lora, B_HEADS, LANES - B_NOPE), F32)], axis=2)
    uv = wuv.reshape(kv_lora, B_HEADS, B_V)
    uv = jnp.concatenate([uv, jnp.zeros((kv_lora, B_HEADS, LANES - B_V), F32)], axis=2)
    flat = lambda w: w.reshape(w.shape[0], B_HEADS * LANES).astype(BF16)
    return wd.astype(BF16), flat(uq), flat(uk), flat(uv).T


def kernel(x, c, ctx, c_ctx, mod_w, mod_b, norm1_g, norm2_g, a_wqkv, a_wo, a_sink, b_wdown, b_qnorm_g, b_wuq,
           b_kvnorm_g, b_wuk, b_wuv, b_wo, f_win, f_conv_w, f_conv_b, f_wout, final_g):
    bsz, seq, d = x.shape
    ctx_len = ctx.shape[1]
    depth = mod_w.shape[0]
    d_ff = f_wout.shape[1]
    assert ctx_len == TM and seq % TK_MLA == 0 and seq % TM_CHANNEL == 0 and seq % GRID_W == 0

    cond_rows = -(-(bsz + 1) // SUBLANES) * SUBLANES
    cond = jnp.concatenate([c, c_ctx[None], jnp.zeros((cond_rows - bsz - 1, d), F32)], axis=0)
    mod = _modulation(cond, mod_w, mod_b)

    tab_a = _tables_a(seq, ctx_len)
    tab_b = _tables_b(seq, ctx_len)
    xs = None
    row = lambda v: v.reshape(1, -1)

    for i in range(depth):
        last = i == depth - 1
        lat = mod[i, :bsz].reshape(bsz, 6, d)
        cmod = jnp.broadcast_to(mod[i, bsz].reshape(1, 6, d), (bsz, 6, d))
        modl = jnp.stack([cmod, lat], axis=1)
        j = i // 2
        if i % 2 == 0:
            w, wvt, qw, kw = _weights_a(a_wqkv[j])
            if i == 0:
                xs, q, k, vt = _proj_a((ctx, x), modl, row(norm1_g[i]), w, wvt, tab_a, qw, kw)
            else:
                q, k, vt = _proj_a(xs, modl, row(norm1_g[i]), w, wvt, tab_a, qw, kw)
            o = _attn_a(q, k, vt, a_sink[j], seq, not last)
            wo = a_wo[j]
        else:
            q_lora, kv_lora = b_qnorm_g.shape[1], b_kvnorm_g.shape[1]
            wd, wuq, wuk, wuvt = _weights_b(b_wdown[j], b_wuq[j], b_wuk[j], b_wuv[j], q_lora, kv_lora)
            q, k, vt = _proj_b(xs, modl, row(norm1_g[i]), wd, row(b_qnorm_g[j]), row(b_kvnorm_g[j]),
                               wuq, wuk, wuvt, tab_b)
            o = _attn_b(q, k, vt, seq, not last)
            wo = b_wo[j]
        xs = _channel(xs, o, modl, row(norm2_g[i]), wo.astype(BF16), f_win[i][:, :d_ff].astype(BF16),
                      f_win[i][:, d_ff:].astype(BF16), f_conv_w[i], row(f_conv_b[i]), f_wout[i].astype(BF16),
                      row(final_g), seq, last)
    return xs
```

```python
import functools

import jax
import jax.numpy as jnp
from jax import lax
from jax.experimental import pallas as pl
from jax.experimental.pallas import tpu as pltpu

GRID_W = 64
HEAD_DIM = 64
A_KV_HEADS = 4
WINDOW = 128
B_HEADS = 16
B_NOPE = 64
B_ROPE = 32
B_V = 64
ROPE_BASE = 10000.0
EPS = 1e-6
NEG = -1e30
LOG2E = 1.4426950408889634

LANES = 128
SUBLANES = 8
V7X_VMEM_BYTES = 64 * 1024 * 1024
VMEM_LIMIT = V7X_VMEM_BYTES * 7 // 8
TM = 256
HALO = 16
TK_MLA = 1024
MLA_PAIRS = 4
GQA_LAG = 3
MAX_FOLD = 64
TM_CHANNEL = 512
FF_CHUNK = 768

F32 = jnp.float32
BF16 = jnp.bfloat16
_NT = (((1,), (1,)), ((), ()))


def _dot(a, b):
    return jnp.dot(a, b, preferred_element_type=F32)


def _dot_nt(a, b):
    return lax.dot_general(a, b, _NT, preferred_element_type=F32)


def _rms(xf, g):
    ms = jnp.mean(xf * xf, axis=-1, keepdims=True)
    return xf * lax.rsqrt(ms + EPS) * g


def _norm_mod(xf, g, shift, scale):
    return _rms(xf, g) * (1.0 + scale) + shift


def _params(n_axes):
    return pltpu.CompilerParams(dimension_semantics=("parallel",) * n_axes,
                                vmem_limit_bytes=VMEM_LIMIT)


def _resident(shape):
    nd = len(shape)
    return pl.BlockSpec(shape, lambda *_: (0,) * nd, pipeline_mode=pl.Buffered(1))


def _mod_kernel(c_ref, w_ref, b_ref, o_ref):
    c = c_ref[...]
    silu = c * (1.0 / (1.0 + jnp.exp(-c)))
    o_ref[0] = _dot(silu.astype(BF16), w_ref[0].astype(BF16)) + b_ref[0]


def _modulation(cond, mod_w, mod_b):
    depth, d, d6 = mod_w.shape
    rows = cond.shape[0]
    return pl.pallas_call(
        _mod_kernel,
        grid=(depth, d6 // d),
        in_specs=[pl.BlockSpec((rows, d), lambda i, j: (0, 0)),
                  pl.BlockSpec((1, d, d), lambda i, j: (i, 0, j)),
                  pl.BlockSpec((1, 1, d), lambda i, j: (i, 0, j))],
        out_specs=pl.BlockSpec((1, rows, d), lambda i, j: (i, 0, j)),
        out_shape=jax.ShapeDtypeStruct((depth, rows, d6), F32),
        compiler_params=_params(2),
        name="modulation",
    )(cond, mod_w, mod_b.reshape(depth, 1, d6))


def _rope_block(xb, cos, sin_signed, first_half, half):
    swapped = jnp.where(first_half, pltpu.roll(xb, LANES - half, 1), pltpu.roll(xb, half, 1))
    return xb * cos + swapped * sin_signed


def _rope_block_dup(xb, cos, sin_signed, half):
    return xb * cos + pltpu.roll(xb, LANES - half, 1) * sin_signed


def _proj_a_kernel(x_ref, mod_ref, g_ref, w_ref, wvt_ref, tab_ref, q_ref, k_ref, vt_ref, *, qw, kw):
    m = mod_ref[0, 0]
    h = _norm_mod(x_ref[0], g_ref[...], m[0:1], m[1:2]).astype(BF16)
    qk = _dot(h, w_ref[...])
    lane = lax.broadcasted_iota(jnp.int32, (TM, LANES), 1)
    first = (lane % HEAD_DIM) < (HEAD_DIM // 2)
    cq, sq, ck, sk = tab_ref[0], tab_ref[1], tab_ref[2], tab_ref[3]
    for j in range(qw // LANES):
        blk = qk[:, j * LANES:(j + 1) * LANES]
        q_ref[0, :, j * LANES:(j + 1) * LANES] = _rope_block(blk, cq, sq, first, HEAD_DIM // 2).astype(BF16)
    for j in range(kw // LANES):
        blk = qk[:, qw + j * LANES:qw + (j + 1) * LANES]
        k_ref[0, :, j * LANES:(j + 1) * LANES] = _rope_block_dup(blk, ck, sk, HEAD_DIM // 2).astype(BF16)
    vrow = lax.broadcasted_iota(jnp.int32, (wvt_ref.shape[0], 1), 0) % LANES
    vt_ref[0] = (_dot_nt(wvt_ref[...], h) + jnp.where(vrow == HEAD_DIM, 1.0, 0.0)).astype(BF16)


def _proj_a_first_kernel(ctx_ref, lat_ref, mod_ref, g_ref, w_ref, wvt_ref, tab_ref, xs_ref, q_ref, k_ref, vt_ref,
                         *, ctx_tiles, **kw):
    xs_ref[0] = jnp.where(pl.program_id(1) < ctx_tiles, ctx_ref[0], lat_ref[0])
    _proj_a_kernel(xs_ref, mod_ref, g_ref, w_ref, wvt_ref, tab_ref, q_ref, k_ref, vt_ref, **kw)


def _proj_a(xs, modl, g, w, wvt, tab, qw, kw):
    first = isinstance(xs, tuple)
    if first:
        ctx, lat = xs
        b, ctx_len, d = ctx.shape
        ctx_tiles = ctx_len // TM
        s_tot = ctx_len + lat.shape[1]
        x_args = (ctx, lat)
        x_specs = [pl.BlockSpec((1, TM, d), lambda i, t: (i, jnp.minimum(t, ctx_tiles - 1), 0)),
                   pl.BlockSpec((1, TM, d), lambda i, t: (i, jnp.maximum(t - ctx_tiles, 0), 0))]
        body = functools.partial(_proj_a_first_kernel, ctx_tiles=ctx_tiles, qw=qw, kw=kw)
    else:
        b, s_tot, d = xs.shape
        x_args = (xs,)
        x_specs = [pl.BlockSpec((1, TM, d), lambda i, t: (i, t, 0))]
        body = functools.partial(_proj_a_kernel, qw=qw, kw=kw)
    nt = s_tot // TM
    nv = wvt.shape[0]
    out = lambda n: jax.ShapeDtypeStruct((b, s_tot, n), BF16)
    ospec = lambda n: pl.BlockSpec((1, TM, n), lambda i, t: (i, t, 0))
    out_specs = [ospec(qw), ospec(kw), pl.BlockSpec((1, nv, TM), lambda i, t: (i, 0, t))]
    out_shape = [out(qw), out(kw), jax.ShapeDtypeStruct((b, nv, s_tot), BF16)]
    if first:
        out_specs.insert(0, ospec(d))
        out_shape.insert(0, jax.ShapeDtypeStruct((b, s_tot, d), F32))
    return pl.pallas_call(
        body,
        grid=(b, nt),
        in_specs=x_specs + [pl.BlockSpec((1, 1, 6, d), lambda i, t: (i, jnp.minimum(t, 1), 0, 0)),
                            _resident(g.shape), _resident(w.shape), _resident(wvt.shape),
                            pl.BlockSpec((4, TM, LANES), lambda i, t: (0, t, 0))],
        out_specs=out_specs,
        out_shape=out_shape,
        compiler_params=_params(2),
        name="proj_gqa",
    )(*x_args, modl, g, w, wvt, tab)


def _proj_b_kernel(x_ref, mod_ref, g_ref, wd_ref, gq_ref, gkv_ref, wuq_ref, wuk_ref, wuvt_ref, tab_ref,
                   q_ref, k_ref, vt_ref, *, q_lora, kv_lora):
    m = mod_ref[0, 0]
    h = _norm_mod(x_ref[0], g_ref[...], m[0:1], m[1:2]).astype(BF16)
    d = _dot(h, wd_ref[...])
    cq = _rms(d[:, :q_lora], gq_ref[...]).astype(BF16)
    ckv = _rms(d[:, q_lora:q_lora + kv_lora], gkv_ref[...]).astype(BF16)
    tq_c, tq_s, tk_c, tk_s = tab_ref[0], tab_ref[1], tab_ref[2], tab_ref[3]
    k_rope = _rope_block_dup(d[:, q_lora + kv_lora:], tk_c, tk_s, B_ROPE // 2)
    q = _dot(cq, wuq_ref[...])
    k = _dot(ckv, wuk_ref[...])
    for j in range(B_HEADS):
        sl = slice(j * LANES, (j + 1) * LANES)
        q_ref[0, :, sl] = _rope_block_dup(q[:, sl], tq_c, tq_s, B_ROPE // 2).astype(BF16)
        k_ref[0, :, sl] = (k[:, sl] + k_rope).astype(BF16)
    vrow = lax.broadcasted_iota(jnp.int32, (wuvt_ref.shape[0], 1), 0) % LANES
    vt_ref[0] = (_dot_nt(wuvt_ref[...], ckv) + jnp.where(vrow == B_V, 1.0, 0.0)).astype(BF16)


def _proj_b(xs, modl, g, wd, gq, gkv, wuq, wuk, wuvt, tab):
    b, s_tot, d = xs.shape
    nt = s_tot // TM
    out = lambda n: jax.ShapeDtypeStruct((b, s_tot, n), BF16)
    ospec = lambda n: pl.BlockSpec((1, TM, n), lambda i, t: (i, t, 0))
    nq, nv = wuq.shape[1], wuvt.shape[0]
    return pl.pallas_call(
        functools.partial(_proj_b_kernel, q_lora=gq.shape[1], kv_lora=gkv.shape[1]),
        grid=(b, nt),
        in_specs=[pl.BlockSpec((1, TM, d), lambda i, t: (i, t, 0)),
                  pl.BlockSpec((1, 1, 6, d), lambda i, t: (i, jnp.minimum(t, 1), 0, 0)),
                  _resident(g.shape), _resident(wd.shape), _resident(gq.shape), _resident(gkv.shape),
                  _resident(wuq.shape), _resident(wuk.shape), _resident(wuvt.shape),
                  pl.BlockSpec((4, TM, LANES), lambda i, t: (0, t, 0))],
        out_specs=[ospec(nq), ospec(nq), pl.BlockSpec((1, nv, TM), lambda i, t: (i, 0, t))],
        out_shape=[out(nq), out(nq), jax.ShapeDtypeStruct((b, nv, s_tot), BF16)],
        compiler_params=_params(2),
        name="proj_mla",
    )(xs, modl, g, wd, gq, gkv, wuq, wuk, wuvt, tab)


def _attn_a_kernel(sink_ref, q_ref, kc_ref, kp_ref, kt_ref, kn_ref, vc_ref, vp_ref, vt_ref, vn_ref, o_ref,
                   *, ctx_tiles, n_tiles, ctx_out):
    t = pl.program_id(1)
    assert TM == 2 * WINDOW
    edge = WINDOW
    n_heads = q_ref.shape[2] // HEAD_DIM
    group = n_heads // A_KV_HEADS
    lane = lax.broadcasted_iota(jnp.int32, (1, LANES), 1)
    lo = lane < HEAD_DIM
    zero = jnp.zeros((), BF16)

    key = lax.broadcasted_iota(jnp.int32, (edge, edge), 0)
    qry = lax.broadcasted_iota(jnp.int32, (edge, edge), 1)
    tri = lambda valid: jnp.where(valid, 0.0, NEG).astype(F32)
    ALL, NONE = "all", "none"
    ctx_layout = [(ALL, ALL)] * (kc_ref.shape[1] // edge)
    win_layout = [(tri((qry <= key) & (t > ctx_tiles)), NONE),
                  (ALL, tri(qry <= key)), (tri(qry >= key), ALL),
                  (NONE, tri((qry >= key) & (t < n_tiles - 1)))]

    def attend(layout, k_of, v_of):
        def score(h):
            qp = q_ref[0, :, (h // 2) * LANES:(h // 2 + 1) * LANES]
            qe = jnp.where(lo, qp, zero) if h % 2 == 0 else jnp.where(lo, zero, qp)
            s = _dot_nt(k_of(h // group), qe)
            sink = sink_ref[h] * LOG2E
            halves = []
            for half in range(2):
                blocks, run = [], None
                for rb, kinds in enumerate(layout):
                    kind = kinds[half]
                    if isinstance(kind, str) and kind == NONE:
                        blocks.append(None)
                        continue
                    blk = s[rb * edge:(rb + 1) * edge, half * edge:(half + 1) * edge]
                    if not isinstance(kind, str):
                        blk = blk + kind
                    blocks.append(blk)
                    for i in range(edge // MAX_FOLD):
                        part = blk[i * MAX_FOLD:(i + 1) * MAX_FOLD]
                        run = part if run is None else jnp.maximum(run, part)
                halves.append((blocks, jnp.maximum(jnp.max(run, axis=0, keepdims=True), sink)))
            return halves, sink

        def probs(st):
            halves, sink = st
            rows = []
            for rb in range(len(layout)):
                parts = [jnp.zeros((edge, edge), BF16) if blocks[rb] is None
                         else jnp.exp2(blocks[rb] - mx).astype(BF16) for blocks, mx in halves]
                rows.append(jnp.concatenate(parts, axis=1))
            p_sink = jnp.concatenate([jnp.exp2(sink - mx) for _, mx in halves], axis=1)
            return jnp.concatenate(rows, axis=0), p_sink

        def values(h, pr):
            p, p_sink = pr
            acc = _dot(v_of(h // group), p)
            return acc[:HEAD_DIM] * (1.0 / (acc[HEAD_DIM:HEAD_DIM + 1] + p_sink))

        scored, weighted, outs = {}, {}, {}
        for step in range(n_heads + 2 * GQA_LAG):
            h = step - 2 * GQA_LAG
            if 0 <= h < n_heads:
                outs[h] = values(h, weighted.pop(h))
                if h % 2 == 1:
                    pair_t = jnp.concatenate([outs.pop(h - 1), outs.pop(h)], axis=0)
                    o_ref[0, :, (h // 2) * LANES:(h // 2 + 1) * LANES] = pair_t.T.astype(BF16)
            if 0 <= step - GQA_LAG < n_heads:
                weighted[step - GQA_LAG] = probs(scored.pop(step - GQA_LAG))
            if step < n_heads:
                scored[step] = score(step)

    def with_kv(body, k_refs, v_refs):
        ks, vs = [], []
        for j in range(A_KV_HEADS):
            sl = slice(j * LANES, (j + 1) * LANES)
            ks.append(jnp.concatenate([r[0, :, sl] for r in k_refs], axis=0))
            vs.append(jnp.concatenate([r[0, sl, :] for r in v_refs], axis=1))
        body(ks.__getitem__, vs.__getitem__)

    @pl.when(t < ctx_tiles)
    def _():
        if ctx_out:
            with_kv(functools.partial(attend, ctx_layout), [kc_ref], [vc_ref])
        else:
            o_ref[...] = jnp.zeros(o_ref.shape, o_ref.dtype)

    @pl.when(t >= ctx_tiles)
    def _():
        with_kv(functools.partial(attend, ctx_layout + win_layout),
                [kc_ref, kp_ref, kt_ref, kn_ref], [vc_ref, vp_ref, vt_ref, vn_ref])


def _attn_a(q, k, vt, sink, seq, with_ctx):
    b, s_tot, qw = q.shape
    kw, nv = k.shape[2], vt.shape[1]
    ctx_tiles = (s_tot - seq) // TM
    n_tiles = s_tot // TM
    hb = TM // WINDOW
    n_half = s_tot // WINDOW
    prev = lambda t: jnp.maximum(t * hb - 1, 0)
    nxt = lambda t: jnp.minimum((t + 1) * hb, n_half - 1)
    k_specs = [pl.BlockSpec((1, TM, kw), lambda i, t: (i, 0, 0)),
               pl.BlockSpec((1, WINDOW, kw), lambda i, t: (i, prev(t), 0)),
               pl.BlockSpec((1, TM, kw), lambda i, t: (i, t, 0)),
               pl.BlockSpec((1, WINDOW, kw), lambda i, t: (i, nxt(t), 0))]
    v_specs = [pl.BlockSpec((1, nv, TM), lambda i, t: (i, 0, 0)),
               pl.BlockSpec((1, nv, WINDOW), lambda i, t: (i, 0, prev(t))),
               pl.BlockSpec((1, nv, TM), lambda i, t: (i, 0, t)),
               pl.BlockSpec((1, nv, WINDOW), lambda i, t: (i, 0, nxt(t)))]
    return pl.pallas_call(
        functools.partial(_attn_a_kernel, ctx_tiles=ctx_tiles, n_tiles=n_tiles, ctx_out=with_ctx),
        grid=(b, n_tiles),
        in_specs=[pl.BlockSpec(memory_space=pltpu.SMEM),
                  pl.BlockSpec((1, TM, qw), lambda i, t: (i, t, 0))] + k_specs + v_specs,
        out_specs=pl.BlockSpec((1, TM, qw), lambda i, t: (i, t, 0)),
        out_shape=jax.ShapeDtypeStruct((b, s_tot, qw), BF16),
        compiler_params=_params(2),
        name="attn_gqa",
    )(sink, q, k, k, k, k, vt, vt, vt, vt)


def _attn_b_kernel(q_ref, qn_ref, k_ref, vt_ref, o_ref, s_ref, run_ref, *, ctx_len, seq, n_tiles, ctx_out):
    t = pl.program_id(2)
    ctx_tiles = ctx_len // TM
    n_pairs = q_ref.shape[2] // (2 * LANES)
    slot = lambda h: h % s_ref.shape[0]
    assert (2 * n_pairs) % s_ref.shape[0] == 0

    def score_chunk(h, q_src, k0, tk, run):
        hs = slice(h * LANES, (h + 1) * LANES)
        s = _dot_nt(k_ref[0, k0:k0 + tk, hs], q_src[0, :, hs])
        s_ref[slot(h), k0:k0 + tk, :] = s
        for i in range(tk // MAX_FOLD):
            run = jnp.maximum(run, s[i * MAX_FOLD:(i + 1) * MAX_FOLD])
        return run

    def value_chunk(h, k0, tk, mx, acc):
        p = jnp.exp2(s_ref[slot(h), k0:k0 + tk, :] - mx).astype(BF16)
        c = _dot(vt_ref[0, h * LANES:(h + 1) * LANES, k0:k0 + tk], p)
        return c if acc is None else acc + c

    def attend(chunks, first_scored, score_next):
        init = jnp.full((MAX_FOLD, TM), NEG, F32)
        run = [run_ref[0], run_ref[1]] if first_scored else None
        for pr in range(1 if first_scored else 0, n_pairs + 1):
            mx = None if run is None else [jnp.max(r, axis=0, keepdims=True) for r in run]
            run, acc = [init, init], [None, None]
            for k0, tk in chunks:
                for e in range(2):
                    if pr < n_pairs:
                        run[e] = score_chunk(2 * pr + e, q_ref, k0, tk, run[e])
                    elif score_next:
                        run[e] = score_chunk(e, qn_ref, k0, tk, run[e])
                    if pr > 0:
                        acc[e] = value_chunk(2 * (pr - 1) + e, k0, tk, mx[e], acc[e])
            if pr > 0:
                out_t = jnp.concatenate([a[:B_V] * (1.0 / a[B_V:B_V + 1]) for a in acc], axis=0)
                o_ref[0, :, (pr - 1) * LANES:pr * LANES] = out_t.T.astype(BF16)
        if score_next:
            run_ref[0], run_ref[1] = run

    ctx_chunk = [(0, ctx_len)]
    all_chunks = ctx_chunk + [(ctx_len + i * TK_MLA, TK_MLA) for i in range(seq // TK_MLA)]

    @pl.when(t < ctx_tiles)
    def _():
        if ctx_out:
            attend(ctx_chunk, False, False)
        else:
            o_ref[...] = jnp.zeros(o_ref.shape, o_ref.dtype)

    pl.when(t == ctx_tiles)(lambda: attend(all_chunks, False, ctx_tiles < n_tiles - 1))
    pl.when((t > ctx_tiles) & (t < n_tiles - 1))(lambda: attend(all_chunks, True, True))
    pl.when((t > ctx_tiles) & (t == n_tiles - 1))(lambda: attend(all_chunks, True, False))


def _attn_b(q, k, vt, seq, with_ctx):
    b, s_tot, qw = q.shape
    ctx_len = s_tot - seq
    n_tiles = s_tot // TM
    heads = 2 * MLA_PAIRS
    wide = heads * LANES
    pair = 2 * LANES
    return pl.pallas_call(
        functools.partial(_attn_b_kernel, ctx_len=ctx_len, seq=seq, n_tiles=n_tiles, ctx_out=with_ctx),
        grid=(b, qw // wide, n_tiles),
        in_specs=[pl.BlockSpec((1, TM, wide), lambda i, h, t: (i, t, h)),
                  pl.BlockSpec((1, TM, pair), lambda i, h, t: (i, jnp.minimum(t + 1, n_tiles - 1), h * MLA_PAIRS)),
                  pl.BlockSpec((1, s_tot, wide), lambda i, h, t: (i, 0, h)),
                  pl.BlockSpec((1, wide, s_tot), lambda i, h, t: (i, h, 0))],
        out_specs=pl.BlockSpec((1, TM, heads * B_V), lambda i, h, t: (i, t, h)),
        out_shape=jax.ShapeDtypeStruct((b, s_tot, B_HEADS * B_V), BF16),
        scratch_shapes=[pltpu.VMEM((min(heads, 4), s_tot, TM), F32), pltpu.VMEM((2, MAX_FOLD, TM), F32)],
        compiler_params=pltpu.CompilerParams(dimension_semantics=("parallel", "parallel", "arbitrary"),
                                             vmem_limit_bytes=VMEM_LIMIT),
        name="attn_mla",
    )(q, q, k, vt)


def _channel_kernel(*refs, rows, seg_tiles, final):
    (xp_ref, xt_ref, xn_ref, op_ref, ot_ref, on_ref, mod_ref, g_ref, wo_ref, wa_ref, wv_ref,
     cw_ref, cb_ref, wout_ref, gf_ref), y_ref = refs[:15], refs[-1]
    t = pl.program_id(1)
    ext = rows + 2 * HALO
    mid = slice(HALO, HALO + rows)
    m = mod_ref[0, 0]
    x_ext = jnp.concatenate([xp_ref[0], xt_ref[0], xn_ref[0]], axis=0)
    o_ext = jnp.concatenate([op_ref[0], ot_ref[0], on_ref[0]], axis=0)
    x1 = x_ext + m[2:3] * _dot(o_ext, wo_ref[...])
    h2 = _norm_mod(x1, g_ref[...], m[3:4], m[4:5])
    row = lax.broadcasted_iota(jnp.int32, (ext, 1), 0)
    keep = ((row >= HALO) | (t > 0)) & ((row < HALO + rows) | (t < seg_tiles - 1))
    h2 = jnp.where(keep, h2, 0.0).astype(BF16)
    h2_mid = h2[mid]
    cw = cw_ref[...]
    cb = cb_ref[...]
    d_ff = wa_ref.shape[1]
    y = None
    for c0 in range(0, d_ff, FF_CHUNK):
        cs = slice(c0, min(c0 + FF_CHUNK, d_ff))
        a = _dot(h2, wa_ref[:, cs])
        a_prev = pltpu.roll(a, 1, 0)[mid]
        a_next = pltpu.roll(a, ext - 1, 0)[mid]
        conv = a_prev * cw[0:1, cs] + a[mid] * cw[1:2, cs] + a_next * cw[2:3, cs] + cb[:, cs]
        gate = conv * (1.0 / (1.0 + jnp.exp(-conv)))
        hid = (gate * _dot(h2_mid, wv_ref[:, cs])).astype(BF16)
        part = _dot(hid, wout_ref[cs, :])
        y = part if y is None else y + part
    x2 = x1[mid] + m[5:6] * y
    y_ref[0] = _rms(x2, gf_ref[...]) if final else x2


def _rows_at(n_rows, width, offset):
    return pl.BlockSpec((pl.Element(1), pl.Element(n_rows), pl.Element(width)),
                        lambda i, t: (i, pl.multiple_of(offset(t), HALO), 0))


def _channel(xs, o, modl, g, wo, wa, wv, cw, cb, wout, gf, seq, final):
    b, s_tot, d = xs.shape
    ctx_len = s_tot - seq
    weights = (g, wo, wa, wv, cw, cb, wout, gf)
    w_specs = [_resident(w.shape) for w in weights]

    def call(rows, base, seg_rows, mod_row, out_spec, out_rows, alias):
        seg_tiles = seg_rows // rows
        prev = lambda t: jnp.maximum(base + t * rows - HALO, 0)
        cur = lambda t: base + t * rows
        nxt = lambda t: jnp.minimum(base + (t + 1) * rows, base + seg_rows - HALO)
        tiles = lambda w: [_rows_at(HALO, w, prev), _rows_at(rows, w, cur), _rows_at(HALO, w, nxt)]
        in_specs = (tiles(d) + tiles(o.shape[2])
                    + [pl.BlockSpec((1, 1, 6, d), lambda i, t: (i, mod_row, 0, 0))] + w_specs)
        args = (xs, xs, xs, o, o, o, modl) + weights
        if alias is not None:
            in_specs.append(pl.BlockSpec(memory_space=pl.ANY))
            args += (alias,)
        return pl.pallas_call(
            functools.partial(_channel_kernel, rows=rows, seg_tiles=seg_tiles, final=final),
            grid=(b, seg_tiles),
            in_specs=in_specs,
            out_specs=out_spec,
            out_shape=jax.ShapeDtypeStruct((b, out_rows, d), F32),
            input_output_aliases={} if alias is None else {len(args) - 1: 0},
            compiler_params=_params(2),
            name="channel",
        )(*args)

    if final:
        return call(TM_CHANNEL, ctx_len, seq, 1, pl.BlockSpec((1, TM_CHANNEL, d), lambda i, t: (i, t, 0)), seq, None)
    y = call(TM_CHANNEL, ctx_len, seq, 1, _rows_at(TM_CHANNEL, d, lambda t: ctx_len + t * TM_CHANNEL), s_tot, None)
    return call(TM, 0, ctx_len, 0, pl.BlockSpec((1, TM, d), lambda i, t: (i, t, 0)), s_tot, y)


def _axial_angles(rows, rot_dim):
    row = jnp.repeat(jnp.arange(rows), GRID_W).astype(F32)
    col = jnp.tile(jnp.arange(GRID_W), rows).astype(F32)
    n_freq = rot_dim // 4
    inv = ROPE_BASE ** (-jnp.arange(n_freq, dtype=F32) / n_freq)
    return jnp.concatenate([row[:, None] * inv, col[:, None] * inv], axis=-1)


def _with_ctx_rows(cos, sin, ctx_cos, ctx_len, q_scale):
    cos = jnp.concatenate([jnp.broadcast_to(ctx_cos, (ctx_len, LANES)), cos], axis=0)
    sin = jnp.concatenate([jnp.zeros((ctx_len, LANES), F32), sin], axis=0)
    return jnp.stack([cos * q_scale, sin * q_scale, cos, sin])


def _tables_a(seq, ctx_len):
    ang = _axial_angles(seq // GRID_W, HEAD_DIM)
    cos, sin = jnp.cos(ang), jnp.sin(ang)
    reps = LANES // HEAD_DIM
    cos_l = jnp.tile(cos, (1, 2 * reps))
    sin_l = jnp.tile(jnp.concatenate([-sin, sin], axis=1), (1, reps))
    return _with_ctx_rows(cos_l, sin_l, jnp.ones((1, LANES), F32), ctx_len, LOG2E * HEAD_DIM ** -0.5)


def _tables_b(seq, ctx_len):
    ang = _axial_angles(seq // GRID_W, B_ROPE)
    cos, sin = jnp.cos(ang), jnp.sin(ang)
    copy = LANES - B_NOPE - B_ROPE
    cos_l = jnp.concatenate([jnp.ones((seq, B_NOPE), F32), cos, cos, jnp.zeros((seq, copy), F32)], axis=1)
    sin_l = jnp.concatenate([jnp.zeros((seq, B_NOPE), F32), -sin, sin, jnp.zeros((seq, copy), F32)], axis=1)
    ctx_cos = jnp.concatenate([jnp.ones((1, B_NOPE + B_ROPE), F32), jnp.zeros((1, copy), F32)], axis=1)
    return _with_ctx_rows(cos_l, sin_l, ctx_cos, ctx_len, LOG2E * (B_NOPE + B_ROPE) ** -0.5)


def _deinterleave(n):
    return jnp.concatenate([jnp.arange(0, n, 2), jnp.arange(1, n, 2)])


def _weights_a(wqkv):
    d = wqkv.shape[0]
    kw = A_KV_HEADS * HEAD_DIM
    qw = wqkv.shape[1] - 2 * kw
    perm = _deinterleave(HEAD_DIM)
    wq = wqkv[:, :qw].reshape(d, -1, HEAD_DIM)[:, :, perm].reshape(d, qw)
    wk = wqkv[:, qw:qw + kw].reshape(d, A_KV_HEADS, HEAD_DIM)[:, :, perm]
    wkk = jnp.concatenate([wk, wk], axis=2).reshape(d, 2 * kw)
    wv = wqkv[:, qw + kw:].reshape(d, A_KV_HEADS, HEAD_DIM)
    wv = jnp.concatenate([wv, jnp.zeros((d, A_KV_HEADS, LANES - HEAD_DIM), F32)], axis=2)
    wvt = wv.reshape(d, A_KV_HEADS * LANES).T
    return jnp.concatenate([wq, wkk], axis=1).astype(BF16), wvt.astype(BF16), qw, 2 * kw


def _weights_b(wdown, wuq, wuk, wuv, q_lora, kv_lora):
    d = wdown.shape[0]
    perm = _deinterleave(B_ROPE)
    assert LANES - B_NOPE - B_ROPE == B_ROPE
    w_rope = wdown[:, q_lora + kv_lora:][:, perm]
    w_rope = jnp.concatenate([jnp.zeros((d, B_NOPE), F32), w_rope, w_rope], axis=1)
    wd = jnp.concatenate([wdown[:, :q_lora + kv_lora], w_rope], axis=1)
    uq = wuq.reshape(q_lora, B_HEADS, B_NOPE + B_ROPE)
    uq_rope = uq[:, :, B_NOPE:][:, :, perm]
    uq = jnp.concatenate([uq[:, :, :B_NOPE], uq_rope, uq_rope], axis=2)
    uk = wuk.reshape(kv_lora, B_HEADS, B_NOPE)
    uk = jnp.concatenate([uk, jnp.zeros((kv_lora, B_HEADS, LANES - B_NOPE), F32)], axis=2)
    uv = wuv.reshape(kv_lora, B_HEADS, B_V)
    uv = jnp.concatenate([uv, jnp.zeros((kv_lora, B_HEADS, LANES - B_V), F32)], axis=2)
    flat = lambda w: w.reshape(w.shape[0], B_HEADS * LANES).astype(BF16)
    return wd.astype(BF16), flat(uq), flat(uk), flat(uv).T


def kernel(x, c, ctx, c_ctx, mod_w, mod_b, norm1_g, norm2_g, a_wqkv, a_wo, a_sink, b_wdown, b_qnorm_g, b_wuq,
           b_kvnorm_g, b_wuk, b_wuv, b_wo, f_win, f_conv_w, f_conv_b, f_wout, final_g):
    bsz, seq, d = x.shape
    ctx_len = ctx.shape[1]
    depth = mod_w.shape[0]
    d_ff = f_wout.shape[1]
    assert ctx_len == TM and seq % TK_MLA == 0 and seq % TM_CHANNEL == 0 and seq % GRID_W == 0

    cond_rows = -(-(bsz + 1) // SUBLANES) * SUBLANES
    cond = jnp.concatenate([c, c_ctx[None], jnp.zeros((cond_rows - bsz - 1, d), F32)], axis=0)
    mod = _modulation(cond, mod_w, mod_b)

    tab_a = _tables_a(seq, ctx_len)
    tab_b = _tables_b(seq, ctx_len)
    xs = None
    row = lambda v: v.reshape(1, -1)

    for i in range(depth):
        last = i == depth - 1
        lat = mod[i, :bsz].reshape(bsz, 6, d)
        cmod = jnp.broadcast_to(mod[i, bsz].reshape(1, 6, d), (bsz, 6, d))
        modl = jnp.stack([cmod, lat], axis=1)
        j = i // 2
        if i % 2 == 0:
            w, wvt, qw, kw = _weights_a(a_wqkv[j])
            if i == 0:
                xs, q, k, vt = _proj_a((ctx, x), modl, row(norm1_g[i]), w, wvt, tab_a, qw, kw)
            else:
                q, k, vt = _proj_a(xs, modl, row(norm1_g[i]), w, wvt, tab_a, qw, kw)
            o = _attn_a(q, k, vt, a_sink[j], seq, not last)
            wo = a_wo[j]
        else:
            q_lora, kv_lora = b_qnorm_g.shape[1], b_kvnorm_g.shape[1]
            wd, wuq, wuk, wuvt = _weights_b(b_wdown[j], b_wuq[j], b_wuk[j], b_wuv[j], q_lora, kv_lora)
            q, k, vt = _proj_b(xs, modl, row(norm1_g[i]), wd, row(b_qnorm_g[j]), row(b_kvnorm_g[j]),
                               wuq, wuk, wuvt, tab_b)
            o = _attn_b(q, k, vt, seq, not last)
            wo = b_wo[j]
        xs = _channel(xs, o, modl, row(norm2_g[i]), wo.astype(BF16), f_win[i][:, :d_ff].astype(BF16),
                      f_win[i][:, d_ff:].astype(BF16), f_conv_w[i], row(f_conv_b[i]), f_wout[i].astype(BF16),
                      row(final_g), seq, last)
    return xs
```

```python
import functools

import jax
import jax.numpy as jnp
from jax import lax
from jax.experimental import pallas as pl
from jax.experimental.pallas import tpu as pltpu

GRID_W = 64
HEAD_DIM = 64
A_KV_HEADS = 4
WINDOW = 128
B_HEADS = 16
B_NOPE = 64
B_ROPE = 32
B_V = 64
ROPE_BASE = 10000.0
EPS = 1e-6
NEG = -1e30
LOG2E = 1.4426950408889634

LANES = 128
SUBLANES = 8
V7X_VMEM_BYTES = 64 * 1024 * 1024
VMEM_LIMIT = V7X_VMEM_BYTES * 7 // 8
TM = 256
HALO = 16
TK_MLA = 1024
MLA_PAIRS = 4
GQA_LAG = 3
MAX_FOLD = 64
TM_CHANNEL = 512
FF_CHUNK = 768

F32 = jnp.float32
BF16 = jnp.bfloat16
_NT = (((1,), (1,)), ((), ()))


def _dot(a, b):
    return jnp.dot(a, b, preferred_element_type=F32)


def _dot_nt(a, b):
    return lax.dot_general(a, b, _NT, preferred_element_type=F32)


def _rms(xf, g):
    ms = jnp.mean(xf * xf, axis=-1, keepdims=True)
    return xf * lax.rsqrt(ms + EPS) * g


def _norm_mod(xf, g, shift, scale):
    return _rms(xf, g) * (1.0 + scale) + shift


def _params(n_axes):
    return pltpu.CompilerParams(dimension_semantics=("parallel",) * n_axes,
                                vmem_limit_bytes=VMEM_LIMIT)


def _resident(shape):
    nd = len(shape)
    return pl.BlockSpec(shape, lambda *_: (0,) * nd, pipeline_mode=pl.Buffered(1))


def _mod_kernel(c_ref, w_ref, b_ref, o_ref):
    c = c_ref[...]
    silu = c * (1.0 / (1.0 + jnp.exp(-c)))
    o_ref[0] = _dot(silu.astype(BF16), w_ref[0].astype(BF16)) + b_ref[0]


def _modulation(cond, mod_w, mod_b):
    depth, d, d6 = mod_w.shape
    rows = cond.shape[0]
    return pl.pallas_call(
        _mod_kernel,
        grid=(depth, d6 // d),
        in_specs=[pl.BlockSpec((rows, d), lambda i, j: (0, 0)),
                  pl.BlockSpec((1, d, d), lambda i, j: (i, 0, j)),
                  pl.BlockSpec((1, 1, d), lambda i, j: (i, 0, j))],
        out_specs=pl.BlockSpec((1, rows, d), lambda i, j: (i, 0, j)),
        out_shape=jax.ShapeDtypeStruct((depth, rows, d6), F32),
        compiler_params=_params(2),
        name="modulation",
    )(cond, mod_w, mod_b.reshape(depth, 1, d6))


def _rope_block(xb, cos, sin_signed, first_half, half):
    swapped = jnp.where(first_half, pltpu.roll(xb, LANES - half, 1), pltpu.roll(xb, half, 1))
    return xb * cos + swapped * sin_signed


def _rope_block_dup(xb, cos, sin_signed, half):
    return xb * cos + pltpu.roll(xb, LANES - half, 1) * sin_signed


def _proj_a_kernel(x_ref, mod_ref, g_ref, w_ref, wvt_ref, tab_ref, q_ref, k_ref, vt_ref, *, qw, kw):
    m = mod_ref[0, 0]
    h = _norm_mod(x_ref[0], g_ref[...], m[0:1], m[1:2]).astype(BF16)
    qk = _dot(h, w_ref[...])
    lane = lax.broadcasted_iota(jnp.int32, (TM, LANES), 1)
    first = (lane % HEAD_DIM) < (HEAD_DIM // 2)
    cq, sq, ck, sk = tab_ref[0], tab_ref[1], tab_ref[2], tab_ref[3]
    for j in range(qw // LANES):
        blk = qk[:, j * LANES:(j + 1) * LANES]
        q_ref[0, :, j * LANES:(j + 1) * LANES] = _rope_block(blk, cq, sq, first, HEAD_DIM // 2).astype(BF16)
    for j in range(kw // LANES):
        blk = qk[:, qw + j * LANES:qw + (j + 1) * LANES]
        k_ref[0, :, j * LANES:(j + 1) * LANES] = _rope_block_dup(blk, ck, sk, HEAD_DIM // 2).astype(BF16)
    vrow = lax.broadcasted_iota(jnp.int32, (wvt_ref.shape[0], 1), 0) % LANES
    vt_ref[0] = (_dot_nt(wvt_ref[...], h) + jnp.where(vrow == HEAD_DIM, 1.0, 0.0)).astype(BF16)


def _proj_a_first_kernel(ctx_ref, lat_ref, mod_ref, g_ref, w_ref, wvt_ref, tab_ref, xs_ref, q_ref, k_ref, vt_ref,
                         *, ctx_tiles, **kw):
    xs_ref[0] = jnp.where(pl.program_id(1) < ctx_tiles, ctx_ref[0], lat_ref[0])
    _proj_a_kernel(xs_ref, mod_ref, g_ref, w_ref, wvt_ref, tab_ref, q_ref, k_ref, vt_ref, **kw)


def _proj_a(xs, modl, g, w, wvt, tab, qw, kw):
    first = isinstance(xs, tuple)
    if first:
        ctx, lat = xs
        b, ctx_len, d = ctx.shape
        ctx_tiles = ctx_len // TM
        s_tot = ctx_len + lat.shape[1]
        x_args = (ctx, lat)
        x_specs = [pl.BlockSpec((1, TM, d), lambda i, t: (i, jnp.minimum(t, ctx_tiles - 1), 0)),
                   pl.BlockSpec((1, TM, d), lambda i, t: (i, jnp.maximum(t - ctx_tiles, 0), 0))]
        body = functools.partial(_proj_a_first_kernel, ctx_tiles=ctx_tiles, qw=qw, kw=kw)
    else:
        b, s_tot, d = xs.shape
        x_args = (xs,)
        x_specs = [pl.BlockSpec((1, TM, d), lambda i, t: (i, t, 0))]
        body = functools.partial(_proj_a_kernel, qw=qw, kw=kw)
    nt = s_tot // TM
    nv = wvt.shape[0]
    out = lambda n: jax.ShapeDtypeStruct((b, s_tot, n), BF16)
    ospec = lambda n: pl.BlockSpec((1, TM, n), lambda i, t: (i, t, 0))
    out_specs = [ospec(qw), ospec(kw), pl.BlockSpec((1, nv, TM), lambda i, t: (i, 0, t))]
    out_shape = [out(qw), out(kw), jax.ShapeDtypeStruct((b, nv, s_tot), BF16)]
    if first:
        out_specs.insert(0, ospec(d))
        out_shape.insert(0, jax.ShapeDtypeStruct((b, s_tot, d), F32))
    return pl.pallas_call(
        body,
        grid=(b, nt),
        in_specs=x_specs + [pl.BlockSpec((1, 1, 6, d), lambda i, t: (i, jnp.minimum(t, 1), 0, 0)),
                            _resident(g.shape), _resident(w.shape), _resident(wvt.shape),
                            pl.BlockSpec((4, TM, LANES), lambda i, t: (0, t, 0))],
        out_specs=out_specs,
        out_shape=out_shape,
        compiler_params=_params(2),
        name="proj_gqa",
    )(*x_args, modl, g, w, wvt, tab)


def _proj_b_kernel(x_ref, mod_ref, g_ref, wd_ref, gq_ref, gkv_ref, wuq_ref, wuk_ref, wuvt_ref, tab_ref,
                   q_ref, k_ref, vt_ref, *, q_lora, kv_lora):
    m = mod_ref[0, 0]
    h = _norm_mod(x_ref[0], g_ref[...], m[0:1], m[1:2]).astype(BF16)
    d = _dot(h, wd_ref[...])
    cq = _rms(d[:, :q_lora], gq_ref[...]).astype(BF16)
    ckv = _rms(d[:, q_lora:q_lora + kv_lora], gkv_ref[...]).astype(BF16)
    tq_c, tq_s, tk_c, tk_s = tab_ref[0], tab_ref[1], tab_ref[2], tab_ref[3]
    k_rope = _rope_block_dup(d[:, q_lora + kv_lora:], tk_c, tk_s, B_ROPE // 2)
    q = _dot(cq, wuq_ref[...])
    k = _dot(ckv, wuk_ref[...])
    for j in range(B_HEADS):
        sl = slice(j * LANES, (j + 1) * LANES)
        q_ref[0, :, sl] = _rope_block_dup(q[:, sl], tq_c, tq_s, B_ROPE // 2).astype(BF16)
        k_ref[0, :, sl] = (k[:, sl] + k_rope).astype(BF16)
    vrow = lax.broadcasted_iota(jnp.int32, (wuvt_ref.shape[0], 1), 0) % LANES
    vt_ref[0] = (_dot_nt(wuvt_ref[...], ckv) + jnp.where(vrow == B_V, 1.0, 0.0)).astype(BF16)


def _proj_b(xs, modl, g, wd, gq, gkv, wuq, wuk, wuvt, tab):
    b, s_tot, d = xs.shape
    nt = s_tot // TM
    out = lambda n: jax.ShapeDtypeStruct((b, s_tot, n), BF16)
    ospec = lambda n: pl.BlockSpec((1, TM, n), lambda i, t: (i, t, 0))
    nq, nv = wuq.shape[1], wuvt.shape[0]
    return pl.pallas_call(
        functools.partial(_proj_b_kernel, q_lora=gq.shape[1], kv_lora=gkv.shape[1]),
        grid=(b, nt),
        in_specs=[pl.BlockSpec((1, TM, d), lambda i, t: (i, t, 0)),
                  pl.BlockSpec((1, 1, 6, d), lambda i, t: (i, jnp.minimum(t, 1), 0, 0)),
                  _resident(g.shape), _resident(wd.shape), _resident(gq.shape), _resident(gkv.shape),
                  _resident(wuq.shape), _resident(wuk.shape), _resident(wuvt.shape),
                  pl.BlockSpec((4, TM, LANES), lambda i, t: (0, t, 0))],
        out_specs=[ospec(nq), ospec(nq), pl.BlockSpec((1, nv, TM), lambda i, t: (i, 0, t))],
        out_shape=[out(nq), out(nq), jax.ShapeDtypeStruct((b, nv, s_tot), BF16)],
        compiler_params=_params(2),
        name="proj_mla",
    )(xs, modl, g, wd, gq, gkv, wuq, wuk, wuvt, tab)


def _attn_a_kernel(sink_ref, q_ref, kc_ref, kp_ref, kt_ref, kn_ref, vc_ref, vp_ref, vt_ref, vn_ref, o_ref,
                   *, ctx_tiles, n_tiles, ctx_out):
    t = pl.program_id(1)
    assert TM == 2 * WINDOW
    edge = WINDOW
    n_heads = q_ref.shape[2] // HEAD_DIM
    group = n_heads // A_KV_HEADS
    lane = lax.broadcasted_iota(jnp.int32, (1, LANES), 1)
    lo = lane < HEAD_DIM
    zero = jnp.zeros((), BF16)

    key = lax.broadcasted_iota(jnp.int32, (edge, edge), 0)
    qry = lax.broadcasted_iota(jnp.int32, (edge, edge), 1)
    tri = lambda valid: jnp.where(valid, 0.0, NEG).astype(F32)
    ALL, NONE = "all", "none"
    ctx_layout = [(ALL, ALL)] * (kc_ref.shape[1] // edge)
    win_layout = [(tri((qry <= key) & (t > ctx_tiles)), NONE),
                  (ALL, tri(qry <= key)), (tri(qry >= key), ALL),
                  (NONE, tri((qry >= key) & (t < n_tiles - 1)))]

    def attend(layout, k_of, v_of):
        def score(h):
            qp = q_ref[0, :, (h // 2) * LANES:(h // 2 + 1) * LANES]
            qe = jnp.where(lo, qp, zero) if h % 2 == 0 else jnp.where(lo, zero, qp)
            s = _dot_nt(k_of(h // group), qe)
            sink = sink_ref[h] * LOG2E
            halves = []
            for half in range(2):
                blocks, run = [], None
                for rb, kinds in enumerate(layout):
                    kind = kinds[half]
                    if isinstance(kind, str) and kind == NONE:
                        blocks.append(None)
                        continue
                    blk = s[rb * edge:(rb + 1) * edge, half * edge:(half + 1) * edge]
                    if not isinstance(kind, str):
                        blk = blk + kind
                    blocks.append(blk)
                    for i in range(edge // MAX_FOLD):
                        part = blk[i * MAX_FOLD:(i + 1) * MAX_FOLD]
                        run = part if run is None else jnp.maximum(run, part)
                halves.append((blocks, jnp.maximum(jnp.max(run, axis=0, keepdims=True), sink)))
            return halves, sink

        def probs(st):
            halves, sink = st
            rows = []
            for rb in range(len(layout)):
                parts = [jnp.zeros((edge, edge), BF16) if blocks[rb] is None
                         else jnp.exp2(blocks[rb] - mx).astype(BF16) for blocks, mx in halves]
                rows.append(jnp.concatenate(parts, axis=1))
            p_sink = jnp.concatenate([jnp.exp2(sink - mx) for _, mx in halves], axis=1)
            return jnp.concatenate(rows, axis=0), p_sink

        def values(h, pr):
            p, p_sink = pr
            acc = _dot(v_of(h // group), p)
            return acc[:HEAD_DIM] * (1.0 / (acc[HEAD_DIM:HEAD_DIM + 1] + p_sink))

        scored, weighted, outs = {}, {}, {}
        for step in range(n_heads + 2 * GQA_LAG):
            h = step - 2 * GQA_LAG
            if 0 <= h < n_heads:
                outs[h] = values(h, weighted.pop(h))
                if h % 2 == 1:
                    pair_t = jnp.concatenate([outs.pop(h - 1), outs.pop(h)], axis=0)
                    o_ref[0, :, (h // 2) * LANES:(h // 2 + 1) * LANES] = pair_t.T.astype(BF16)
            if 0 <= step - GQA_LAG < n_heads:
                weighted[step - GQA_LAG] = probs(scored.pop(step - GQA_LAG))
            if step < n_heads:
                scored[step] = score(step)

    def with_kv(body, k_refs, v_refs):
        ks, vs = [], []
        for j in range(A_KV_HEADS):
            sl = slice(j * LANES, (j + 1) * LANES)
            ks.append(jnp.concatenate([r[0, :, sl] for r in k_refs], axis=0))
            vs.append(jnp.concatenate([r[0, sl, :] for r in v_refs], axis=1))
        body(ks.__getitem__, vs.__getitem__)

    @pl.when(t < ctx_tiles)
    def _():
        if ctx_out:
            with_kv(functools.partial(attend, ctx_layout), [kc_ref], [vc_ref])
        else:
            o_ref[...] = jnp.zeros(o_ref.shape, o_ref.dtype)

    @pl.when(t >= ctx_tiles)
    def _():
        with_kv(functools.partial(attend, ctx_layout + win_layout),
                [kc_ref, kp_ref, kt_ref, kn_ref], [vc_ref, vp_ref, vt_ref, vn_ref])


def _attn_a(q, k, vt, sink, seq, with_ctx):
    b, s_tot, qw = q.shape
    kw, nv = k.shape[2], vt.shape[1]
    ctx_tiles = (s_tot - seq) // TM
    n_tiles = s_tot // TM
    hb = TM // WINDOW
    n_half = s_tot // WINDOW
    prev = lambda t: jnp.maximum(t * hb - 1, 0)
    nxt = lambda t: jnp.minimum((t + 1) * hb, n_half - 1)
    k_specs = [pl.BlockSpec((1, TM, kw), lambda i, t: (i, 0, 0)),
               pl.BlockSpec((1, WINDOW, kw), lambda i, t: (i, prev(t), 0)),
               pl.BlockSpec((1, TM, kw), lambda i, t: (i, t, 0)),
               pl.BlockSpec((1, WINDOW, kw), lambda i, t: (i, nxt(t), 0))]
    v_specs = [pl.BlockSpec((1, nv, TM), lambda i, t: (i, 0, 0)),
               pl.BlockSpec((1, nv, WINDOW), lambda i, t: (i, 0, prev(t))),
               pl.BlockSpec((1, nv, TM), lambda i, t: (i, 0, t)),
               pl.BlockSpec((1, nv, WINDOW), lambda i, t: (i, 0, nxt(t)))]
    return pl.pallas_call(
        functools.partial(_attn_a_kernel, ctx_tiles=ctx_tiles, n_tiles=n_tiles, ctx_out=with_ctx),
        grid=(b, n_tiles),
        in_specs=[pl.BlockSpec(memory_space=pltpu.SMEM),
                  pl.BlockSpec((1, TM, qw), lambda i, t: (i, t, 0))] + k_specs + v_specs,
        out_specs=pl.BlockSpec((1, TM, qw), lambda i, t: (i, t, 0)),
        out_shape=jax.ShapeDtypeStruct((b, s_tot, qw), BF16),
        compiler_params=_params(2),
        name="attn_gqa",
    )(sink, q, k, k, k, k, vt, vt, vt, vt)


def _attn_b_kernel(q_ref, qn_ref, k_ref, vt_ref, o_ref, s_ref, run_ref, *, ctx_len, seq, n_tiles, ctx_out):
    t = pl.program_id(2)
    ctx_tiles = ctx_len // TM
    n_pairs = q_ref.shape[2] // (2 * LANES)
    slot = lambda h: h % s_ref.shape[0]
    assert (2 * n_pairs) % s_ref.shape[0] == 0

    def score_chunk(h, q_src, k0, tk, run):
        hs = slice(h * LANES, (h + 1) * LANES)
        s = _dot_nt(k_ref[0, k0:k0 + tk, hs], q_src[0, :, hs])
        s_ref[slot(h), k0:k0 + tk, :] = s
        for i in range(tk // MAX_FOLD):
            run = jnp.maximum(run, s[i * MAX_FOLD:(i + 1) * MAX_FOLD])
        return run

    def value_chunk(h, k0, tk, mx, acc):
        p = jnp.exp2(s_ref[slot(h), k0:k0 + tk, :] - mx).astype(BF16)
        c = _dot(vt_ref[0, h * LANES:(h + 1) * LANES, k0:k0 + tk], p)
        return c if acc is None else acc + c

    def attend(chunks, first_scored, score_next):
        init = jnp.full((MAX_FOLD, TM), NEG, F32)
        run = [run_ref[0], run_ref[1]] if first_scored else None
        for pr in range(1 if first_scored else 0, n_pairs + 1):
            mx = None if run is None else [jnp.max(r, axis=0, keepdims=True) for r in run]
            run, acc = [init, init], [None, None]
            for k0, tk in chunks:
                for e in range(2):
                    if pr < n_pairs:
                        run[e] = score_chunk(2 * pr + e, q_ref, k0, tk, run[e])
                    elif score_next:
                        run[e] = score_chunk(e, qn_ref, k0, tk, run[e])
                    if pr > 0:
                        acc[e] = value_chunk(2 * (pr - 1) + e, k0, tk, mx[e], acc[e])
            if pr > 0:
                out_t = jnp.concatenate([a[:B_V] * (1.0 / a[B_V:B_V + 1]) for a in acc], axis=0)
                o_ref[0, :, (pr - 1) * LANES:pr * LANES] = out_t.T.astype(BF16)
        if score_next:
            run_ref[0], run_ref[1] = run

    ctx_chunk = [(0, ctx_len)]
    all_chunks = ctx_chunk + [(ctx_len + i * TK_MLA, TK_MLA) for i in range(seq // TK_MLA)]

    @pl.when(t < ctx_tiles)
    def _():
        if ctx_out:
            attend(ctx_chunk, False, False)
        else:
            o_ref[...] = jnp.zeros(o_ref.shape, o_ref.dtype)

    pl.when(t == ctx_tiles)(lambda: attend(all_chunks, False, ctx_tiles < n_tiles - 1))
    pl.when((t > ctx_tiles) & (t < n_tiles - 1))(lambda: attend(all_chunks, True, True))
    pl.when((t > ctx_tiles) & (t == n_tiles - 1))(lambda: attend(all_chunks, True, False))


def _attn_b(q, k, vt, seq, with_ctx):
    b, s_tot, qw = q.shape
    ctx_len = s_tot - seq
    n_tiles = s_tot // TM
    heads = 2 * MLA_PAIRS
    wide = heads * LANES
    pair = 2 * LANES
    return pl.pallas_call(
        functools.partial(_attn_b_kernel, ctx_len=ctx_len, seq=seq, n_tiles=n_tiles, ctx_out=with_ctx),
        grid=(b, qw // wide, n_tiles),
        in_specs=[pl.BlockSpec((1, TM, wide), lambda i, h, t: (i, t, h)),
                  pl.BlockSpec((1, TM, pair), lambda i, h, t: (i, jnp.minimum(t + 1, n_tiles - 1), h * MLA_PAIRS)),
                  pl.BlockSpec((1, s_tot, wide), lambda i, h, t: (i, 0, h)),
                  pl.BlockSpec((1, wide, s_tot), lambda i, h, t: (i, h, 0))],
        out_specs=pl.BlockSpec((1, TM, heads * B_V), lambda i, h, t: (i, t, h)),
        out_shape=jax.ShapeDtypeStruct((b, s_tot, B_HEADS * B_V), BF16),
        scratch_shapes=[pltpu.VMEM((min(heads, 4), s_tot, TM), F32), pltpu.VMEM((2, MAX_FOLD, TM), F32)],
        compiler_params=pltpu.CompilerParams(dimension_semantics=("parallel", "parallel", "arbitrary"),
                                             vmem_limit_bytes=VMEM_LIMIT),
        name="attn_mla",
    )(q, q, k, vt)


def _channel_kernel(*refs, rows, seg_tiles, final):
    (xp_ref, xt_ref, xn_ref, op_ref, ot_ref, on_ref, mod_ref, g_ref, wo_ref, wa_ref, wv_ref,
     cw_ref, cb_ref, wout_ref, gf_ref), y_ref = refs[:15], refs[-1]
    t = pl.program_id(1)
    ext = rows + 2 * HALO
    mid = slice(HALO, HALO + rows)
    m = mod_ref[0, 0]
    x_ext = jnp.concatenate([xp_ref[0], xt_ref[0], xn_ref[0]], axis=0)
    o_ext = jnp.concatenate([op_ref[0], ot_ref[0], on_ref[0]], axis=0)
    x1 = x_ext + m[2:3] * _dot(o_ext, wo_ref[...])
    h2 = _norm_mod(x1, g_ref[...], m[3:4], m[4:5])
    row = lax.broadcasted_iota(jnp.int32, (ext, 1), 0)
    keep = ((row >= HALO) | (t > 0)) & ((row < HALO + rows) | (t < seg_tiles - 1))
    h2 = jnp.where(keep, h2, 0.0).astype(BF16)
    h2_mid = h2[mid]
    cw = cw_ref[...]
    cb = cb_ref[...]
    d_ff = wa_ref.shape[1]
    y = None
    chunks = [slice(c0, min(c0 + FF_CHUNK, d_ff)) for c0 in range(0, d_ff, FF_CHUNK)]
    branches = lambda cs: (_dot(h2, wa_ref[:, cs]), _dot(h2_mid, wv_ref[:, cs]))
    ahead = branches(chunks[0])
    for i, cs in enumerate(chunks):
        a, v = ahead
        if i + 1 < len(chunks):
            ahead = branches(chunks[i + 1])
        a_prev = pltpu.roll(a, 1, 0)[mid]
        a_next = pltpu.roll(a, ext - 1, 0)[mid]
        conv = a_prev * cw[0:1, cs] + a[mid] * cw[1:2, cs] + a_next * cw[2:3, cs] + cb[:, cs]
        gate = conv * (1.0 / (1.0 + jnp.exp(-conv)))
        hid = (gate * v).astype(BF16)
        part = _dot(hid, wout_ref[cs, :])
        y = part if y is None else y + part
    x2 = x1[mid] + m[5:6] * y
    y_ref[0] = _rms(x2, gf_ref[...]) if final else x2


def _rows_at(n_rows, width, offset):
    return pl.BlockSpec((pl.Element(1), pl.Element(n_rows), pl.Element(width)),
                        lambda i, t: (i, pl.multiple_of(offset(t), HALO), 0))


def _channel(xs, o, modl, g, wo, wa, wv, cw, cb, wout, gf, seq, final):
    b, s_tot, d = xs.shape
    ctx_len = s_tot - seq
    weights = (g, wo, wa, wv, cw, cb, wout, gf)
    w_specs = [_resident(w.shape) for w in weights]

    def call(rows, base, seg_rows, mod_row, out_spec, out_rows, alias):
        seg_tiles = seg_rows // rows
        prev = lambda t: jnp.maximum(base + t * rows - HALO, 0)
        cur = lambda t: base + t * rows
        nxt = lambda t: jnp.minimum(base + (t + 1) * rows, base + seg_rows - HALO)
        tiles = lambda w: [_rows_at(HALO, w, prev), _rows_at(rows, w, cur), _rows_at(HALO, w, nxt)]
        in_specs = (tiles(d) + tiles(o.shape[2])
                    + [pl.BlockSpec((1, 1, 6, d), lambda i, t: (i, mod_row, 0, 0))] + w_specs)
        args = (xs, xs, xs, o, o, o, modl) + weights
        if alias is not None:
            in_specs.append(pl.BlockSpec(memory_space=pl.ANY))
            args += (alias,)
        return pl.pallas_call(
            functools.partial(_channel_kernel, rows=rows, seg_tiles=seg_tiles, final=final),
            grid=(b, seg_tiles),
            in_specs=in_specs,
            out_specs=out_spec,
            out_shape=jax.ShapeDtypeStruct((b, out_rows, d), F32),
            input_output_aliases={} if alias is None else {len(args) - 1: 0},
            compiler_params=_params(2),
            name="channel",
        )(*args)

    if final:
        return call(TM_CHANNEL, ctx_len, seq, 1, pl.BlockSpec((1, TM_CHANNEL, d), lambda i, t: (i, t, 0)), seq, None)
    y = call(TM_CHANNEL, ctx_len, seq, 1, _rows_at(TM_CHANNEL, d, lambda t: ctx_len + t * TM_CHANNEL), s_tot, None)
    return call(TM, 0, ctx_len, 0, pl.BlockSpec((1, TM, d), lambda i, t: (i, t, 0)), s_tot, y)


def _axial_angles(rows, rot_dim):
    row = jnp.repeat(jnp.arange(rows), GRID_W).astype(F32)
    col = jnp.tile(jnp.arange(GRID_W), rows).astype(F32)
    n_freq = rot_dim // 4
    inv = ROPE_BASE ** (-jnp.arange(n_freq, dtype=F32) / n_freq)
    return jnp.concatenate([row[:, None] * inv, col[:, None] * inv], axis=-1)


def _with_ctx_rows(cos, sin, ctx_cos, ctx_len, q_scale):
    cos = jnp.concatenate([jnp.broadcast_to(ctx_cos, (ctx_len, LANES)), cos], axis=0)
    sin = jnp.concatenate([jnp.zeros((ctx_len, LANES), F32), sin], axis=0)
    return jnp.stack([cos * q_scale, sin * q_scale, cos, sin])


def _tables_a(seq, ctx_len):
    ang = _axial_angles(seq // GRID_W, HEAD_DIM)
    cos, sin = jnp.cos(ang), jnp.sin(ang)
    reps = LANES // HEAD_DIM
    cos_l = jnp.tile(cos, (1, 2 * reps))
    sin_l = jnp.tile(jnp.concatenate([-sin, sin], axis=1), (1, reps))
    return _with_ctx_rows(cos_l, sin_l, jnp.ones((1, LANES), F32), ctx_len, LOG2E * HEAD_DIM ** -0.5)


def _tables_b(seq, ctx_len):
    ang = _axial_angles(seq // GRID_W, B_ROPE)
    cos, sin = jnp.cos(ang), jnp.sin(ang)
    copy = LANES - B_NOPE - B_ROPE
    cos_l = jnp.concatenate([jnp.ones((seq, B_NOPE), F32), cos, cos, jnp.zeros((seq, copy), F32)], axis=1)
    sin_l = jnp.concatenate([jnp.zeros((seq, B_NOPE), F32), -sin, sin, jnp.zeros((seq, copy), F32)], axis=1)
    ctx_cos = jnp.concatenate([jnp.ones((1, B_NOPE + B_ROPE), F32), jnp.zeros((1, copy), F32)], axis=1)
    return _with_ctx_rows(cos_l, sin_l, ctx_cos, ctx_len, LOG2E * (B_NOPE + B_ROPE) ** -0.5)


def _deinterleave(n):
    return jnp.concatenate([jnp.arange(0, n, 2), jnp.arange(1, n, 2)])


def _weights_a(wqkv):
    d = wqkv.shape[0]
    kw = A_KV_HEADS * HEAD_DIM
    qw = wqkv.shape[1] - 2 * kw
    perm = _deinterleave(HEAD_DIM)
    wq = wqkv[:, :qw].reshape(d, -1, HEAD_DIM)[:, :, perm].reshape(d, qw)
    wk = wqkv[:, qw:qw + kw].reshape(d, A_KV_HEADS, HEAD_DIM)[:, :, perm]
    wkk = jnp.concatenate([wk, wk], axis=2).reshape(d, 2 * kw)
    wv = wqkv[:, qw + kw:].reshape(d, A_KV_HEADS, HEAD_DIM)
    wv = jnp.concatenate([wv, jnp.zeros((d, A_KV_HEADS, LANES - HEAD_DIM), F32)], axis=2)
    wvt = wv.reshape(d, A_KV_HEADS * LANES).T
    return jnp.concatenate([wq, wkk], axis=1).astype(BF16), wvt.astype(BF16), qw, 2 * kw


def _weights_b(wdown, wuq, wuk, wuv, q_lora, kv_lora):
    d = wdown.shape[0]
    perm = _deinterleave(B_ROPE)
    assert LANES - B_NOPE - B_ROPE == B_ROPE
    w_rope = wdown[:, q_lora + kv_lora:][:, perm]
    w_rope = jnp.concatenate([jnp.zeros((d, B_NOPE), F32), w_rope, w_rope], axis=1)
    wd = jnp.concatenate([wdown[:, :q_lora + kv_lora], w_rope], axis=1)
    uq = wuq.reshape(q_lora, B_HEADS, B_NOPE + B_ROPE)
    uq_rope = uq[:, :, B_NOPE:][:, :, perm]
    uq = jnp.concatenate([uq[:, :, :B_NOPE], uq_rope, uq_rope], axis=2)
    uk = wuk.reshape(kv_lora, B_HEADS, B_NOPE)
    uk = jnp.concatenate([uk, jnp.zeros((kv_lora, B_HEADS, LANES - B_NOPE), F32)], axis=2)
    uv = wuv.reshape(kv_lora, B_HEADS, B_V)
    uv = jnp.concatenate([uv, jnp.zeros((kv_lora, B_HEADS, LANES - B_V), F32)], axis=2)
    flat = lambda w: w.reshape(w.shape[0], B_HEADS * LANES).astype(BF16)
    return wd.astype(BF16), flat(uq), flat(uk), flat(uv).T


def kernel(x, c, ctx, c_ctx, mod_w, mod_b, norm1_g, norm2_g, a_wqkv, a_wo, a_sink, b_wdown, b_qnorm_g, b_wuq,
           b_kvnorm_g, b_wuk, b_wuv, b_wo, f_win, f_conv_w, f_conv_b, f_wout, final_g):
    bsz, seq, d = x.shape
    ctx_len = ctx.shape[1]
    depth = mod_w.shape[0]
    d_ff = f_wout.shape[1]
    assert ctx_len == TM and seq % TK_MLA == 0 and seq % TM_CHANNEL == 0 and seq % GRID_W == 0

    cond_rows = -(-(bsz + 1) // SUBLANES) * SUBLANES
    cond = jnp.concatenate([c, c_ctx[None], jnp.zeros((cond_rows - bsz - 1, d), F32)], axis=0)
    mod = _modulation(cond, mod_w, mod_b)

    tab_a = _tables_a(seq, ctx_len)
    tab_b = _tables_b(seq, ctx_len)
    xs = None
    row = lambda v: v.reshape(1, -1)

    for i in range(depth):
        last = i == depth - 1
        lat = mod[i, :bsz].reshape(bsz, 6, d)
        cmod = jnp.broadcast_to(mod[i, bsz].reshape(1, 6, d), (bsz, 6, d))
        modl = jnp.stack([cmod, lat], axis=1)
        j = i // 2
        if i % 2 == 0:
            w, wvt, qw, kw = _weights_a(a_wqkv[j])
            if i == 0:
                xs, q, k, vt = _proj_a((ctx, x), modl, row(norm1_g[i]), w, wvt, tab_a, qw, kw)
            else:
                q, k, vt = _proj_a(xs, modl, row(norm1_g[i]), w, wvt, tab_a, qw, kw)
            o = _attn_a(q, k, vt, a_sink[j], seq, not last)
            wo = a_wo[j]
        else:
            q_lora, kv_lora = b_qnorm_g.shape[1], b_kvnorm_g.shape[1]
            wd, wuq, wuk, wuvt = _weights_b(b_wdown[j], b_wuq[j], b_wuk[j], b_wuv[j], q_lora, kv_lora)
            q, k, vt = _proj_b(xs, modl, row(norm1_g[i]), wd, row(b_qnorm_g[j]), row(b_kvnorm_g[j]),
                               wuq, wuk, wuvt, tab_b)
            o = _attn_b(q, k, vt, seq, not last)
            wo = b_wo[j]
        xs = _channel(xs, o, modl, row(norm2_g[i]), wo.astype(BF16), f_win[i][:, :d_ff].astype(BF16),
                      f_win[i][:, d_ff:].astype(BF16), f_conv_w[i], row(f_conv_b[i]), f_wout[i].astype(BF16),
                      row(final_g), seq, last)
    return xs
```

```python
import functools

import jax
import jax.numpy as jnp
from jax import lax
from jax.experimental import pallas as pl
from jax.experimental.pallas import tpu as pltpu

GRID_W = 64
HEAD_DIM = 64
A_KV_HEADS = 4
WINDOW = 128
B_HEADS = 16
B_NOPE = 64
B_ROPE = 32
B_V = 64
ROPE_BASE = 10000.0
EPS = 1e-6
NEG = -1e30
LOG2E = 1.4426950408889634

LANES = 128
SUBLANES = 8
V7X_VMEM_BYTES = 64 * 1024 * 1024
VMEM_LIMIT = V7X_VMEM_BYTES * 7 // 8
TM = 256
HALO = 16
TK_MLA = 1024
MLA_PAIRS = 4
GQA_LAG = 3
MAX_FOLD = 64
TM_CHANNEL = 512
FF_CHUNK = 768

F32 = jnp.float32
BF16 = jnp.bfloat16
_NT = (((1,), (1,)), ((), ()))


def _dot(a, b):
    return jnp.dot(a, b, preferred_element_type=F32)


def _dot_nt(a, b):
    return lax.dot_general(a, b, _NT, preferred_element_type=F32)


def _rms(xf, g):
    ms = jnp.mean(xf * xf, axis=-1, keepdims=True)
    return xf * lax.rsqrt(ms + EPS) * g


def _norm_mod(xf, g, shift, scale):
    return _rms(xf, g) * (1.0 + scale) + shift


def _params(n_axes):
    return pltpu.CompilerParams(dimension_semantics=("parallel",) * n_axes,
                                vmem_limit_bytes=VMEM_LIMIT)


def _resident(shape):
    nd = len(shape)
    return pl.BlockSpec(shape, lambda *_: (0,) * nd, pipeline_mode=pl.Buffered(1))


def _mod_kernel(c_ref, w_ref, b_ref, o_ref):
    c = c_ref[...]
    silu = c * (1.0 / (1.0 + jnp.exp(-c)))
    o_ref[0] = _dot(silu.astype(BF16), w_ref[0].astype(BF16)) + b_ref[0]


def _modulation(cond, mod_w, mod_b):
    depth, d, d6 = mod_w.shape
    rows = cond.shape[0]
    return pl.pallas_call(
        _mod_kernel,
        grid=(depth, d6 // d),
        in_specs=[pl.BlockSpec((rows, d), lambda i, j: (0, 0)),
                  pl.BlockSpec((1, d, d), lambda i, j: (i, 0, j)),
                  pl.BlockSpec((1, 1, d), lambda i, j: (i, 0, j))],
        out_specs=pl.BlockSpec((1, rows, d), lambda i, j: (i, 0, j)),
        out_shape=jax.ShapeDtypeStruct((depth, rows, d6), F32),
        compiler_params=_params(2),
        name="modulation",
    )(cond, mod_w, mod_b.reshape(depth, 1, d6))


def _rope_block(xb, cos, sin_signed, first_half, half):
    swapped = jnp.where(first_half, pltpu.roll(xb, LANES - half, 1), pltpu.roll(xb, half, 1))
    return xb * cos + swapped * sin_signed


def _rope_block_dup(xb, cos, sin_signed, half):
    return xb * cos + pltpu.roll(xb, LANES - half, 1) * sin_signed


def _proj_a_kernel(x_ref, mod_ref, g_ref, w_ref, wvt_ref, tab_ref, q_ref, k_ref, vt_ref, *, qw, kw):
    m = mod_ref[0, 0]
    h = _norm_mod(x_ref[0], g_ref[...], m[0:1], m[1:2]).astype(BF16)
    qk = _dot(h, w_ref[...])
    lane = lax.broadcasted_iota(jnp.int32, (TM, LANES), 1)
    first = (lane % HEAD_DIM) < (HEAD_DIM // 2)
    cq, sq, ck, sk = tab_ref[0], tab_ref[1], tab_ref[2], tab_ref[3]
    for j in range(qw // LANES):
        blk = qk[:, j * LANES:(j + 1) * LANES]
        q_ref[0, :, j * LANES:(j + 1) * LANES] = _rope_block(blk, cq, sq, first, HEAD_DIM // 2).astype(BF16)
    for j in range(kw // LANES):
        blk = qk[:, qw + j * LANES:qw + (j + 1) * LANES]
        k_ref[0, :, j * LANES:(j + 1) * LANES] = _rope_block_dup(blk, ck, sk, HEAD_DIM // 2).astype(BF16)
    vrow = lax.broadcasted_iota(jnp.int32, (wvt_ref.shape[0], 1), 0) % LANES
    vt_ref[0] = (_dot_nt(wvt_ref[...], h) + jnp.where(vrow == HEAD_DIM, 1.0, 0.0)).astype(BF16)


def _proj_a_first_kernel(ctx_ref, lat_ref, mod_ref, g_ref, w_ref, wvt_ref, tab_ref, xs_ref, q_ref, k_ref, vt_ref,
                         *, ctx_tiles, **kw):
    xs_ref[0] = jnp.where(pl.program_id(1) < ctx_tiles, ctx_ref[0], lat_ref[0])
    _proj_a_kernel(xs_ref, mod_ref, g_ref, w_ref, wvt_ref, tab_ref, q_ref, k_ref, vt_ref, **kw)


def _proj_a(xs, modl, g, w, wvt, tab, qw, kw):
    first = isinstance(xs, tuple)
    if first:
        ctx, lat = xs
        b, ctx_len, d = ctx.shape
        ctx_tiles = ctx_len // TM
        s_tot = ctx_len + lat.shape[1]
        x_args = (ctx, lat)
        x_specs = [pl.BlockSpec((1, TM, d), lambda i, t: (i, jnp.minimum(t, ctx_tiles - 1), 0)),
                   pl.BlockSpec((1, TM, d), lambda i, t: (i, jnp.maximum(t - ctx_tiles, 0), 0))]
        body = functools.partial(_proj_a_first_kernel, ctx_tiles=ctx_tiles, qw=qw, kw=kw)
    else:
        b, s_tot, d = xs.shape
        x_args = (xs,)
        x_specs = [pl.BlockSpec((1, TM, d), lambda i, t: (i, t, 0))]
        body = functools.partial(_proj_a_kernel, qw=qw, kw=kw)
    nt = s_tot // TM
    nv = wvt.shape[0]
    out = lambda n: jax.ShapeDtypeStruct((b, s_tot, n), BF16)
    ospec = lambda n: pl.BlockSpec((1, TM, n), lambda i, t: (i, t, 0))
    out_specs = [ospec(qw), ospec(kw), pl.BlockSpec((1, nv, TM), lambda i, t: (i, 0, t))]
    out_shape = [out(qw), out(kw), jax.ShapeDtypeStruct((b, nv, s_tot), BF16)]
    if first:
        out_specs.insert(0, ospec(d))
        out_shape.insert(0, jax.ShapeDtypeStruct((b, s_tot, d), F32))
    return pl.pallas_call(
        body,
        grid=(b, nt),
        in_specs=x_specs + [pl.BlockSpec((1, 1, 6, d), lambda i, t: (i, jnp.minimum(t, 1), 0, 0)),
                            _resident(g.shape), _resident(w.shape), _resident(wvt.shape),
                            pl.BlockSpec((4, TM, LANES), lambda i, t: (0, t, 0))],
        out_specs=out_specs,
        out_shape=out_shape,
        compiler_params=_params(2),
        name="proj_gqa",
    )(*x_args, modl, g, w, wvt, tab)


def _proj_b_kernel(x_ref, mod_ref, g_ref, wd_ref, gq_ref, gkv_ref, wuq_ref, wuk_ref, wuvt_ref, tab_ref,
                   q_ref, k_ref, vt_ref, *, q_lora, kv_lora):
    m = mod_ref[0, 0]
    h = _norm_mod(x_ref[0], g_ref[...], m[0:1], m[1:2]).astype(BF16)
    d = _dot(h, wd_ref[...])
    cq = _rms(d[:, :q_lora], gq_ref[...]).astype(BF16)
    ckv = _rms(d[:, q_lora:q_lora + kv_lora], gkv_ref[...]).astype(BF16)
    tq_c, tq_s, tk_c, tk_s = tab_ref[0], tab_ref[1], tab_ref[2], tab_ref[3]
    k_rope = _rope_block_dup(d[:, q_lora + kv_lora:], tk_c, tk_s, B_ROPE // 2)
    q = _dot(cq, wuq_ref[...])
    k = _dot(ckv, wuk_ref[...])
    for j in range(B_HEADS):
        sl = slice(j * LANES, (j + 1) * LANES)
        q_ref[0, :, sl] = _rope_block_dup(q[:, sl], tq_c, tq_s, B_ROPE // 2).astype(BF16)
        k_ref[0, :, sl] = (k[:, sl] + k_rope).astype(BF16)
    vrow = lax.broadcasted_iota(jnp.int32, (wuvt_ref.shape[0], 1), 0) % LANES
    vt_ref[0] = (_dot_nt(wuvt_ref[...], ckv) + jnp.where(vrow == B_V, 1.0, 0.0)).astype(BF16)


def _proj_b(xs, modl, g, wd, gq, gkv, wuq, wuk, wuvt, tab):
    b, s_tot, d = xs.shape
    nt = s_tot // TM
    out = lambda n: jax.ShapeDtypeStruct((b, s_tot, n), BF16)
    ospec = lambda n: pl.BlockSpec((1, TM, n), lambda i, t: (i, t, 0))
    nq, nv = wuq.shape[1], wuvt.shape[0]
    return pl.pallas_call(
        functools.partial(_proj_b_kernel, q_lora=gq.shape[1], kv_lora=gkv.shape[1]),
        grid=(b, nt),
        in_specs=[pl.BlockSpec((1, TM, d), lambda i, t: (i, t, 0)),
                  pl.BlockSpec((1, 1, 6, d), lambda i, t: (i, jnp.minimum(t, 1), 0, 0)),
                  _resident(g.shape), _resident(wd.shape), _resident(gq.shape), _resident(gkv.shape),
                  _resident(wuq.shape), _resident(wuk.shape), _resident(wuvt.shape),
                  pl.BlockSpec((4, TM, LANES), lambda i, t: (0, t, 0))],
        out_specs=[ospec(nq), ospec(nq), pl.BlockSpec((1, nv, TM), lambda i, t: (i, 0, t))],
        out_shape=[out(nq), out(nq), jax.ShapeDtypeStruct((b, nv, s_tot), BF16)],
        compiler_params=_params(2),
        name="proj_mla",
    )(xs, modl, g, wd, gq, gkv, wuq, wuk, wuvt, tab)


def _attn_a_kernel(sink_ref, q_ref, kc_ref, kp_ref, kt_ref, kn_ref, vc_ref, vp_ref, vt_ref, vn_ref, o_ref,
                   *, ctx_tiles, n_tiles, ctx_out):
    t = pl.program_id(1)
    assert TM == 2 * WINDOW
    edge = WINDOW
    n_heads = q_ref.shape[2] // HEAD_DIM
    group = n_heads // A_KV_HEADS
    lane = lax.broadcasted_iota(jnp.int32, (1, LANES), 1)
    lo = lane < HEAD_DIM
    zero = jnp.zeros((), BF16)

    key = lax.broadcasted_iota(jnp.int32, (edge, edge), 0)
    qry = lax.broadcasted_iota(jnp.int32, (edge, edge), 1)
    tri = lambda valid: jnp.where(valid, 0.0, NEG).astype(F32)
    ALL, NONE = "all", "none"
    ctx_layout = [(ALL, ALL)] * (kc_ref.shape[1] // edge)
    win_layout = [(tri((qry <= key) & (t > ctx_tiles)), NONE),
                  (ALL, tri(qry <= key)), (tri(qry >= key), ALL),
                  (NONE, tri((qry >= key) & (t < n_tiles - 1)))]

    def attend(layout, k_of, v_of):
        def score(h):
            qp = q_ref[0, :, (h // 2) * LANES:(h // 2 + 1) * LANES]
            qe = jnp.where(lo, qp, zero) if h % 2 == 0 else jnp.where(lo, zero, qp)
            s = _dot_nt(k_of(h // group), qe)
            sink = sink_ref[h] * LOG2E
            halves = []
            for half in range(2):
                blocks, run = [], None
                for rb, kinds in enumerate(layout):
                    kind = kinds[half]
                    if isinstance(kind, str) and kind == NONE:
                        blocks.append(None)
                        continue
                    blk = s[rb * edge:(rb + 1) * edge, half * edge:(half + 1) * edge]
                    if not isinstance(kind, str):
                        blk = blk + kind
                    blocks.append(blk)
                    for i in range(edge // MAX_FOLD):
                        part = blk[i * MAX_FOLD:(i + 1) * MAX_FOLD]
                        run = part if run is None else jnp.maximum(run, part)
                halves.append((blocks, jnp.maximum(jnp.max(run, axis=0, keepdims=True), sink)))
            return halves, sink

        def probs(st):
            halves, sink = st
            rows = []
            for rb in range(len(layout)):
                parts = [jnp.zeros((edge, edge), BF16) if blocks[rb] is None
                         else jnp.exp2(blocks[rb] - mx).astype(BF16) for blocks, mx in halves]
                rows.append(jnp.concatenate(parts, axis=1))
            p_sink = jnp.concatenate([jnp.exp2(sink - mx) for _, mx in halves], axis=1)
            return jnp.concatenate(rows, axis=0), p_sink

        def values(h, pr):
            p, p_sink = pr
            acc = _dot(v_of(h // group), p)
            return acc[:HEAD_DIM] * (1.0 / (acc[HEAD_DIM:HEAD_DIM + 1] + p_sink))

        scored, weighted, outs = {}, {}, {}
        for step in range(n_heads + 2 * GQA_LAG):
            h = step - 2 * GQA_LAG
            if 0 <= h < n_heads:
                outs[h] = values(h, weighted.pop(h))
                if h % 2 == 1:
                    pair_t = jnp.concatenate([outs.pop(h - 1), outs.pop(h)], axis=0)
                    o_ref[0, :, (h // 2) * LANES:(h // 2 + 1) * LANES] = pair_t.T.astype(BF16)
            if 0 <= step - GQA_LAG < n_heads:
                weighted[step - GQA_LAG] = probs(scored.pop(step - GQA_LAG))
            if step < n_heads:
                scored[step] = score(step)

    def with_kv(body, k_refs, v_refs):
        ks, vs = [], []
        for j in range(A_KV_HEADS):
            sl = slice(j * LANES, (j + 1) * LANES)
            ks.append(jnp.concatenate([r[0, :, sl] for r in k_refs], axis=0))
            vs.append(jnp.concatenate([r[0, sl, :] for r in v_refs], axis=1))
        body(ks.__getitem__, vs.__getitem__)

    @pl.when(t < ctx_tiles)
    def _():
        if ctx_out:
            with_kv(functools.partial(attend, ctx_layout), [kc_ref], [vc_ref])
        else:
            o_ref[...] = jnp.zeros(o_ref.shape, o_ref.dtype)

    @pl.when(t >= ctx_tiles)
    def _():
        with_kv(functools.partial(attend, ctx_layout + win_layout),
                [kc_ref, kp_ref, kt_ref, kn_ref], [vc_ref, vp_ref, vt_ref, vn_ref])


def _attn_a(q, k, vt, sink, seq, with_ctx):
    b, s_tot, qw = q.shape
    kw, nv = k.shape[2], vt.shape[1]
    ctx_tiles = (s_tot - seq) // TM
    n_tiles = s_tot // TM
    hb = TM // WINDOW
    n_half = s_tot // WINDOW
    prev = lambda t: jnp.maximum(t * hb - 1, 0)
    nxt = lambda t: jnp.minimum((t + 1) * hb, n_half - 1)
    k_specs = [pl.BlockSpec((1, TM, kw), lambda i, t: (i, 0, 0)),
               pl.BlockSpec((1, WINDOW, kw), lambda i, t: (i, prev(t), 0)),
               pl.BlockSpec((1, TM, kw), lambda i, t: (i, t, 0)),
               pl.BlockSpec((1, WINDOW, kw), lambda i, t: (i, nxt(t), 0))]
    v_specs = [pl.BlockSpec((1, nv, TM), lambda i, t: (i, 0, 0)),
               pl.BlockSpec((1, nv, WINDOW), lambda i, t: (i, 0, prev(t))),
               pl.BlockSpec((1, nv, TM), lambda i, t: (i, 0, t)),
               pl.BlockSpec((1, nv, WINDOW), lambda i, t: (i, 0, nxt(t)))]
    return pl.pallas_call(
        functools.partial(_attn_a_kernel, ctx_tiles=ctx_tiles, n_tiles=n_tiles, ctx_out=with_ctx),
        grid=(b, n_tiles),
        in_specs=[pl.BlockSpec(memory_space=pltpu.SMEM),
                  pl.BlockSpec((1, TM, qw), lambda i, t: (i, t, 0))] + k_specs + v_specs,
        out_specs=pl.BlockSpec((1, TM, qw), lambda i, t: (i, t, 0)),
        out_shape=jax.ShapeDtypeStruct((b, s_tot, qw), BF16),
        compiler_params=_params(2),
        name="attn_gqa",
    )(sink, q, k, k, k, k, vt, vt, vt, vt)


def _attn_b_kernel(q_ref, qn_ref, k_ref, vt_ref, o_ref, s_ref, run_ref, *, ctx_len, seq, n_tiles, ctx_out):
    t = pl.program_id(2)
    ctx_tiles = ctx_len // TM
    n_pairs = q_ref.shape[2] // (2 * LANES)
    slot = lambda h: h % s_ref.shape[0]
    assert (2 * n_pairs) % s_ref.shape[0] == 0

    def score_chunk(h, q_src, k0, tk, run):
        hs = slice(h * LANES, (h + 1) * LANES)
        s = _dot_nt(k_ref[0, k0:k0 + tk, hs], q_src[0, :, hs])
        s_ref[slot(h), k0:k0 + tk, :] = s
        for i in range(tk // MAX_FOLD):
            run = jnp.maximum(run, s[i * MAX_FOLD:(i + 1) * MAX_FOLD])
        return run

    def value_chunk(h, k0, tk, mx, acc):
        p = jnp.exp2(s_ref[slot(h), k0:k0 + tk, :] - mx).astype(BF16)
        c = _dot(vt_ref[0, h * LANES:(h + 1) * LANES, k0:k0 + tk], p)
        return c if acc is None else acc + c

    def attend(chunks, first_scored, score_next):
        init = jnp.full((MAX_FOLD, TM), NEG, F32)
        run = [run_ref[0], run_ref[1]] if first_scored else None
        for pr in range(1 if first_scored else 0, n_pairs + 1):
            mx = None if run is None else [jnp.max(r, axis=0, keepdims=True) for r in run]
            run, acc = [init, init], [None, None]
            for k0, tk in chunks:
                for e in range(2):
                    if pr < n_pairs:
                        run[e] = score_chunk(2 * pr + e, q_ref, k0, tk, run[e])
                    elif score_next:
                        run[e] = score_chunk(e, qn_ref, k0, tk, run[e])
                    if pr > 0:
                        acc[e] = value_chunk(2 * (pr - 1) + e, k0, tk, mx[e], acc[e])
            if pr > 0:
                out_t = jnp.concatenate([a[:B_V] * (1.0 / a[B_V:B_V + 1]) for a in acc], axis=0)
                o_ref[0, :, (pr - 1) * LANES:pr * LANES] = out_t.T.astype(BF16)
        if score_next:
            run_ref[0], run_ref[1] = run

    ctx_chunk = [(0, ctx_len)]
    all_chunks = ctx_chunk + [(ctx_len + i * TK_MLA, TK_MLA) for i in range(seq // TK_MLA)]

    @pl.when(t < ctx_tiles)
    def _():
        if ctx_out:
            attend(ctx_chunk, False, False)
        else:
            o_ref[...] = jnp.zeros(o_ref.shape, o_ref.dtype)

    pl.when(t == ctx_tiles)(lambda: attend(all_chunks, False, ctx_tiles < n_tiles - 1))
    pl.when((t > ctx_tiles) & (t < n_tiles - 1))(lambda: attend(all_chunks, True, True))
    pl.when((t > ctx_tiles) & (t == n_tiles - 1))(lambda: attend(all_chunks, True, False))


def _attn_b(q, k, vt, seq, with_ctx):
    b, s_tot, qw = q.shape
    ctx_len = s_tot - seq
    n_tiles = s_tot // TM
    heads = 2 * MLA_PAIRS
    wide = heads * LANES
    pair = 2 * LANES
    return pl.pallas_call(
        functools.partial(_attn_b_kernel, ctx_len=ctx_len, seq=seq, n_tiles=n_tiles, ctx_out=with_ctx),
        grid=(b, qw // wide, n_tiles),
        in_specs=[pl.BlockSpec((1, TM, wide), lambda i, h, t: (i, t, h)),
                  pl.BlockSpec((1, TM, pair), lambda i, h, t: (i, jnp.minimum(t + 1, n_tiles - 1), h * MLA_PAIRS)),
                  pl.BlockSpec((1, s_tot, wide), lambda i, h, t: (i, 0, h)),
                  pl.BlockSpec((1, wide, s_tot), lambda i, h, t: (i, h, 0))],
        out_specs=pl.BlockSpec((1, TM, heads * B_V), lambda i, h, t: (i, t, h)),
        out_shape=jax.ShapeDtypeStruct((b, s_tot, B_HEADS * B_V), BF16),
        scratch_shapes=[pltpu.VMEM((min(heads, 4), s_tot, TM), F32), pltpu.VMEM((2, MAX_FOLD, TM), F32)],
        compiler_params=pltpu.CompilerParams(dimension_semantics=("parallel", "parallel", "arbitrary"),
                                             vmem_limit_bytes=VMEM_LIMIT),
        name="attn_mla",
    )(q, q, k, vt)


def _channel_kernel(*refs, rows, seg_tiles, final):
    (xp_ref, xt_ref, xn_ref, op_ref, ot_ref, on_ref, mod_ref, g_ref, wo_ref, wa_ref, wv_ref,
     cw_ref, cb_ref, wout_ref, gf_ref), y_ref = refs[:15], refs[-1]
    t = pl.program_id(1)
    ext = rows + 2 * HALO
    mid = slice(HALO, HALO + rows)
    m = mod_ref[0, 0]
    x_ext = jnp.concatenate([xp_ref[0], xt_ref[0], xn_ref[0]], axis=0)
    o_ext = jnp.concatenate([op_ref[0], ot_ref[0], on_ref[0]], axis=0)
    x1 = x_ext + m[2:3] * _dot(o_ext, wo_ref[...])
    h2 = _norm_mod(x1, g_ref[...], m[3:4], m[4:5])
    row = lax.broadcasted_iota(jnp.int32, (ext, 1), 0)
    keep = ((row >= HALO) | (t > 0)) & ((row < HALO + rows) | (t < seg_tiles - 1))
    h2 = jnp.where(keep, h2, 0.0).astype(BF16)
    h2_mid = h2[mid]
    cw = cw_ref[...]
    cb = cb_ref[...]
    d_ff = wa_ref.shape[2]
    y = None
    chunks = [slice(c0, min(c0 + FF_CHUNK, d_ff)) for c0 in range(0, d_ff, FF_CHUNK)]
    branches = lambda cs: (_dot(h2, wa_ref[0, :, cs]), _dot(h2_mid, wv_ref[0, :, cs]))
    ahead = branches(chunks[0])
    for i, cs in enumerate(chunks):
        a, v = ahead
        if i + 1 < len(chunks):
            ahead = branches(chunks[i + 1])
        a_prev = pltpu.roll(a, 1, 0)[mid]
        a_next = pltpu.roll(a, ext - 1, 0)[mid]
        conv = a_prev * cw[0:1, cs] + a[mid] * cw[1:2, cs] + a_next * cw[2:3, cs] + cb[:, cs]
        gate = conv * (1.0 / (1.0 + jnp.exp(-conv)))
        hid = (gate * v).astype(BF16)
        part = _dot(hid, wout_ref[0, cs, :])
        y = part if y is None else y + part
    x2 = x1[mid] + m[5:6] * y
    y_ref[0] = _rms(x2, gf_ref[...]) if final else x2


def _rows_at(n_rows, width, offset):
    return pl.BlockSpec((pl.Element(1), pl.Element(n_rows), pl.Element(width)),
                        lambda i, t: (i, pl.multiple_of(offset(t), HALO), 0))


def _channel(xs, o, modl, g, wo, w_in, cw, cb, w_out, gf, layer, seq, final):
    b, s_tot, d = xs.shape
    ctx_len = s_tot - seq
    d_ff = w_out.shape[1]
    weights = (g, wo, w_in, w_in, cw, cb, w_out, gf)
    of_layer = lambda shape, col: pl.BlockSpec((1,) + shape, lambda *_: (layer, 0, col), pipeline_mode=pl.Buffered(1))
    w_specs = [_resident(g.shape), _resident(wo.shape), of_layer((d, d_ff), 0), of_layer((d, d_ff), 1),
               _resident(cw.shape), _resident(cb.shape), of_layer((d_ff, d), 0), _resident(gf.shape)]

    def call(rows, base, seg_rows, mod_row, out_spec, out_rows, alias):
        seg_tiles = seg_rows // rows
        prev = lambda t: jnp.maximum(base + t * rows - HALO, 0)
        cur = lambda t: base + t * rows
        nxt = lambda t: jnp.minimum(base + (t + 1) * rows, base + seg_rows - HALO)
        tiles = lambda w: [_rows_at(HALO, w, prev), _rows_at(rows, w, cur), _rows_at(HALO, w, nxt)]
        in_specs = (tiles(d) + tiles(o.shape[2])
                    + [pl.BlockSpec((1, 1, 6, d), lambda i, t: (i, mod_row, 0, 0))] + w_specs)
        args = (xs, xs, xs, o, o, o, modl) + weights
        if alias is not None:
            in_specs.append(pl.BlockSpec(memory_space=pl.ANY))
            args += (alias,)
        return pl.pallas_call(
            functools.partial(_channel_kernel, rows=rows, seg_tiles=seg_tiles, final=final),
            grid=(b, seg_tiles),
            in_specs=in_specs,
            out_specs=out_spec,
            out_shape=jax.ShapeDtypeStruct((b, out_rows, d), F32),
            input_output_aliases={} if alias is None else {len(args) - 1: 0},
            compiler_params=_params(2),
            name="channel",
        )(*args)

    if final:
        return call(TM_CHANNEL, ctx_len, seq, 1, pl.BlockSpec((1, TM_CHANNEL, d), lambda i, t: (i, t, 0)), seq, None)
    y = call(TM_CHANNEL, ctx_len, seq, 1, _rows_at(TM_CHANNEL, d, lambda t: ctx_len + t * TM_CHANNEL), s_tot, None)
    return call(TM, 0, ctx_len, 0, pl.BlockSpec((1, TM, d), lambda i, t: (i, t, 0)), s_tot, y)


def _axial_angles(rows, rot_dim):
    row = jnp.repeat(jnp.arange(rows), GRID_W).astype(F32)
    col = jnp.tile(jnp.arange(GRID_W), rows).astype(F32)
    n_freq = rot_dim // 4
    inv = ROPE_BASE ** (-jnp.arange(n_freq, dtype=F32) / n_freq)
    return jnp.concatenate([row[:, None] * inv, col[:, None] * inv], axis=-1)


def _with_ctx_rows(cos, sin, ctx_cos, ctx_len, q_scale):
    cos = jnp.concatenate([jnp.broadcast_to(ctx_cos, (ctx_len, LANES)), cos], axis=0)
    sin = jnp.concatenate([jnp.zeros((ctx_len, LANES), F32), sin], axis=0)
    return jnp.stack([cos * q_scale, sin * q_scale, cos, sin])


def _tables_a(seq, ctx_len):
    ang = _axial_angles(seq // GRID_W, HEAD_DIM)
    cos, sin = jnp.cos(ang), jnp.sin(ang)
    reps = LANES // HEAD_DIM
    cos_l = jnp.tile(cos, (1, 2 * reps))
    sin_l = jnp.tile(jnp.concatenate([-sin, sin], axis=1), (1, reps))
    return _with_ctx_rows(cos_l, sin_l, jnp.ones((1, LANES), F32), ctx_len, LOG2E * HEAD_DIM ** -0.5)


def _tables_b(seq, ctx_len):
    ang = _axial_angles(seq // GRID_W, B_ROPE)
    cos, sin = jnp.cos(ang), jnp.sin(ang)
    copy = LANES - B_NOPE - B_ROPE
    cos_l = jnp.concatenate([jnp.ones((seq, B_NOPE), F32), cos, cos, jnp.zeros((seq, copy), F32)], axis=1)
    sin_l = jnp.concatenate([jnp.zeros((seq, B_NOPE), F32), -sin, sin, jnp.zeros((seq, copy), F32)], axis=1)
    ctx_cos = jnp.concatenate([jnp.ones((1, B_NOPE + B_ROPE), F32), jnp.zeros((1, copy), F32)], axis=1)
    return _with_ctx_rows(cos_l, sin_l, ctx_cos, ctx_len, LOG2E * (B_NOPE + B_ROPE) ** -0.5)


def _deinterleave(n):
    return jnp.concatenate([jnp.arange(0, n, 2), jnp.arange(1, n, 2)])


def _weights_a(wqkv):
    d = wqkv.shape[0]
    kw = A_KV_HEADS * HEAD_DIM
    qw = wqkv.shape[1] - 2 * kw
    perm = _deinterleave(HEAD_DIM)
    wq = wqkv[:, :qw].reshape(d, -1, HEAD_DIM)[:, :, perm].reshape(d, qw)
    wk = wqkv[:, qw:qw + kw].reshape(d, A_KV_HEADS, HEAD_DIM)[:, :, perm]
    wkk = jnp.concatenate([wk, wk], axis=2).reshape(d, 2 * kw)
    wv = wqkv[:, qw + kw:].reshape(d, A_KV_HEADS, HEAD_DIM)
    wv = jnp.concatenate([wv, jnp.zeros((d, A_KV_HEADS, LANES - HEAD_DIM), F32)], axis=2)
    wvt = wv.reshape(d, A_KV_HEADS * LANES).T
    return jnp.concatenate([wq, wkk], axis=1).astype(BF16), wvt.astype(BF16), qw, 2 * kw


def _weights_b(wdown, wuq, wuk, wuv, q_lora, kv_lora):
    d = wdown.shape[0]
    perm = _deinterleave(B_ROPE)
    assert LANES - B_NOPE - B_ROPE == B_ROPE
    w_rope = wdown[:, q_lora + kv_lora:][:, perm]
    w_rope = jnp.concatenate([jnp.zeros((d, B_NOPE), F32), w_rope, w_rope], axis=1)
    wd = jnp.concatenate([wdown[:, :q_lora + kv_lora], w_rope], axis=1)
    uq = wuq.reshape(q_lora, B_HEADS, B_NOPE + B_ROPE)
    uq_rope = uq[:, :, B_NOPE:][:, :, perm]
    uq = jnp.concatenate([uq[:, :, :B_NOPE], uq_rope, uq_rope], axis=2)
    uk = wuk.reshape(kv_lora, B_HEADS, B_NOPE)
    uk = jnp.concatenate([uk, jnp.zeros((kv_lora, B_HEADS, LANES - B_NOPE), F32)], axis=2)
    uv = wuv.reshape(kv_lora, B_HEADS, B_V)
    uv = jnp.concatenate([uv, jnp.zeros((kv_lora, B_HEADS, LANES - B_V), F32)], axis=2)
    flat = lambda w: w.reshape(w.shape[0], B_HEADS * LANES).astype(BF16)
    return wd.astype(BF16), flat(uq), flat(uk), flat(uv).T


def kernel(x, c, ctx, c_ctx, mod_w, mod_b, norm1_g, norm2_g, a_wqkv, a_wo, a_sink, b_wdown, b_qnorm_g, b_wuq,
           b_kvnorm_g, b_wuk, b_wuv, b_wo, f_win, f_conv_w, f_conv_b, f_wout, final_g):
    bsz, seq, d = x.shape
    ctx_len = ctx.shape[1]
    depth = mod_w.shape[0]
    assert ctx_len == TM and seq % TK_MLA == 0 and seq % TM_CHANNEL == 0 and seq % GRID_W == 0

    cond_rows = -(-(bsz + 1) // SUBLANES) * SUBLANES
    cond = jnp.concatenate([c, c_ctx[None], jnp.zeros((cond_rows - bsz - 1, d), F32)], axis=0)
    mod = _modulation(cond, mod_w, mod_b)

    tab_a = _tables_a(seq, ctx_len)
    tab_b = _tables_b(seq, ctx_len)
    xs = None
    w_in, w_out = f_win.astype(BF16), f_wout.astype(BF16)
    row = lambda v: v.reshape(1, -1)

    for i in range(depth):
        last = i == depth - 1
        lat = mod[i, :bsz].reshape(bsz, 6, d)
        cmod = jnp.broadcast_to(mod[i, bsz].reshape(1, 6, d), (bsz, 6, d))
        modl = jnp.stack([cmod, lat], axis=1)
        j = i // 2
        if i % 2 == 0:
            w, wvt, qw, kw = _weights_a(a_wqkv[j])
            if i == 0:
                xs, q, k, vt = _proj_a((ctx, x), modl, row(norm1_g[i]), w, wvt, tab_a, qw, kw)
            else:
                q, k, vt = _proj_a(xs, modl, row(norm1_g[i]), w, wvt, tab_a, qw, kw)
            o = _attn_a(q, k, vt, a_sink[j], seq, not last)
            wo = a_wo[j]
        else:
            q_lora, kv_lora = b_qnorm_g.shape[1], b_kvnorm_g.shape[1]
            wd, wuq, wuk, wuvt = _weights_b(b_wdown[j], b_wuq[j], b_wuk[j], b_wuv[j], q_lora, kv_lora)
            q, k, vt = _proj_b(xs, modl, row(norm1_g[i]), wd, row(b_qnorm_g[j]), row(b_kvnorm_g[j]),
                               wuq, wuk, wuvt, tab_b)
            o = _attn_b(q, k, vt, seq, not last)
            wo = b_wo[j]
        xs = _channel(xs, o, modl, row(norm2_g[i]), wo.astype(BF16), w_in, f_conv_w[i], row(f_conv_b[i]), w_out,
                      row(final_g), i, seq, last)
    return xs
```

```python
import functools

import jax
import jax.numpy as jnp
from jax import lax
from jax.experimental import pallas as pl
from jax.experimental.pallas import tpu as pltpu

GRID_W = 64
HEAD_DIM = 64
A_KV_HEADS = 4
WINDOW = 128
B_HEADS = 16
B_NOPE = 64
B_ROPE = 32
B_V = 64
ROPE_BASE = 10000.0
EPS = 1e-6
NEG = -1e30
LOG2E = 1.4426950408889634

LANES = 128
SUBLANES = 8
V7X_VMEM_BYTES = 64 * 1024 * 1024
VMEM_LIMIT = V7X_VMEM_BYTES * 7 // 8
TM = 256
HALO = 16
TK_MLA = 1024
MLA_PAIRS = 4
GQA_LAG = 3
MAX_FOLD = 64
TM_CHANNEL = 512
FF_CHUNK = 768

F32 = jnp.float32
BF16 = jnp.bfloat16
_NT = (((1,), (1,)), ((), ()))


def _dot(a, b):
    return jnp.dot(a, b, preferred_element_type=F32)


def _dot_nt(a, b):
    return lax.dot_general(a, b, _NT, preferred_element_type=F32)


def _rms(xf, g):
    ms = jnp.mean(xf * xf, axis=-1, keepdims=True)
    return xf * lax.rsqrt(ms + EPS) * g


def _norm_mod(xf, g, shift, scale):
    return _rms(xf, g) * (1.0 + scale) + shift


def _params(n_axes):
    return pltpu.CompilerParams(dimension_semantics=("parallel",) * n_axes,
                                vmem_limit_bytes=VMEM_LIMIT)


def _resident(shape):
    nd = len(shape)
    return pl.BlockSpec(shape, lambda *_: (0,) * nd, pipeline_mode=pl.Buffered(1))


def _mod_kernel(c_ref, w_ref, b_ref, o_ref):
    c = c_ref[...]
    silu = c * (1.0 / (1.0 + jnp.exp(-c)))
    o_ref[0] = _dot(silu.astype(BF16), w_ref[0].astype(BF16)) + b_ref[0]


def _modulation(cond, mod_w, mod_b):
    depth, d, d6 = mod_w.shape
    rows = cond.shape[0]
    return pl.pallas_call(
        _mod_kernel,
        grid=(depth, d6 // d),
        in_specs=[pl.BlockSpec((rows, d), lambda i, j: (0, 0)),
                  pl.BlockSpec((1, d, d), lambda i, j: (i, 0, j)),
                  pl.BlockSpec((1, 1, d), lambda i, j: (i, 0, j))],
        out_specs=pl.BlockSpec((1, rows, d), lambda i, j: (i, 0, j)),
        out_shape=jax.ShapeDtypeStruct((depth, rows, d6), F32),
        compiler_params=_params(2),
        name="modulation",
    )(cond, mod_w, mod_b.reshape(depth, 1, d6))


def _rope_block(xb, cos, sin_signed, first_half, half):
    swapped = jnp.where(first_half, pltpu.roll(xb, LANES - half, 1), pltpu.roll(xb, half, 1))
    return xb * cos + swapped * sin_signed


def _rope_block_dup(xb, cos, sin_signed, half):
    return xb * cos + pltpu.roll(xb, LANES - half, 1) * sin_signed


def _stream_tile(ctx_ref, lat_ref, ctx_tiles):
    return jnp.where(pl.program_id(1) < ctx_tiles, ctx_ref[0], lat_ref[0])


def _stream_specs(d, ctx_tiles):
    return [pl.BlockSpec((1, TM, d), lambda i, t: (i, jnp.minimum(t, ctx_tiles - 1), 0)),
            pl.BlockSpec((1, TM, d), lambda i, t: (i, jnp.maximum(t - ctx_tiles, 0), 0))]


def _proj_a_kernel(ctx_ref, lat_ref, mod_ref, g_ref, w_ref, wvt_ref, tab_ref, q_ref, k_ref, vt_ref, *, ctx_tiles, qw, kw):
    m = mod_ref[0, 0]
    x = _stream_tile(ctx_ref, lat_ref, ctx_tiles)
    h = _norm_mod(x, g_ref[...], m[0:1], m[1:2]).astype(BF16)
    qk = _dot(h, w_ref[...])
    lane = lax.broadcasted_iota(jnp.int32, (TM, LANES), 1)
    first = (lane % HEAD_DIM) < (HEAD_DIM // 2)
    cq, sq, ck, sk = tab_ref[0], tab_ref[1], tab_ref[2], tab_ref[3]
    for j in range(qw // LANES):
        blk = qk[:, j * LANES:(j + 1) * LANES]
        q_ref[0, :, j * LANES:(j + 1) * LANES] = _rope_block(blk, cq, sq, first, HEAD_DIM // 2).astype(BF16)
    for j in range(kw // LANES):
        blk = qk[:, qw + j * LANES:qw + (j + 1) * LANES]
        k_ref[0, :, j * LANES:(j + 1) * LANES] = _rope_block_dup(blk, ck, sk, HEAD_DIM // 2).astype(BF16)
    vrow = lax.broadcasted_iota(jnp.int32, (wvt_ref.shape[0], 1), 0) % LANES
    vt_ref[0] = (_dot_nt(wvt_ref[...], h) + jnp.where(vrow == HEAD_DIM, 1.0, 0.0)).astype(BF16)


def _proj_a(x_ctx, x_lat, modl, g, w, wvt, tab, qw, kw):
    b, ctx_len, d = x_ctx.shape
    ctx_tiles = ctx_len // TM
    s_tot = ctx_len + x_lat.shape[1]
    nv = wvt.shape[0]
    out = lambda n: jax.ShapeDtypeStruct((b, s_tot, n), BF16)
    ospec = lambda n: pl.BlockSpec((1, TM, n), lambda i, t: (i, t, 0))
    return pl.pallas_call(
        functools.partial(_proj_a_kernel, ctx_tiles=ctx_tiles, qw=qw, kw=kw),
        grid=(b, s_tot // TM),
        in_specs=_stream_specs(d, ctx_tiles) + [pl.BlockSpec((1, 1, 6, d), lambda i, t: (i, jnp.minimum(t, 1), 0, 0)),
                                                _resident(g.shape), _resident(w.shape), _resident(wvt.shape),
                                                pl.BlockSpec((4, TM, LANES), lambda i, t: (0, t, 0))],
        out_specs=[ospec(qw), ospec(kw), pl.BlockSpec((1, nv, TM), lambda i, t: (i, 0, t))],
        out_shape=[out(qw), out(kw), jax.ShapeDtypeStruct((b, nv, s_tot), BF16)],
        compiler_params=_params(2),
        name="proj_gqa",
    )(x_ctx, x_lat, modl, g, w, wvt, tab)


def _proj_b_kernel(ctx_ref, lat_ref, mod_ref, g_ref, wd_ref, gq_ref, gkv_ref, wuq_ref, wuk_ref, wuvt_ref, tab_ref,
                   q_ref, k_ref, vt_ref, *, ctx_tiles, q_lora, kv_lora):
    m = mod_ref[0, 0]
    x = _stream_tile(ctx_ref, lat_ref, ctx_tiles)
    h = _norm_mod(x, g_ref[...], m[0:1], m[1:2]).astype(BF16)
    d = _dot(h, wd_ref[...])
    cq = _rms(d[:, :q_lora], gq_ref[...]).astype(BF16)
    ckv = _rms(d[:, q_lora:q_lora + kv_lora], gkv_ref[...]).astype(BF16)
    tq_c, tq_s, tk_c, tk_s = tab_ref[0], tab_ref[1], tab_ref[2], tab_ref[3]
    k_rope = _rope_block_dup(d[:, q_lora + kv_lora:], tk_c, tk_s, B_ROPE // 2)
    q = _dot(cq, wuq_ref[...])
    k = _dot(ckv, wuk_ref[...])
    for j in range(B_HEADS):
        sl = slice(j * LANES, (j + 1) * LANES)
        q_ref[0, :, sl] = _rope_block_dup(q[:, sl], tq_c, tq_s, B_ROPE // 2).astype(BF16)
        k_ref[0, :, sl] = (k[:, sl] + k_rope).astype(BF16)
    vrow = lax.broadcasted_iota(jnp.int32, (wuvt_ref.shape[0], 1), 0) % LANES
    vt_ref[0] = (_dot_nt(wuvt_ref[...], ckv) + jnp.where(vrow == B_V, 1.0, 0.0)).astype(BF16)


def _proj_b(x_ctx, x_lat, modl, g, wd, gq, gkv, wuq, wuk, wuvt, tab):
    b, ctx_len, d = x_ctx.shape
    ctx_tiles = ctx_len // TM
    s_tot = ctx_len + x_lat.shape[1]
    out = lambda n: jax.ShapeDtypeStruct((b, s_tot, n), BF16)
    ospec = lambda n: pl.BlockSpec((1, TM, n), lambda i, t: (i, t, 0))
    nq, nv = wuq.shape[1], wuvt.shape[0]
    return pl.pallas_call(
        functools.partial(_proj_b_kernel, ctx_tiles=ctx_tiles, q_lora=gq.shape[1], kv_lora=gkv.shape[1]),
        grid=(b, s_tot // TM),
        in_specs=_stream_specs(d, ctx_tiles) + [
            pl.BlockSpec((1, 1, 6, d), lambda i, t: (i, jnp.minimum(t, 1), 0, 0)),
            _resident(g.shape), _resident(wd.shape), _resident(gq.shape), _resident(gkv.shape),
            _resident(wuq.shape), _resident(wuk.shape), _resident(wuvt.shape),
            pl.BlockSpec((4, TM, LANES), lambda i, t: (0, t, 0))],
        out_specs=[ospec(nq), ospec(nq), pl.BlockSpec((1, nv, TM), lambda i, t: (i, 0, t))],
        out_shape=[out(nq), out(nq), jax.ShapeDtypeStruct((b, nv, s_tot), BF16)],
        compiler_params=_params(2),
        name="proj_mla",
    )(x_ctx, x_lat, modl, g, wd, gq, gkv, wuq, wuk, wuvt, tab)


def _attn_a_kernel(sink_ref, q_ref, kc_ref, kp_ref, kt_ref, kn_ref, vc_ref, vp_ref, vt_ref, vn_ref, o_ref,
                   *, ctx_tiles, n_tiles, ctx_out):
    t = pl.program_id(1)
    assert TM == 2 * WINDOW
    edge = WINDOW
    n_heads = q_ref.shape[2] // HEAD_DIM
    group = n_heads // A_KV_HEADS
    lane = lax.broadcasted_iota(jnp.int32, (1, LANES), 1)
    lo = lane < HEAD_DIM
    zero = jnp.zeros((), BF16)

    key = lax.broadcasted_iota(jnp.int32, (edge, edge), 0)
    qry = lax.broadcasted_iota(jnp.int32, (edge, edge), 1)
    tri = lambda valid: jnp.where(valid, 0.0, NEG).astype(F32)
    ALL, NONE = "all", "none"
    ctx_layout = [(ALL, ALL)] * (kc_ref.shape[1] // edge)
    win_layout = [(tri((qry <= key) & (t > ctx_tiles)), NONE),
                  (ALL, tri(qry <= key)), (tri(qry >= key), ALL),
                  (NONE, tri((qry >= key) & (t < n_tiles - 1)))]

    def attend(layout, k_of, v_of):
        def score(h):
            qp = q_ref[0, :, (h // 2) * LANES:(h // 2 + 1) * LANES]
            qe = jnp.where(lo, qp, zero) if h % 2 == 0 else jnp.where(lo, zero, qp)
            s = _dot_nt(k_of(h // group), qe)
            sink = sink_ref[h] * LOG2E
            halves = []
            for half in range(2):
                blocks, run = [], None
                for rb, kinds in enumerate(layout):
                    kind = kinds[half]
                    if isinstance(kind, str) and kind == NONE:
                        blocks.append(None)
                        continue
                    blk = s[rb * edge:(rb + 1) * edge, half * edge:(half + 1) * edge]
                    if not isinstance(kind, str):
                        blk = blk + kind
                    blocks.append(blk)
                    for i in range(edge // MAX_FOLD):
                        part = blk[i * MAX_FOLD:(i + 1) * MAX_FOLD]
                        run = part if run is None else jnp.maximum(run, part)
                halves.append((blocks, jnp.maximum(jnp.max(run, axis=0, keepdims=True), sink)))
            return halves, sink

        def probs(st):
            halves, sink = st
            rows = []
            for rb in range(len(layout)):
                parts = [jnp.zeros((edge, edge), BF16) if blocks[rb] is None
                         else jnp.exp2(blocks[rb] - mx).astype(BF16) for blocks, mx in halves]
                rows.append(jnp.concatenate(parts, axis=1))
            p_sink = jnp.concatenate([jnp.exp2(sink - mx) for _, mx in halves], axis=1)
            return jnp.concatenate(rows, axis=0), p_sink

        def values(h, pr):
            p, p_sink = pr
            acc = _dot(v_of(h // group), p)
            return acc[:HEAD_DIM] * (1.0 / (acc[HEAD_DIM:HEAD_DIM + 1] + p_sink))

        scored, weighted, outs = {}, {}, {}
        for step in range(n_heads + 2 * GQA_LAG):
            h = step - 2 * GQA_LAG
            if 0 <= h < n_heads:
                outs[h] = values(h, weighted.pop(h))
                if h % 2 == 1:
                    pair_t = jnp.concatenate([outs.pop(h - 1), outs.pop(h)], axis=0)
                    o_ref[0, :, (h // 2) * LANES:(h // 2 + 1) * LANES] = pair_t.T.astype(BF16)
            if 0 <= step - GQA_LAG < n_heads:
                weighted[step - GQA_LAG] = probs(scored.pop(step - GQA_LAG))
            if step < n_heads:
                scored[step] = score(step)

    def with_kv(body, k_refs, v_refs):
        ks, vs = [], []
        for j in range(A_KV_HEADS):
            sl = slice(j * LANES, (j + 1) * LANES)
            ks.append(jnp.concatenate([r[0, :, sl] for r in k_refs], axis=0))
            vs.append(jnp.concatenate([r[0, sl, :] for r in v_refs], axis=1))
        body(ks.__getitem__, vs.__getitem__)

    @pl.when(t < ctx_tiles)
    def _():
        if ctx_out:
            with_kv(functools.partial(attend, ctx_layout), [kc_ref], [vc_ref])
        else:
            o_ref[...] = jnp.zeros(o_ref.shape, o_ref.dtype)

    @pl.when(t >= ctx_tiles)
    def _():
        with_kv(functools.partial(attend, ctx_layout + win_layout),
                [kc_ref, kp_ref, kt_ref, kn_ref], [vc_ref, vp_ref, vt_ref, vn_ref])


def _attn_a(q, k, vt, sink, seq, with_ctx):
    b, s_tot, qw = q.shape
    kw, nv = k.shape[2], vt.shape[1]
    ctx_tiles = (s_tot - seq) // TM
    n_tiles = s_tot // TM
    hb = TM // WINDOW
    n_half = s_tot // WINDOW
    prev = lambda t: jnp.maximum(t * hb - 1, 0)
    nxt = lambda t: jnp.minimum((t + 1) * hb, n_half - 1)
    k_specs = [pl.BlockSpec((1, TM, kw), lambda i, t: (i, 0, 0)),
               pl.BlockSpec((1, WINDOW, kw), lambda i, t: (i, prev(t), 0)),
               pl.BlockSpec((1, TM, kw), lambda i, t: (i, t, 0)),
               pl.BlockSpec((1, WINDOW, kw), lambda i, t: (i, nxt(t), 0))]
    v_specs = [pl.BlockSpec((1, nv, TM), lambda i, t: (i, 0, 0)),
               pl.BlockSpec((1, nv, WINDOW), lambda i, t: (i, 0, prev(t))),
               pl.BlockSpec((1, nv, TM), lambda i, t: (i, 0, t)),
               pl.BlockSpec((1, nv, WINDOW), lambda i, t: (i, 0, nxt(t)))]
    return pl.pallas_call(
        functools.partial(_attn_a_kernel, ctx_tiles=ctx_tiles, n_tiles=n_tiles, ctx_out=with_ctx),
        grid=(b, n_tiles),
        in_specs=[pl.BlockSpec(memory_space=pltpu.SMEM),
                  pl.BlockSpec((1, TM, qw), lambda i, t: (i, t, 0))] + k_specs + v_specs,
        out_specs=pl.BlockSpec((1, TM, qw), lambda i, t: (i, t, 0)),
        out_shape=jax.ShapeDtypeStruct((b, s_tot, qw), BF16),
        compiler_params=_params(2),
        name="attn_gqa",
    )(sink, q, k, k, k, k, vt, vt, vt, vt)


def _attn_b_kernel(q_ref, qn_ref, k_ref, vt_ref, o_ref, s_ref, run_ref, *, ctx_len, seq, n_tiles, ctx_out):
    t = pl.program_id(2)
    ctx_tiles = ctx_len // TM
    n_pairs = q_ref.shape[2] // (2 * LANES)
    slot = lambda h: h % s_ref.shape[0]
    assert (2 * n_pairs) % s_ref.shape[0] == 0

    def score_chunk(h, q_src, k0, tk, run):
        hs = slice(h * LANES, (h + 1) * LANES)
        s = _dot_nt(k_ref[0, k0:k0 + tk, hs], q_src[0, :, hs])
        s_ref[slot(h), k0:k0 + tk, :] = s
        for i in range(tk // MAX_FOLD):
            run = jnp.maximum(run, s[i * MAX_FOLD:(i + 1) * MAX_FOLD])
        return run

    def value_chunk(h, k0, tk, mx, acc):
        p = jnp.exp2(s_ref[slot(h), k0:k0 + tk, :] - mx).astype(BF16)
        c = _dot(vt_ref[0, h * LANES:(h + 1) * LANES, k0:k0 + tk], p)
        return c if acc is None else acc + c

    def attend(chunks, first_scored, score_next):
        init = jnp.full((MAX_FOLD, TM), NEG, F32)
        run = [run_ref[0], run_ref[1]] if first_scored else None
        for pr in range(1 if first_scored else 0, n_pairs + 1):
            mx = None if run is None else [jnp.max(r, axis=0, keepdims=True) for r in run]
            run, acc = [init, init], [None, None]
            for k0, tk in chunks:
                for e in range(2):
                    if pr < n_pairs:
                        run[e] = score_chunk(2 * pr + e, q_ref, k0, tk, run[e])
                    elif score_next:
                        run[e] = score_chunk(e, qn_ref, k0, tk, run[e])
                    if pr > 0:
                        acc[e] = value_chunk(2 * (pr - 1) + e, k0, tk, mx[e], acc[e])
            if pr > 0:
                out_t = jnp.concatenate([a[:B_V] * (1.0 / a[B_V:B_V + 1]) for a in acc], axis=0)
                o_ref[0, :, (pr - 1) * LANES:pr * LANES] = out_t.T.astype(BF16)
        if score_next:
            run_ref[0], run_ref[1] = run

    ctx_chunk = [(0, ctx_len)]
    all_chunks = ctx_chunk + [(ctx_len + i * TK_MLA, TK_MLA) for i in range(seq // TK_MLA)]

    @pl.when(t < ctx_tiles)
    def _():
        if ctx_out:
            attend(ctx_chunk, False, False)
        else:
            o_ref[...] = jnp.zeros(o_ref.shape, o_ref.dtype)

    pl.when(t == ctx_tiles)(lambda: attend(all_chunks, False, ctx_tiles < n_tiles - 1))
    pl.when((t > ctx_tiles) & (t < n_tiles - 1))(lambda: attend(all_chunks, True, True))
    pl.when((t > ctx_tiles) & (t == n_tiles - 1))(lambda: attend(all_chunks, True, False))


def _attn_b(q, k, vt, seq, with_ctx):
    b, s_tot, qw = q.shape
    ctx_len = s_tot - seq
    n_tiles = s_tot // TM
    heads = 2 * MLA_PAIRS
    wide = heads * LANES
    pair = 2 * LANES
    return pl.pallas_call(
        functools.partial(_attn_b_kernel, ctx_len=ctx_len, seq=seq, n_tiles=n_tiles, ctx_out=with_ctx),
        grid=(b, qw // wide, n_tiles),
        in_specs=[pl.BlockSpec((1, TM, wide), lambda i, h, t: (i, t, h)),
                  pl.BlockSpec((1, TM, pair), lambda i, h, t: (i, jnp.minimum(t + 1, n_tiles - 1), h * MLA_PAIRS)),
                  pl.BlockSpec((1, s_tot, wide), lambda i, h, t: (i, 0, h)),
                  pl.BlockSpec((1, wide, s_tot), lambda i, h, t: (i, h, 0))],
        out_specs=pl.BlockSpec((1, TM, heads * B_V), lambda i, h, t: (i, t, h)),
        out_shape=jax.ShapeDtypeStruct((b, s_tot, B_HEADS * B_V), BF16),
        scratch_shapes=[pltpu.VMEM((min(heads, 4), s_tot, TM), F32), pltpu.VMEM((2, MAX_FOLD, TM), F32)],
        compiler_params=pltpu.CompilerParams(dimension_semantics=("parallel", "parallel", "arbitrary"),
                                             vmem_limit_bytes=VMEM_LIMIT),
        name="attn_mla",
    )(q, q, k, vt)


def _channel_kernel(*refs, rows, seg_tiles, final):
    (xp_ref, xt_ref, xn_ref, op_ref, ot_ref, on_ref, mod_ref, g_ref, wo_ref, wa_ref, wv_ref,
     cw_ref, cb_ref, wout_ref, gf_ref, y_ref) = refs
    t = pl.program_id(1)
    ext = rows + 2 * HALO
    mid = slice(HALO, HALO + rows)
    m = mod_ref[0, 0]
    x_ext = jnp.concatenate([xp_ref[0], xt_ref[0], xn_ref[0]], axis=0)
    o_ext = jnp.concatenate([op_ref[0], ot_ref[0], on_ref[0]], axis=0)
    x1 = x_ext + m[2:3] * _dot(o_ext, wo_ref[...])
    h2 = _norm_mod(x1, g_ref[...], m[3:4], m[4:5])
    row = lax.broadcasted_iota(jnp.int32, (ext, 1), 0)
    keep = ((row >= HALO) | (t > 0)) & ((row < HALO + rows) | (t < seg_tiles - 1))
    h2 = jnp.where(keep, h2, 0.0).astype(BF16)
    h2_mid = h2[mid]
    cw = cw_ref[...]
    cb = cb_ref[...]
    d_ff = wa_ref.shape[2]
    y = None
    chunks = [slice(c0, min(c0 + FF_CHUNK, d_ff)) for c0 in range(0, d_ff, FF_CHUNK)]
    branches = lambda cs: (_dot(h2, wa_ref[0, :, cs]), _dot(h2_mid, wv_ref[0, :, cs]))
    ahead = branches(chunks[0])
    for i, cs in enumerate(chunks):
        a, v = ahead
        if i + 1 < len(chunks):
            ahead = branches(chunks[i + 1])
        a_prev = pltpu.roll(a, 1, 0)[mid]
        a_next = pltpu.roll(a, ext - 1, 0)[mid]
        conv = a_prev * cw[0:1, cs] + a[mid] * cw[1:2, cs] + a_next * cw[2:3, cs] + cb[:, cs]
        gate = conv * (1.0 / (1.0 + jnp.exp(-conv)))
        hid = (gate * v).astype(BF16)
        part = _dot(hid, wout_ref[0, cs, :])
        y = part if y is None else y + part
    x2 = x1[mid] + m[5:6] * y
    y_ref[0] = _rms(x2, gf_ref[...]) if final else x2


def _rows_at(n_rows, width, offset):
    return pl.BlockSpec((pl.Element(1), pl.Element(n_rows), pl.Element(width)),
                        lambda i, t: (i, pl.multiple_of(offset(t), HALO), 0))


def _channel(x_ctx, x_lat, o, modl, g, wo, w_in, cw, cb, w_out, gf, layer, final):
    b, ctx_len, d = x_ctx.shape
    d_ff = w_out.shape[1]
    weights = (g, wo, w_in, w_in, cw, cb, w_out, gf)
    of_layer = lambda shape, col: pl.BlockSpec((1,) + shape, lambda *_: (layer, 0, col), pipeline_mode=pl.Buffered(1))
    w_specs = [_resident(g.shape), _resident(wo.shape), of_layer((d, d_ff), 0), of_layer((d, d_ff), 1),
               _resident(cw.shape), _resident(cb.shape), of_layer((d_ff, d), 0), _resident(gf.shape)]

    def call(x, rows, o_base, mod_row):
        seg_rows = x.shape[1]
        seg_tiles = seg_rows // rows

        def tiles(width, base):
            prev = lambda t: base + jnp.maximum(t * rows - HALO, 0)
            cur = lambda t: base + t * rows
            nxt = lambda t: base + jnp.minimum((t + 1) * rows, seg_rows - HALO)
            return [_rows_at(HALO, width, prev), _rows_at(rows, width, cur), _rows_at(HALO, width, nxt)]

        return pl.pallas_call(
            functools.partial(_channel_kernel, rows=rows, seg_tiles=seg_tiles, final=final),
            grid=(b, seg_tiles),
            in_specs=(tiles(d, 0) + tiles(o.shape[2], o_base)
                      + [pl.BlockSpec((1, 1, 6, d), lambda i, t: (i, mod_row, 0, 0))] + w_specs),
            out_specs=pl.BlockSpec((1, rows, d), lambda i, t: (i, t, 0)),
            out_shape=jax.ShapeDtypeStruct((b, seg_rows, d), F32),
            compiler_params=_params(2),
            name="channel",
        )(x, x, x, o, o, o, modl, *weights)

    y_lat = call(x_lat, TM_CHANNEL, ctx_len, 1)
    return (None if final else call(x_ctx, TM, 0, 0)), y_lat


def _axial_angles(rows, rot_dim):
    row = jnp.repeat(jnp.arange(rows), GRID_W).astype(F32)
    col = jnp.tile(jnp.arange(GRID_W), rows).astype(F32)
    n_freq = rot_dim // 4
    inv = ROPE_BASE ** (-jnp.arange(n_freq, dtype=F32) / n_freq)
    return jnp.concatenate([row[:, None] * inv, col[:, None] * inv], axis=-1)


def _with_ctx_rows(cos, sin, ctx_cos, ctx_len, q_scale):
    cos = jnp.concatenate([jnp.broadcast_to(ctx_cos, (ctx_len, LANES)), cos], axis=0)
    sin = jnp.concatenate([jnp.zeros((ctx_len, LANES), F32), sin], axis=0)
    return jnp.stack([cos * q_scale, sin * q_scale, cos, sin])


def _tables_a(seq, ctx_len):
    ang = _axial_angles(seq // GRID_W, HEAD_DIM)
    cos, sin = jnp.cos(ang), jnp.sin(ang)
    reps = LANES // HEAD_DIM
    cos_l = jnp.tile(cos, (1, 2 * reps))
    sin_l = jnp.tile(jnp.concatenate([-sin, sin], axis=1), (1, reps))
    return _with_ctx_rows(cos_l, sin_l, jnp.ones((1, LANES), F32), ctx_len, LOG2E * HEAD_DIM ** -0.5)


def _tables_b(seq, ctx_len):
    ang = _axial_angles(seq // GRID_W, B_ROPE)
    cos, sin = jnp.cos(ang), jnp.sin(ang)
    copy = LANES - B_NOPE - B_ROPE
    cos_l = jnp.concatenate([jnp.ones((seq, B_NOPE), F32), cos, cos, jnp.zeros((seq, copy), F32)], axis=1)
    sin_l = jnp.concatenate([jnp.zeros((seq, B_NOPE), F32), -sin, sin, jnp.zeros((seq, copy), F32)], axis=1)
    ctx_cos = jnp.concatenate([jnp.ones((1, B_NOPE + B_ROPE), F32), jnp.zeros((1, copy), F32)], axis=1)
    return _with_ctx_rows(cos_l, sin_l, ctx_cos, ctx_len, LOG2E * (B_NOPE + B_ROPE) ** -0.5)


def _deinterleave(n):
    return jnp.concatenate([jnp.arange(0, n, 2), jnp.arange(1, n, 2)])


def _weights_a(wqkv):
    d = wqkv.shape[0]
    kw = A_KV_HEADS * HEAD_DIM
    qw = wqkv.shape[1] - 2 * kw
    perm = _deinterleave(HEAD_DIM)
    wq = wqkv[:, :qw].reshape(d, -1, HEAD_DIM)[:, :, perm].reshape(d, qw)
    wk = wqkv[:, qw:qw + kw].reshape(d, A_KV_HEADS, HEAD_DIM)[:, :, perm]
    wkk = jnp.concatenate([wk, wk], axis=2).reshape(d, 2 * kw)
    wv = wqkv[:, qw + kw:].reshape(d, A_KV_HEADS, HEAD_DIM)
    wv = jnp.concatenate([wv, jnp.zeros((d, A_KV_HEADS, LANES - HEAD_DIM), F32)], axis=2)
    wvt = wv.reshape(d, A_KV_HEADS * LANES).T
    return jnp.concatenate([wq, wkk], axis=1).astype(BF16), wvt.astype(BF16), qw, 2 * kw


def _weights_b(wdown, wuq, wuk, wuv, q_lora, kv_lora):
    d = wdown.shape[0]
    perm = _deinterleave(B_ROPE)
    assert LANES - B_NOPE - B_ROPE == B_ROPE
    w_rope = wdown[:, q_lora + kv_lora:][:, perm]
    w_rope = jnp.concatenate([jnp.zeros((d, B_NOPE), F32), w_rope, w_rope], axis=1)
    wd = jnp.concatenate([wdown[:, :q_lora + kv_lora], w_rope], axis=1)
    uq = wuq.reshape(q_lora, B_HEADS, B_NOPE + B_ROPE)
    uq_rope = uq[:, :, B_NOPE:][:, :, perm]
    uq = jnp.concatenate([uq[:, :, :B_NOPE], uq_rope, uq_rope], axis=2)
    uk = wuk.reshape(kv_lora, B_HEADS, B_NOPE)
    uk = jnp.concatenate([uk, jnp.zeros((kv_lora, B_HEADS, LANES - B_NOPE), F32)], axis=2)
    uv = wuv.reshape(kv_lora, B_HEADS, B_V)
    uv = jnp.concatenate([uv, jnp.zeros((kv_lora, B_HEADS, LANES - B_V), F32)], axis=2)
    flat = lambda w: w.reshape(w.shape[0], B_HEADS * LANES).astype(BF16)
    return wd.astype(BF16), flat(uq), flat(uk), flat(uv).T


def kernel(x, c, ctx, c_ctx, mod_w, mod_b, norm1_g, norm2_g, a_wqkv, a_wo, a_sink, b_wdown, b_qnorm_g, b_wuq,
           b_kvnorm_g, b_wuk, b_wuv, b_wo, f_win, f_conv_w, f_conv_b, f_wout, final_g):
    bsz, seq, d = x.shape
    ctx_len = ctx.shape[1]
    depth = mod_w.shape[0]
    assert ctx_len == TM and seq % TK_MLA == 0 and seq % TM_CHANNEL == 0 and seq % GRID_W == 0

    cond_rows = -(-(bsz + 1) // SUBLANES) * SUBLANES
    cond = jnp.concatenate([c, c_ctx[None], jnp.zeros((cond_rows - bsz - 1, d), F32)], axis=0)
    mod = _modulation(cond, mod_w, mod_b)

    tab_a = _tables_a(seq, ctx_len)
    tab_b = _tables_b(seq, ctx_len)
    x_ctx, x_lat = ctx, x
    w_in, w_out = f_win.astype(BF16), f_wout.astype(BF16)
    row = lambda v: v.reshape(1, -1)

    for i in range(depth):
        last = i == depth - 1
        lat = mod[i, :bsz].reshape(bsz, 6, d)
        cmod = jnp.broadcast_to(mod[i, bsz].reshape(1, 6, d), (bsz, 6, d))
        modl = jnp.stack([cmod, lat], axis=1)
        j = i // 2
        if i % 2 == 0:
            w, wvt, qw, kw = _weights_a(a_wqkv[j])
            q, k, vt = _proj_a(x_ctx, x_lat, modl, row(norm1_g[i]), w, wvt, tab_a, qw, kw)
            o = _attn_a(q, k, vt, a_sink[j], seq, not last)
            wo = a_wo[j]
        else:
            q_lora, kv_lora = b_qnorm_g.shape[1], b_kvnorm_g.shape[1]
            wd, wuq, wuk, wuvt = _weights_b(b_wdown[j], b_wuq[j], b_wuk[j], b_wuv[j], q_lora, kv_lora)
            q, k, vt = _proj_b(x_ctx, x_lat, modl, row(norm1_g[i]), wd, row(b_qnorm_g[j]), row(b_kvnorm_g[j]),
                               wuq, wuk, wuvt, tab_b)
            o = _attn_b(q, k, vt, seq, not last)
            wo = b_wo[j]
        x_ctx, x_lat = _channel(x_ctx, x_lat, o, modl, row(norm2_g[i]), wo.astype(BF16), w_in, f_conv_w[i],
                                row(f_conv_b[i]), w_out, row(final_g), i, last)
    return x_lat
```

```python
import functools

import jax
import jax.numpy as jnp
from jax import lax
from jax.experimental import pallas as pl
from jax.experimental.pallas import tpu as pltpu

GRID_W = 64
HEAD_DIM = 64
A_KV_HEADS = 4
WINDOW = 128
B_HEADS = 16
B_NOPE = 64
B_ROPE = 32
B_V = 64
ROPE_BASE = 10000.0
EPS = 1e-6
NEG = -1e30
LOG2E = 1.4426950408889634

LANES = 128
SUBLANES = 8
V7X_VMEM_BYTES = 64 * 1024 * 1024
VMEM_LIMIT = V7X_VMEM_BYTES * 7 // 8
TM = 256
HALO = 16
TK_MLA = 1024
MLA_PAIRS = 4
GQA_LAG = 3
MAX_FOLD = 64
TM_CHANNEL = 512
FF_CHUNK = 768

F32 = jnp.float32
BF16 = jnp.bfloat16
_NT = (((1,), (1,)), ((), ()))


def _dot(a, b):
    return jnp.dot(a, b, preferred_element_type=F32)


def _dot_nt(a, b):
    return lax.dot_general(a, b, _NT, preferred_element_type=F32)


def _rms(xf, g):
    ms = jnp.mean(xf * xf, axis=-1, keepdims=True)
    return xf * lax.rsqrt(ms + EPS) * g


def _norm_mod(xf, g, shift, scale):
    return _rms(xf, g) * (1.0 + scale) + shift


def _params(n_axes):
    return pltpu.CompilerParams(dimension_semantics=("parallel",) * n_axes,
                                vmem_limit_bytes=VMEM_LIMIT)


def _resident(shape):
    nd = len(shape)
    return pl.BlockSpec(shape, lambda *_: (0,) * nd, pipeline_mode=pl.Buffered(1))


def _mod_kernel(c_ref, w_ref, b_ref, o_ref):
    c = c_ref[...]
    silu = c * (1.0 / (1.0 + jnp.exp(-c)))
    o_ref[0] = _dot(silu.astype(BF16), w_ref[0].astype(BF16)) + b_ref[0]


def _modulation(cond, mod_w, mod_b):
    depth, d, d6 = mod_w.shape
    rows = cond.shape[0]
    return pl.pallas_call(
        _mod_kernel,
        grid=(depth, d6 // d),
        in_specs=[pl.BlockSpec((rows, d), lambda i, j: (0, 0)),
                  pl.BlockSpec((1, d, d), lambda i, j: (i, 0, j)),
                  pl.BlockSpec((1, 1, d), lambda i, j: (i, 0, j))],
        out_specs=pl.BlockSpec((1, rows, d), lambda i, j: (i, 0, j)),
        out_shape=jax.ShapeDtypeStruct((depth, rows, d6), F32),
        compiler_params=_params(2),
        name="modulation",
    )(cond, mod_w, mod_b.reshape(depth, 1, d6))


def _rope_block(xb, cos, sin_signed, first_half, half):
    swapped = jnp.where(first_half, pltpu.roll(xb, LANES - half, 1), pltpu.roll(xb, half, 1))
    return xb * cos + swapped * sin_signed


def _rope_block_dup(xb, cos, sin_signed, half):
    return xb * cos + pltpu.roll(xb, LANES - half, 1) * sin_signed


def _on_stream_tile(ctx_ref, lat_ref, ctx_tiles, body):
    t = pl.program_id(1)
    pl.when(t < ctx_tiles)(lambda: body(ctx_ref))
    pl.when(t >= ctx_tiles)(lambda: body(lat_ref))


def _stream_specs(d, ctx_tiles):
    return [pl.BlockSpec((1, TM, d), lambda i, t: (i, jnp.minimum(t, ctx_tiles - 1), 0)),
            pl.BlockSpec((1, TM, d), lambda i, t: (i, jnp.maximum(t - ctx_tiles, 0), 0))]


def _proj_a_kernel(ctx_ref, lat_ref, mod_ref, g_ref, w_ref, wvt_ref, tab_ref, q_ref, k_ref, vt_ref, *, ctx_tiles, qw, kw):
    def project(x_ref):
        m = mod_ref[0, 0]
        h = _norm_mod(x_ref[0], g_ref[...], m[0:1], m[1:2]).astype(BF16)
        qk = _dot(h, w_ref[...])
        lane = lax.broadcasted_iota(jnp.int32, (TM, LANES), 1)
        first = (lane % HEAD_DIM) < (HEAD_DIM // 2)
        cq, sq, ck, sk = tab_ref[0], tab_ref[1], tab_ref[2], tab_ref[3]
        for j in range(qw // LANES):
            blk = qk[:, j * LANES:(j + 1) * LANES]
            q_ref[0, :, j * LANES:(j + 1) * LANES] = _rope_block(blk, cq, sq, first, HEAD_DIM // 2).astype(BF16)
        for j in range(kw // LANES):
            blk = qk[:, qw + j * LANES:qw + (j + 1) * LANES]
            k_ref[0, :, j * LANES:(j + 1) * LANES] = _rope_block_dup(blk, ck, sk, HEAD_DIM // 2).astype(BF16)
        vrow = lax.broadcasted_iota(jnp.int32, (wvt_ref.shape[0], 1), 0) % LANES
        vt_ref[0] = (_dot_nt(wvt_ref[...], h) + jnp.where(vrow == HEAD_DIM, 1.0, 0.0)).astype(BF16)

    _on_stream_tile(ctx_ref, lat_ref, ctx_tiles, project)


def _proj_a(x_ctx, x_lat, modl, g, w, wvt, tab, qw, kw):
    b, ctx_len, d = x_ctx.shape
    ctx_tiles = ctx_len // TM
    s_tot = ctx_len + x_lat.shape[1]
    nv = wvt.shape[0]
    out = lambda n: jax.ShapeDtypeStruct((b, s_tot, n), BF16)
    ospec = lambda n: pl.BlockSpec((1, TM, n), lambda i, t: (i, t, 0))
    return pl.pallas_call(
        functools.partial(_proj_a_kernel, ctx_tiles=ctx_tiles, qw=qw, kw=kw),
        grid=(b, s_tot // TM),
        in_specs=_stream_specs(d, ctx_tiles) + [pl.BlockSpec((1, 1, 6, d), lambda i, t: (i, jnp.minimum(t, 1), 0, 0)),
                                                _resident(g.shape), _resident(w.shape), _resident(wvt.shape),
                                                pl.BlockSpec((4, TM, LANES), lambda i, t: (0, t, 0))],
        out_specs=[ospec(qw), ospec(kw), pl.BlockSpec((1, nv, TM), lambda i, t: (i, 0, t))],
        out_shape=[out(qw), out(kw), jax.ShapeDtypeStruct((b, nv, s_tot), BF16)],
        compiler_params=_params(2),
        name="proj_gqa",
    )(x_ctx, x_lat, modl, g, w, wvt, tab)


def _proj_b_kernel(ctx_ref, lat_ref, mod_ref, g_ref, wd_ref, gq_ref, gkv_ref, wuq_ref, wuk_ref, wuvt_ref, tab_ref,
                   q_ref, k_ref, vt_ref, *, ctx_tiles, q_lora, kv_lora):
    def project(x_ref):
        m = mod_ref[0, 0]
        h = _norm_mod(x_ref[0], g_ref[...], m[0:1], m[1:2]).astype(BF16)
        d = _dot(h, wd_ref[...])
        cq = _rms(d[:, :q_lora], gq_ref[...]).astype(BF16)
        ckv = _rms(d[:, q_lora:q_lora + kv_lora], gkv_ref[...]).astype(BF16)
        tq_c, tq_s, tk_c, tk_s = tab_ref[0], tab_ref[1], tab_ref[2], tab_ref[3]
        k_rope = _rope_block_dup(d[:, q_lora + kv_lora:], tk_c, tk_s, B_ROPE // 2)
        q = _dot(cq, wuq_ref[...])
        k = _dot(ckv, wuk_ref[...])
        for j in range(B_HEADS):
            sl = slice(j * LANES, (j + 1) * LANES)
            q_ref[0, :, sl] = _rope_block_dup(q[:, sl], tq_c, tq_s, B_ROPE // 2).astype(BF16)
            k_ref[0, :, sl] = (k[:, sl] + k_rope).astype(BF16)
        vrow = lax.broadcasted_iota(jnp.int32, (wuvt_ref.shape[0], 1), 0) % LANES
        vt_ref[0] = (_dot_nt(wuvt_ref[...], ckv) + jnp.where(vrow == B_V, 1.0, 0.0)).astype(BF16)

    _on_stream_tile(ctx_ref, lat_ref, ctx_tiles, project)


def _proj_b(x_ctx, x_lat, modl, g, wd, gq, gkv, wuq, wuk, wuvt, tab):
    b, ctx_len, d = x_ctx.shape
    ctx_tiles = ctx_len // TM
    s_tot = ctx_len + x_lat.shape[1]
    out = lambda n: jax.ShapeDtypeStruct((b, s_tot, n), BF16)
    ospec = lambda n: pl.BlockSpec((1, TM, n), lambda i, t: (i, t, 0))
    nq, nv = wuq.shape[1], wuvt.shape[0]
    return pl.pallas_call(
        functools.partial(_proj_b_kernel, ctx_tiles=ctx_tiles, q_lora=gq.shape[1], kv_lora=gkv.shape[1]),
        grid=(b, s_tot // TM),
        in_specs=_stream_specs(d, ctx_tiles) + [
            pl.BlockSpec((1, 1, 6, d), lambda i, t: (i, jnp.minimum(t, 1), 0, 0)),
            _resident(g.shape), _resident(wd.shape), _resident(gq.shape), _resident(gkv.shape),
            _resident(wuq.shape), _resident(wuk.shape), _resident(wuvt.shape),
            pl.BlockSpec((4, TM, LANES), lambda i, t: (0, t, 0))],
        out_specs=[ospec(nq), ospec(nq), pl.BlockSpec((1, nv, TM), lambda i, t: (i, 0, t))],
        out_shape=[out(nq), out(nq), jax.ShapeDtypeStruct((b, nv, s_tot), BF16)],
        compiler_params=_params(2),
        name="proj_mla",
    )(x_ctx, x_lat, modl, g, wd, gq, gkv, wuq, wuk, wuvt, tab)


def _attn_a_kernel(sink_ref, q_ref, kc_ref, kp_ref, kt_ref, kn_ref, vc_ref, vp_ref, vt_ref, vn_ref, o_ref,
                   *, ctx_tiles, n_tiles, ctx_out):
    t = pl.program_id(1)
    assert TM == 2 * WINDOW
    edge = WINDOW
    n_heads = q_ref.shape[2] // HEAD_DIM
    group = n_heads // A_KV_HEADS
    lane = lax.broadcasted_iota(jnp.int32, (1, LANES), 1)
    lo = lane < HEAD_DIM
    zero = jnp.zeros((), BF16)

    key = lax.broadcasted_iota(jnp.int32, (edge, edge), 0)
    qry = lax.broadcasted_iota(jnp.int32, (edge, edge), 1)
    tri = lambda valid: jnp.where(valid, 0.0, NEG).astype(F32)
    ALL, NONE = "all", "none"
    ctx_layout = [(ALL, ALL)] * (kc_ref.shape[1] // edge)
    win_layout = [(tri((qry <= key) & (t > ctx_tiles)), NONE),
                  (ALL, tri(qry <= key)), (tri(qry >= key), ALL),
                  (NONE, tri((qry >= key) & (t < n_tiles - 1)))]

    def attend(layout, k_of, v_of):
        def score(h):
            qp = q_ref[0, :, (h // 2) * LANES:(h // 2 + 1) * LANES]
            qe = jnp.where(lo, qp, zero) if h % 2 == 0 else jnp.where(lo, zero, qp)
            s = _dot_nt(k_of(h // group), qe)
            sink = sink_ref[h] * LOG2E
            halves = []
            for half in range(2):
                blocks, run = [], None
                for rb, kinds in enumerate(layout):
                    kind = kinds[half]
                    if isinstance(kind, str) and kind == NONE:
                        blocks.append(None)
                        continue
                    blk = s[rb * edge:(rb + 1) * edge, half * edge:(half + 1) * edge]
                    if not isinstance(kind, str):
                        blk = blk + kind
                    blocks.append(blk)
                    for i in range(edge // MAX_FOLD):
                        part = blk[i * MAX_FOLD:(i + 1) * MAX_FOLD]
                        run = part if run is None else jnp.maximum(run, part)
                halves.append((blocks, jnp.maximum(jnp.max(run, axis=0, keepdims=True), sink)))
            return halves, sink

        def probs(st):
            halves, sink = st
            rows = []
            for rb in range(len(layout)):
                parts = [jnp.zeros((edge, edge), BF16) if blocks[rb] is None
                         else jnp.exp2(blocks[rb] - mx).astype(BF16) for blocks, mx in halves]
                rows.append(jnp.concatenate(parts, axis=1))
            p_sink = jnp.concatenate([jnp.exp2(sink - mx) for _, mx in halves], axis=1)
            return jnp.concatenate(rows, axis=0), p_sink

        def values(h, pr):
            p, p_sink = pr
            acc = _dot(v_of(h // group), p)
            return acc[:HEAD_DIM] * (1.0 / (acc[HEAD_DIM:HEAD_DIM + 1] + p_sink))

        scored, weighted, outs = {}, {}, {}
        for step in range(n_heads + 2 * GQA_LAG):
            h = step - 2 * GQA_LAG
            if 0 <= h < n_heads:
                outs[h] = values(h, weighted.pop(h))
                if h % 2 == 1:
                    pair_t = jnp.concatenate([outs.pop(h - 1), outs.pop(h)], axis=0)
                    o_ref[0, :, (h // 2) * LANES:(h // 2 + 1) * LANES] = pair_t.T.astype(BF16)
            if 0 <= step - GQA_LAG < n_heads:
                weighted[step - GQA_LAG] = probs(scored.pop(step - GQA_LAG))
            if step < n_heads:
                scored[step] = score(step)

    def with_kv(body, k_refs, v_refs):
        ks, vs = [], []
        for j in range(A_KV_HEADS):
            sl = slice(j * LANES, (j + 1) * LANES)
            ks.append(jnp.concatenate([r[0, :, sl] for r in k_refs], axis=0))
            vs.append(jnp.concatenate([r[0, sl, :] for r in v_refs], axis=1))
        body(ks.__getitem__, vs.__getitem__)

    @pl.when(t < ctx_tiles)
    def _():
        if ctx_out:
            with_kv(functools.partial(attend, ctx_layout), [kc_ref], [vc_ref])
        else:
            o_ref[...] = jnp.zeros(o_ref.shape, o_ref.dtype)

    @pl.when(t >= ctx_tiles)
    def _():
        with_kv(functools.partial(attend, ctx_layout + win_layout),
                [kc_ref, kp_ref, kt_ref, kn_ref], [vc_ref, vp_ref, vt_ref, vn_ref])


def _attn_a(q, k, vt, sink, seq, with_ctx):
    b, s_tot, qw = q.shape
    kw, nv = k.shape[2], vt.shape[1]
    ctx_tiles = (s_tot - seq) // TM
    n_tiles = s_tot // TM
    hb = TM // WINDOW
    n_half = s_tot // WINDOW
    prev = lambda t: jnp.maximum(t * hb - 1, 0)
    nxt = lambda t: jnp.minimum((t + 1) * hb, n_half - 1)
    k_specs = [pl.BlockSpec((1, TM, kw), lambda i, t: (i, 0, 0)),
               pl.BlockSpec((1, WINDOW, kw), lambda i, t: (i, prev(t), 0)),
               pl.BlockSpec((1, TM, kw), lambda i, t: (i, t, 0)),
               pl.BlockSpec((1, WINDOW, kw), lambda i, t: (i, nxt(t), 0))]
    v_specs = [pl.BlockSpec((1, nv, TM), lambda i, t: (i, 0, 0)),
               pl.BlockSpec((1, nv, WINDOW), lambda i, t: (i, 0, prev(t))),
               pl.BlockSpec((1, nv, TM), lambda i, t: (i, 0, t)),
               pl.BlockSpec((1, nv, WINDOW), lambda i, t: (i, 0, nxt(t)))]
    return pl.pallas_call(
        functools.partial(_attn_a_kernel, ctx_tiles=ctx_tiles, n_tiles=n_tiles, ctx_out=with_ctx),
        grid=(b, n_tiles),
        in_specs=[pl.BlockSpec(memory_space=pltpu.SMEM),
                  pl.BlockSpec((1, TM, qw), lambda i, t: (i, t, 0))] + k_specs + v_specs,
        out_specs=pl.BlockSpec((1, TM, qw), lambda i, t: (i, t, 0)),
        out_shape=jax.ShapeDtypeStruct((b, s_tot, qw), BF16),
        compiler_params=_params(2),
        name="attn_gqa",
    )(sink, q, k, k, k, k, vt, vt, vt, vt)


def _attn_b_kernel(q_ref, qn_ref, k_ref, vt_ref, o_ref, s_ref, run_ref, *, ctx_len, seq, n_tiles, ctx_out):
    t = pl.program_id(2)
    ctx_tiles = ctx_len // TM
    n_pairs = q_ref.shape[2] // (2 * LANES)
    slot = lambda h: h % s_ref.shape[0]
    assert (2 * n_pairs) % s_ref.shape[0] == 0

    def score_chunk(h, q_src, k0, tk, run):
        hs = slice(h * LANES, (h + 1) * LANES)
        s = _dot_nt(k_ref[0, k0:k0 + tk, hs], q_src[0, :, hs])
        s_ref[slot(h), k0:k0 + tk, :] = s
        for i in range(tk // MAX_FOLD):
            run = jnp.maximum(run, s[i * MAX_FOLD:(i + 1) * MAX_FOLD])
        return run

    def value_chunk(h, k0, tk, mx, acc):
        p = jnp.exp2(s_ref[slot(h), k0:k0 + tk, :] - mx).astype(BF16)
        c = _dot(vt_ref[0, h * LANES:(h + 1) * LANES, k0:k0 + tk], p)
        return c if acc is None else acc + c

    def attend(chunks, first_scored, score_next):
        init = jnp.full((MAX_FOLD, TM), NEG, F32)
        run = [run_ref[0], run_ref[1]] if first_scored else None
        for pr in range(1 if first_scored else 0, n_pairs + 1):
            mx = None if run is None else [jnp.max(r, axis=0, keepdims=True) for r in run]
            run, acc = [init, init], [None, None]
            for k0, tk in chunks:
                for e in range(2):
                    if pr < n_pairs:
                        run[e] = score_chunk(2 * pr + e, q_ref, k0, tk, run[e])
                    elif score_next:
                        run[e] = score_chunk(e, qn_ref, k0, tk, run[e])
                    if pr > 0:
                        acc[e] = value_chunk(2 * (pr - 1) + e, k0, tk, mx[e], acc[e])
            if pr > 0:
                out_t = jnp.concatenate([a[:B_V] * (1.0 / a[B_V:B_V + 1]) for a in acc], axis=0)
                o_ref[0, :, (pr - 1) * LANES:pr * LANES] = out_t.T.astype(BF16)
        if score_next:
            run_ref[0], run_ref[1] = run

    ctx_chunk = [(0, ctx_len)]
    all_chunks = ctx_chunk + [(ctx_len + i * TK_MLA, TK_MLA) for i in range(seq // TK_MLA)]

    @pl.when(t < ctx_tiles)
    def _():
        if ctx_out:
            attend(ctx_chunk, False, False)
        else:
            o_ref[...] = jnp.zeros(o_ref.shape, o_ref.dtype)

    pl.when(t == ctx_tiles)(lambda: attend(all_chunks, False, ctx_tiles < n_tiles - 1))
    pl.when((t > ctx_tiles) & (t < n_tiles - 1))(lambda: attend(all_chunks, True, True))
    pl.when((t > ctx_tiles) & (t == n_tiles - 1))(lambda: attend(all_chunks, True, False))


def _attn_b(q, k, vt, seq, with_ctx):
    b, s_tot, qw = q.shape
    ctx_len = s_tot - seq
    n_tiles = s_tot // TM
    heads = 2 * MLA_PAIRS
    wide = heads * LANES
    pair = 2 * LANES
    return pl.pallas_call(
        functools.partial(_attn_b_kernel, ctx_len=ctx_len, seq=seq, n_tiles=n_tiles, ctx_out=with_ctx),
        grid=(b, qw // wide, n_tiles),
        in_specs=[pl.BlockSpec((1, TM, wide), lambda i, h, t: (i, t, h)),
                  pl.BlockSpec((1, TM, pair), lambda i, h, t: (i, jnp.minimum(t + 1, n_tiles - 1), h * MLA_PAIRS)),
                  pl.BlockSpec((1, s_tot, wide), lambda i, h, t: (i, 0, h)),
                  pl.BlockSpec((1, wide, s_tot), lambda i, h, t: (i, h, 0))],
        out_specs=pl.BlockSpec((1, TM, heads * B_V), lambda i, h, t: (i, t, h)),
        out_shape=jax.ShapeDtypeStruct((b, s_tot, B_HEADS * B_V), BF16),
        scratch_shapes=[pltpu.VMEM((min(heads, 4), s_tot, TM), F32), pltpu.VMEM((2, MAX_FOLD, TM), F32)],
        compiler_params=pltpu.CompilerParams(dimension_semantics=("parallel", "parallel", "arbitrary"),
                                             vmem_limit_bytes=VMEM_LIMIT),
        name="attn_mla",
    )(q, q, k, vt)


def _channel_kernel(*refs, rows, seg_tiles, final):
    (xp_ref, xt_ref, xn_ref, op_ref, ot_ref, on_ref, mod_ref, g_ref, wo_ref, wa_ref, wv_ref,
     cw_ref, cb_ref, wout_ref, gf_ref, y_ref) = refs
    t = pl.program_id(1)
    ext = rows + 2 * HALO
    mid = slice(HALO, HALO + rows)
    m = mod_ref[0, 0]
    x_ext = jnp.concatenate([xp_ref[0], xt_ref[0], xn_ref[0]], axis=0)
    o_ext = jnp.concatenate([op_ref[0], ot_ref[0], on_ref[0]], axis=0)
    x1 = x_ext + m[2:3] * _dot(o_ext, wo_ref[...])
    h2 = _norm_mod(x1, g_ref[...], m[3:4], m[4:5])
    row = lax.broadcasted_iota(jnp.int32, (ext, 1), 0)
    keep = ((row >= HALO) | (t > 0)) & ((row < HALO + rows) | (t < seg_tiles - 1))
    h2 = jnp.where(keep, h2, 0.0).astype(BF16)
    h2_mid = h2[mid]
    cw = cw_ref[...]
    cb = cb_ref[...]
    d_ff = wa_ref.shape[2]
    y = None
    chunks = [slice(c0, min(c0 + FF_CHUNK, d_ff)) for c0 in range(0, d_ff, FF_CHUNK)]
    branches = lambda cs: (_dot(h2, wa_ref[0, :, cs]), _dot(h2_mid, wv_ref[0, :, cs]))
    ahead = branches(chunks[0])
    for i, cs in enumerate(chunks):
        a, v = ahead
        if i + 1 < len(chunks):
            ahead = branches(chunks[i + 1])
        a_prev = pltpu.roll(a, 1, 0)[mid]
        a_next = pltpu.roll(a, ext - 1, 0)[mid]
        conv = a_prev * cw[0:1, cs] + a[mid] * cw[1:2, cs] + a_next * cw[2:3, cs] + cb[:, cs]
        gate = conv * (1.0 / (1.0 + jnp.exp(-conv)))
        hid = (gate * v).astype(BF16)
        part = _dot(hid, wout_ref[0, cs, :])
        y = part if y is None else y + part
    x2 = x1[mid] + m[5:6] * y
    y_ref[0] = _rms(x2, gf_ref[...]) if final else x2


def _rows_at(n_rows, width, offset):
    return pl.BlockSpec((pl.Element(1), pl.Element(n_rows), pl.Element(width)),
                        lambda i, t: (i, pl.multiple_of(offset(t), HALO), 0))


def _channel(x_ctx, x_lat, o, modl, g, wo, w_in, cw, cb, w_out, gf, layer, final):
    b, ctx_len, d = x_ctx.shape
    d_ff = w_out.shape[1]
    weights = (g, wo, w_in, w_in, cw, cb, w_out, gf)
    of_layer = lambda shape, col: pl.BlockSpec((1,) + shape, lambda *_: (layer, 0, col), pipeline_mode=pl.Buffered(1))
    w_specs = [_resident(g.shape), _resident(wo.shape), of_layer((d, d_ff), 0), of_layer((d, d_ff), 1),
               _resident(cw.shape), _resident(cb.shape), of_layer((d_ff, d), 0), _resident(gf.shape)]

    def call(x, rows, o_base, mod_row):
        seg_rows = x.shape[1]
        seg_tiles = seg_rows // rows

        def tiles(width, base):
            prev = lambda t: base + jnp.maximum(t * rows - HALO, 0)
            cur = lambda t: base + t * rows
            nxt = lambda t: base + jnp.minimum((t + 1) * rows, seg_rows - HALO)
            return [_rows_at(HALO, width, prev), _rows_at(rows, width, cur), _rows_at(HALO, width, nxt)]

        return pl.pallas_call(
            functools.partial(_channel_kernel, rows=rows, seg_tiles=seg_tiles, final=final),
            grid=(b, seg_tiles),
            in_specs=(tiles(d, 0) + tiles(o.shape[2], o_base)
                      + [pl.BlockSpec((1, 1, 6, d), lambda i, t: (i, mod_row, 0, 0))] + w_specs),
            out_specs=pl.BlockSpec((1, rows, d), lambda i, t: (i, t, 0)),
            out_shape=jax.ShapeDtypeStruct((b, seg_rows, d), F32),
            compiler_params=_params(2),
            name="channel",
        )(x, x, x, o, o, o, modl, *weights)

    y_lat = call(x_lat, TM_CHANNEL, ctx_len, 1)
    return (None if final else call(x_ctx, TM, 0, 0)), y_lat


def _axial_angles(rows, rot_dim):
    row = jnp.repeat(jnp.arange(rows), GRID_W).astype(F32)
    col = jnp.tile(jnp.arange(GRID_W), rows).astype(F32)
    n_freq = rot_dim // 4
    inv = ROPE_BASE ** (-jnp.arange(n_freq, dtype=F32) / n_freq)
    return jnp.concatenate([row[:, None] * inv, col[:, None] * inv], axis=-1)


def _with_ctx_rows(cos, sin, ctx_cos, ctx_len, q_scale):
    cos = jnp.concatenate([jnp.broadcast_to(ctx_cos, (ctx_len, LANES)), cos], axis=0)
    sin = jnp.concatenate([jnp.zeros((ctx_len, LANES), F32), sin], axis=0)
    return jnp.stack([cos * q_scale, sin * q_scale, cos, sin])


def _tables_a(seq, ctx_len):
    ang = _axial_angles(seq // GRID_W, HEAD_DIM)
    cos, sin = jnp.cos(ang), jnp.sin(ang)
    reps = LANES // HEAD_DIM
    cos_l = jnp.tile(cos, (1, 2 * reps))
    sin_l = jnp.tile(jnp.concatenate([-sin, sin], axis=1), (1, reps))
    return _with_ctx_rows(cos_l, sin_l, jnp.ones((1, LANES), F32), ctx_len, LOG2E * HEAD_DIM ** -0.5)


def _tables_b(seq, ctx_len):
    ang = _axial_angles(seq // GRID_W, B_ROPE)
    cos, sin = jnp.cos(ang), jnp.sin(ang)
    copy = LANES - B_NOPE - B_ROPE
    cos_l = jnp.concatenate([jnp.ones((seq, B_NOPE), F32), cos, cos, jnp.zeros((seq, copy), F32)], axis=1)
    sin_l = jnp.concatenate([jnp.zeros((seq, B_NOPE), F32), -sin, sin, jnp.zeros((seq, copy), F32)], axis=1)
    ctx_cos = jnp.concatenate([jnp.ones((1, B_NOPE + B_ROPE), F32), jnp.zeros((1, copy), F32)], axis=1)
    return _with_ctx_rows(cos_l, sin_l, ctx_cos, ctx_len, LOG2E * (B_NOPE + B_ROPE) ** -0.5)


def _deinterleave(n):
    return jnp.concatenate([jnp.arange(0, n, 2), jnp.arange(1, n, 2)])


def _weights_a(wqkv):
    d = wqkv.shape[0]
    kw = A_KV_HEADS * HEAD_DIM
    qw = wqkv.shape[1] - 2 * kw
    perm = _deinterleave(HEAD_DIM)
    wq = wqkv[:, :qw].reshape(d, -1, HEAD_DIM)[:, :, perm].reshape(d, qw)
    wk = wqkv[:, qw:qw + kw].reshape(d, A_KV_HEADS, HEAD_DIM)[:, :, perm]
    wkk = jnp.concatenate([wk, wk], axis=2).reshape(d, 2 * kw)
    wv = wqkv[:, qw + kw:].reshape(d, A_KV_HEADS, HEAD_DIM)
    wv = jnp.concatenate([wv, jnp.zeros((d, A_KV_HEADS, LANES - HEAD_DIM), F32)], axis=2)
    wvt = wv.reshape(d, A_KV_HEADS * LANES).T
    return jnp.concatenate([wq, wkk], axis=1).astype(BF16), wvt.astype(BF16), qw, 2 * kw


def _weights_b(wdown, wuq, wuk, wuv, q_lora, kv_lora):
    d = wdown.shape[0]
    perm = _deinterleave(B_ROPE)
    assert LANES - B_NOPE - B_ROPE == B_ROPE
    w_rope = wdown[:, q_lora + kv_lora:][:, perm]
    w_rope = jnp.concatenate([jnp.zeros((d, B_NOPE), F32), w_rope, w_rope], axis=1)
    wd = jnp.concatenate([wdown[:, :q_lora + kv_lora], w_rope], axis=1)
    uq = wuq.reshape(q_lora, B_HEADS, B_NOPE + B_ROPE)
    uq_rope = uq[:, :, B_NOPE:][:, :, perm]
    uq = jnp.concatenate([uq[:, :, :B_NOPE], uq_rope, uq_rope], axis=2)
    uk = wuk.reshape(kv_lora, B_HEADS, B_NOPE)
    uk = jnp.concatenate([uk, jnp.zeros((kv_lora, B_HEADS, LANES - B_NOPE), F32)], axis=2)
    uv = wuv.reshape(kv_lora, B_HEADS, B_V)
    uv = jnp.concatenate([uv, jnp.zeros((kv_lora, B_HEADS, LANES - B_V), F32)], axis=2)
    flat = lambda w: w.reshape(w.shape[0], B_HEADS * LANES).astype(BF16)
    return wd.astype(BF16), flat(uq), flat(uk), flat(uv).T


def kernel(x, c, ctx, c_ctx, mod_w, mod_b, norm1_g, norm2_g, a_wqkv, a_wo, a_sink, b_wdown, b_qnorm_g, b_wuq,
           b_kvnorm_g, b_wuk, b_wuv, b_wo, f_win, f_conv_w, f_conv_b, f_wout, final_g):
    bsz, seq, d = x.shape
    ctx_len = ctx.shape[1]
    depth = mod_w.shape[0]
    assert ctx_len == TM and seq % TK_MLA == 0 and seq % TM_CHANNEL == 0 and seq % GRID_W == 0

    cond_rows = -(-(bsz + 1) // SUBLANES) * SUBLANES
    cond = jnp.concatenate([c, c_ctx[None], jnp.zeros((cond_rows - bsz - 1, d), F32)], axis=0)
    mod = _modulation(cond, mod_w, mod_b)

    tab_a = _tables_a(seq, ctx_len)
    tab_b = _tables_b(seq, ctx_len)
    x_ctx, x_lat = ctx, x
    w_in, w_out = f_win.astype(BF16), f_wout.astype(BF16)
    row = lambda v: v.reshape(1, -1)

    for i in range(depth):
        last = i == depth - 1
        lat = mod[i, :bsz].reshape(bsz, 6, d)
        cmod = jnp.broadcast_to(mod[i, bsz].reshape(1, 6, d), (bsz, 6, d))
        modl = jnp.stack([cmod, lat], axis=1)
        j = i // 2
        if i % 2 == 0:
            w, wvt, qw, kw = _weights_a(a_wqkv[j])
            q, k, vt = _proj_a(x_ctx, x_lat, modl, row(norm1_g[i]), w, wvt, tab_a, qw, kw)
            o = _attn_a(q, k, vt, a_sink[j], seq, not last)
            wo = a_wo[j]
        else:
            q_lora, kv_lora = b_qnorm_g.shape[1], b_kvnorm_g.shape[1]
            wd, wuq, wuk, wuvt = _weights_b(b_wdown[j], b_wuq[j], b_wuk[j], b_wuv[j], q_lora, kv_lora)
            q, k, vt = _proj_b(x_ctx, x_lat, modl, row(norm1_g[i]), wd, row(b_qnorm_g[j]), row(b_kvnorm_g[j]),
                               wuq, wuk, wuvt, tab_b)
            o = _attn_b(q, k, vt, seq, not last)
            wo = b_wo[j]
        x_ctx, x_lat = _channel(x_ctx, x_lat, o, modl, row(norm2_g[i]), wo.astype(BF16), w_in, f_conv_w[i],
                                row(f_conv_b[i]), w_out, row(final_g), i, last)
    return x_lat
```

```python
import functools

import jax
import jax.numpy as jnp
from jax import lax
from jax.experimental import pallas as pl
from jax.experimental.pallas import tpu as pltpu

GRID_W = 64
HEAD_DIM = 64
A_KV_HEADS = 4
WINDOW = 128
B_HEADS = 16
B_NOPE = 64
B_ROPE = 32
B_V = 64
ROPE_BASE = 10000.0
EPS = 1e-6
NEG = -1e30
LOG2E = 1.4426950408889634

LANES = 128
SUBLANES = 8
V7X_VMEM_BYTES = 64 * 1024 * 1024
VMEM_LIMIT = V7X_VMEM_BYTES * 7 // 8
TM = 256
HALO = 16
TK_MLA = 1024
MLA_PAIRS = 4
GQA_LAG = 3
MAX_FOLD = 64
TM_CHANNEL = 512
FF_CHUNK = 768

F32 = jnp.float32
BF16 = jnp.bfloat16
_NT = (((1,), (1,)), ((), ()))


def _dot(a, b):
    return jnp.dot(a, b, preferred_element_type=F32)


def _dot_nt(a, b):
    return lax.dot_general(a, b, _NT, preferred_element_type=F32)


def _rms(xf, g):
    ms = jnp.mean(xf * xf, axis=-1, keepdims=True)
    return xf * lax.rsqrt(ms + EPS) * g


def _norm_mod(xf, g, shift, scale):
    return _rms(xf, g) * (1.0 + scale) + shift


def _params(n_axes):
    return pltpu.CompilerParams(dimension_semantics=("parallel",) * n_axes,
                                vmem_limit_bytes=VMEM_LIMIT)


def _resident(shape):
    nd = len(shape)
    return pl.BlockSpec(shape, lambda *_: (0,) * nd, pipeline_mode=pl.Buffered(1))


def _mod_kernel(c_ref, w_ref, b_ref, o_ref):
    c = c_ref[...]
    silu = c * (1.0 / (1.0 + jnp.exp(-c)))
    o_ref[0] = _dot(silu.astype(BF16), w_ref[0].astype(BF16)) + b_ref[0]


def _modulation(cond, mod_w, mod_b):
    depth, d, d6 = mod_w.shape
    rows = cond.shape[0]
    return pl.pallas_call(
        _mod_kernel,
        grid=(depth, d6 // d),
        in_specs=[pl.BlockSpec((rows, d), lambda i, j: (0, 0)),
                  pl.BlockSpec((1, d, d), lambda i, j: (i, 0, j)),
                  pl.BlockSpec((1, 1, d), lambda i, j: (i, 0, j))],
        out_specs=pl.BlockSpec((1, rows, d), lambda i, j: (i, 0, j)),
        out_shape=jax.ShapeDtypeStruct((depth, rows, d6), F32),
        compiler_params=_params(2),
        name="modulation",
    )(cond, mod_w, mod_b.reshape(depth, 1, d6))


def _rope_block(xb, cos, sin_signed, first_half, half):
    swapped = jnp.where(first_half, pltpu.roll(xb, LANES - half, 1), pltpu.roll(xb, half, 1))
    return xb * cos + swapped * sin_signed


def _rope_block_dup(xb, cos, sin_signed, half):
    return xb * cos + pltpu.roll(xb, LANES - half, 1) * sin_signed


def _stream_tile(ctx_ref, lat_ref, ctx_tiles):
    return jnp.where(pl.program_id(1) < ctx_tiles, ctx_ref[0], lat_ref[0])


def _stream_specs(d, ctx_tiles):
    return [pl.BlockSpec((1, TM, d), lambda i, t: (i, jnp.minimum(t, ctx_tiles - 1), 0)),
            pl.BlockSpec((1, TM, d), lambda i, t: (i, jnp.maximum(t - ctx_tiles, 0), 0))]


def _proj_a_kernel(ctx_ref, lat_ref, mod_ref, g_ref, w_ref, wvt_ref, tab_ref, q_ref, k_ref, vt_ref, *, ctx_tiles, qw, kw):
    m = mod_ref[0, 0]
    x = _stream_tile(ctx_ref, lat_ref, ctx_tiles)
    h = _norm_mod(x, g_ref[...], m[0:1], m[1:2]).astype(BF16)
    qk = _dot(h, w_ref[...])
    lane = lax.broadcasted_iota(jnp.int32, (TM, LANES), 1)
    first = (lane % HEAD_DIM) < (HEAD_DIM // 2)
    cq, sq, ck, sk = tab_ref[0], tab_ref[1], tab_ref[2], tab_ref[3]
    for j in range(qw // LANES):
        blk = qk[:, j * LANES:(j + 1) * LANES]
        q_ref[0, :, j * LANES:(j + 1) * LANES] = _rope_block(blk, cq, sq, first, HEAD_DIM // 2).astype(BF16)
    for j in range(kw // LANES):
        blk = qk[:, qw + j * LANES:qw + (j + 1) * LANES]
        k_ref[0, :, j * LANES:(j + 1) * LANES] = _rope_block_dup(blk, ck, sk, HEAD_DIM // 2).astype(BF16)
    vrow = lax.broadcasted_iota(jnp.int32, (wvt_ref.shape[0], 1), 0) % LANES
    vt_ref[0] = (_dot_nt(wvt_ref[...], h) + jnp.where(vrow == HEAD_DIM, 1.0, 0.0)).astype(BF16)


def _proj_a(x_ctx, x_lat, modl, g, w, wvt, tab, qw, kw):
    b, ctx_len, d = x_ctx.shape
    ctx_tiles = ctx_len // TM
    s_tot = ctx_len + x_lat.shape[1]
    nv = wvt.shape[0]
    out = lambda n: jax.ShapeDtypeStruct((b, s_tot, n), BF16)
    ospec = lambda n: pl.BlockSpec((1, TM, n), lambda i, t: (i, t, 0))
    return pl.pallas_call(
        functools.partial(_proj_a_kernel, ctx_tiles=ctx_tiles, qw=qw, kw=kw),
        grid=(b, s_tot // TM),
        in_specs=_stream_specs(d, ctx_tiles) + [pl.BlockSpec((1, 1, 6, d), lambda i, t: (i, jnp.minimum(t, 1), 0, 0)),
                                                _resident(g.shape), _resident(w.shape), _resident(wvt.shape),
                                                pl.BlockSpec((4, TM, LANES), lambda i, t: (0, t, 0))],
        out_specs=[ospec(qw), ospec(kw), pl.BlockSpec((1, nv, TM), lambda i, t: (i, 0, t))],
        out_shape=[out(qw), out(kw), jax.ShapeDtypeStruct((b, nv, s_tot), BF16)],
        compiler_params=_params(2),
        name="proj_gqa",
    )(x_ctx, x_lat, modl, g, w, wvt, tab)


def _proj_b_kernel(ctx_ref, lat_ref, mod_ref, g_ref, wd_ref, gq_ref, gkv_ref, wuq_ref, wuk_ref, wuvt_ref, tab_ref,
                   q_ref, k_ref, vt_ref, *, ctx_tiles, q_lora, kv_lora):
    m = mod_ref[0, 0]
    x = _stream_tile(ctx_ref, lat_ref, ctx_tiles)
    h = _norm_mod(x, g_ref[...], m[0:1], m[1:2]).astype(BF16)
    d = _dot(h, wd_ref[...])
    cq = _rms(d[:, :q_lora], gq_ref[...]).astype(BF16)
    ckv = _rms(d[:, q_lora:q_lora + kv_lora], gkv_ref[...]).astype(BF16)
    tq_c, tq_s, tk_c, tk_s = tab_ref[0], tab_ref[1], tab_ref[2], tab_ref[3]
    k_rope = _rope_block_dup(d[:, q_lora + kv_lora:], tk_c, tk_s, B_ROPE // 2)
    q = _dot(cq, wuq_ref[...])
    k = _dot(ckv, wuk_ref[...])
    for j in range(B_HEADS):
        sl = slice(j * LANES, (j + 1) * LANES)
        q_ref[0, :, sl] = _rope_block_dup(q[:, sl], tq_c, tq_s, B_ROPE // 2).astype(BF16)
        k_ref[0, :, sl] = (k[:, sl] + k_rope).astype(BF16)
    vrow = lax.broadcasted_iota(jnp.int32, (wuvt_ref.shape[0], 1), 0) % LANES
    vt_ref[0] = (_dot_nt(wuvt_ref[...], ckv) + jnp.where(vrow == B_V, 1.0, 0.0)).astype(BF16)


def _proj_b(x_ctx, x_lat, modl, g, wd, gq, gkv, wuq, wuk, wuvt, tab):
    b, ctx_len, d = x_ctx.shape
    ctx_tiles = ctx_len // TM
    s_tot = ctx_len + x_lat.shape[1]
    out = lambda n: jax.ShapeDtypeStruct((b, s_tot, n), BF16)
    ospec = lambda n: pl.BlockSpec((1, TM, n), lambda i, t: (i, t, 0))
    nq, nv = wuq.shape[1], wuvt.shape[0]
    return pl.pallas_call(
        functools.partial(_proj_b_kernel, ctx_tiles=ctx_tiles, q_lora=gq.shape[1], kv_lora=gkv.shape[1]),
        grid=(b, s_tot // TM),
        in_specs=_stream_specs(d, ctx_tiles) + [
            pl.BlockSpec((1, 1, 6, d), lambda i, t: (i, jnp.minimum(t, 1), 0, 0)),
            _resident(g.shape), _resident(wd.shape), _resident(gq.shape), _resident(gkv.shape),
            _resident(wuq.shape), _resident(wuk.shape), _resident(wuvt.shape),
            pl.BlockSpec((4, TM, LANES), lambda i, t: (0, t, 0))],
        out_specs=[ospec(nq), ospec(nq), pl.BlockSpec((1, nv, TM), lambda i, t: (i, 0, t))],
        out_shape=[out(nq), out(nq), jax.ShapeDtypeStruct((b, nv, s_tot), BF16)],
        compiler_params=_params(2),
        name="proj_mla",
    )(x_ctx, x_lat, modl, g, wd, gq, gkv, wuq, wuk, wuvt, tab)


def _attn_a_kernel(sink_ref, q_ref, kc_ref, kp_ref, kt_ref, kn_ref, vc_ref, vp_ref, vt_ref, vn_ref, o_ref,
                   *, ctx_tiles, n_tiles, ctx_out):
    t = pl.program_id(1)
    assert TM == 2 * WINDOW
    edge = WINDOW
    n_heads = q_ref.shape[2] // HEAD_DIM
    group = n_heads // A_KV_HEADS
    lane = lax.broadcasted_iota(jnp.int32, (1, LANES), 1)
    lo = lane < HEAD_DIM
    zero = jnp.zeros((), BF16)

    key = lax.broadcasted_iota(jnp.int32, (edge, edge), 0)
    qry = lax.broadcasted_iota(jnp.int32, (edge, edge), 1)
    tri = lambda valid: jnp.where(valid, 0.0, NEG).astype(F32)
    ALL, NONE = "all", "none"
    ctx_layout = [(ALL, ALL)] * (kc_ref.shape[1] // edge)
    win_layout = [(tri((qry <= key) & (t > ctx_tiles)), NONE),
                  (ALL, tri(qry <= key)), (tri(qry >= key), ALL),
                  (NONE, tri((qry >= key) & (t < n_tiles - 1)))]

    def attend(layout, k_of, v_of):
        def score(h):
            qp = q_ref[0, :, (h // 2) * LANES:(h // 2 + 1) * LANES]
            qe = jnp.where(lo, qp, zero) if h % 2 == 0 else jnp.where(lo, zero, qp)
            s = _dot_nt(k_of(h // group), qe)
            sink = sink_ref[h] * LOG2E
            halves = []
            for half in range(2):
                blocks, run = [], None
                for rb, kinds in enumerate(layout):
                    kind = kinds[half]
                    if isinstance(kind, str) and kind == NONE:
                        blocks.append(None)
                        continue
                    blk = s[rb * edge:(rb + 1) * edge, half * edge:(half + 1) * edge]
                    if not isinstance(kind, str):
                        blk = blk + kind
                    blocks.append(blk)
                    for i in range(edge // MAX_FOLD):
                        part = blk[i * MAX_FOLD:(i + 1) * MAX_FOLD]
                        run = part if run is None else jnp.maximum(run, part)
                halves.append((blocks, jnp.maximum(jnp.max(run, axis=0, keepdims=True), sink)))
            return halves, sink

        def probs(st):
            halves, sink = st
            rows = []
            for rb in range(len(layout)):
                parts = [jnp.zeros((edge, edge), BF16) if blocks[rb] is None
                         else jnp.exp2(blocks[rb] - mx).astype(BF16) for blocks, mx in halves]
                rows.append(jnp.concatenate(parts, axis=1))
            p_sink = jnp.concatenate([jnp.exp2(sink - mx) for _, mx in halves], axis=1)
            return jnp.concatenate(rows, axis=0), p_sink

        def values(h, pr):
            p, p_sink = pr
            acc = _dot(v_of(h // group), p)
            return acc[:HEAD_DIM] * (1.0 / (acc[HEAD_DIM:HEAD_DIM + 1] + p_sink))

        scored, weighted, outs = {}, {}, {}
        for step in range(n_heads + 2 * GQA_LAG):
            h = step - 2 * GQA_LAG
            if 0 <= h < n_heads:
                outs[h] = values(h, weighted.pop(h))
                if h % 2 == 1:
                    pair_t = jnp.concatenate([outs.pop(h - 1), outs.pop(h)], axis=0)
                    o_ref[0, :, (h // 2) * LANES:(h // 2 + 1) * LANES] = pair_t.T.astype(BF16)
            if 0 <= step - GQA_LAG < n_heads:
                weighted[step - GQA_LAG] = probs(scored.pop(step - GQA_LAG))
            if step < n_heads:
                scored[step] = score(step)

    def with_kv(body, k_refs, v_refs):
        ks, vs = [], []
        for j in range(A_KV_HEADS):
            sl = slice(j * LANES, (j + 1) * LANES)
            ks.append(jnp.concatenate([r[0, :, sl] for r in k_refs], axis=0))
            vs.append(jnp.concatenate([r[0, sl, :] for r in v_refs], axis=1))
        body(ks.__getitem__, vs.__getitem__)

    @pl.when(t < ctx_tiles)
    def _():
        if ctx_out:
            with_kv(functools.partial(attend, ctx_layout), [kc_ref], [vc_ref])
        else:
            o_ref[...] = jnp.zeros(o_ref.shape, o_ref.dtype)

    @pl.when(t >= ctx_tiles)
    def _():
        with_kv(functools.partial(attend, ctx_layout + win_layout),
                [kc_ref, kp_ref, kt_ref, kn_ref], [vc_ref, vp_ref, vt_ref, vn_ref])


def _attn_a(q, k, vt, sink, seq, with_ctx):
    b, s_tot, qw = q.shape
    kw, nv = k.shape[2], vt.shape[1]
    ctx_tiles = (s_tot - seq) // TM
    n_tiles = s_tot // TM
    hb = TM // WINDOW
    n_half = s_tot // WINDOW
    prev = lambda t: jnp.maximum(t * hb - 1, 0)
    nxt = lambda t: jnp.minimum((t + 1) * hb, n_half - 1)
    k_specs = [pl.BlockSpec((1, TM, kw), lambda i, t: (i, 0, 0)),
               pl.BlockSpec((1, WINDOW, kw), lambda i, t: (i, prev(t), 0)),
               pl.BlockSpec((1, TM, kw), lambda i, t: (i, t, 0)),
               pl.BlockSpec((1, WINDOW, kw), lambda i, t: (i, nxt(t), 0))]
    v_specs = [pl.BlockSpec((1, nv, TM), lambda i, t: (i, 0, 0)),
               pl.BlockSpec((1, nv, WINDOW), lambda i, t: (i, 0, prev(t))),
               pl.BlockSpec((1, nv, TM), lambda i, t: (i, 0, t)),
               pl.BlockSpec((1, nv, WINDOW), lambda i, t: (i, 0, nxt(t)))]
    return pl.pallas_call(
        functools.partial(_attn_a_kernel, ctx_tiles=ctx_tiles, n_tiles=n_tiles, ctx_out=with_ctx),
        grid=(b, n_tiles),
        in_specs=[pl.BlockSpec(memory_space=pltpu.SMEM),
                  pl.BlockSpec((1, TM, qw), lambda i, t: (i, t, 0))] + k_specs + v_specs,
        out_specs=pl.BlockSpec((1, TM, qw), lambda i, t: (i, t, 0)),
        out_shape=jax.ShapeDtypeStruct((b, s_tot, qw), BF16),
        compiler_params=_params(2),
        name="attn_gqa",
    )(sink, q, k, k, k, k, vt, vt, vt, vt)


def _attn_b_kernel(q_ref, qn_ref, k_ref, vt_ref, o_ref, s_ref, run_ref, *, ctx_len, seq, n_tiles, ctx_out):
    t = pl.program_id(2)
    ctx_tiles = ctx_len // TM
    n_pairs = q_ref.shape[2] // (2 * LANES)
    slot = lambda h: h % s_ref.shape[0]
    assert (2 * n_pairs) % s_ref.shape[0] == 0

    def score_chunk(h, q_src, k0, tk, run):
        hs = slice(h * LANES, (h + 1) * LANES)
        s = _dot_nt(k_ref[0, k0:k0 + tk, hs], q_src[0, :, hs])
        s_ref[slot(h), k0:k0 + tk, :] = s
        for i in range(tk // MAX_FOLD):
            run = jnp.maximum(run, s[i * MAX_FOLD:(i + 1) * MAX_FOLD])
        return run

    def value_chunk(h, k0, tk, mx, acc):
        p = jnp.exp2(s_ref[slot(h), k0:k0 + tk, :] - mx).astype(BF16)
        c = _dot(vt_ref[0, h * LANES:(h + 1) * LANES, k0:k0 + tk], p)
        return c if acc is None else acc + c

    def attend(chunks, first_scored, score_next):
        init = jnp.full((MAX_FOLD, TM), NEG, F32)
        run = [run_ref[0], run_ref[1]] if first_scored else None
        for pr in range(1 if first_scored else 0, n_pairs + 1):
            mx = None if run is None else [jnp.max(r, axis=0, keepdims=True) for r in run]
            run, acc = [init, init], [None, None]
            for k0, tk in chunks:
                for e in range(2):
                    if pr < n_pairs:
                        run[e] = score_chunk(2 * pr + e, q_ref, k0, tk, run[e])
                    elif score_next:
                        run[e] = score_chunk(e, qn_ref, k0, tk, run[e])
                    if pr > 0:
                        acc[e] = value_chunk(2 * (pr - 1) + e, k0, tk, mx[e], acc[e])
            if pr > 0:
                out_t = jnp.concatenate([a[:B_V] * (1.0 / a[B_V:B_V + 1]) for a in acc], axis=0)
                o_ref[0, :, (pr - 1) * LANES:pr * LANES] = out_t.T.astype(BF16)
        if score_next:
            run_ref[0], run_ref[1] = run

    ctx_chunk = [(0, ctx_len)]
    all_chunks = ctx_chunk + [(ctx_len + i * TK_MLA, TK_MLA) for i in range(seq // TK_MLA)]

    @pl.when(t < ctx_tiles)
    def _():
        if ctx_out:
            attend(ctx_chunk, False, False)
        else:
            o_ref[...] = jnp.zeros(o_ref.shape, o_ref.dtype)

    pl.when(t == ctx_tiles)(lambda: attend(all_chunks, False, ctx_tiles < n_tiles - 1))
    pl.when((t > ctx_tiles) & (t < n_tiles - 1))(lambda: attend(all_chunks, True, True))
    pl.when((t > ctx_tiles) & (t == n_tiles - 1))(lambda: attend(all_chunks, True, False))


def _attn_b(q, k, vt, seq, with_ctx):
    b, s_tot, qw = q.shape
    ctx_len = s_tot - seq
    n_tiles = s_tot // TM
    heads = 2 * MLA_PAIRS
    wide = heads * LANES
    pair = 2 * LANES
    return pl.pallas_call(
        functools.partial(_attn_b_kernel, ctx_len=ctx_len, seq=seq, n_tiles=n_tiles, ctx_out=with_ctx),
        grid=(b, qw // wide, n_tiles),
        in_specs=[pl.BlockSpec((1, TM, wide), lambda i, h, t: (i, t, h)),
                  pl.BlockSpec((1, TM, pair), lambda i, h, t: (i, jnp.minimum(t + 1, n_tiles - 1), h * MLA_PAIRS)),
                  pl.BlockSpec((1, s_tot, wide), lambda i, h, t: (i, 0, h)),
                  pl.BlockSpec((1, wide, s_tot), lambda i, h, t: (i, h, 0))],
        out_specs=pl.BlockSpec((1, TM, heads * B_V), lambda i, h, t: (i, t, h)),
        out_shape=jax.ShapeDtypeStruct((b, s_tot, B_HEADS * B_V), BF16),
        scratch_shapes=[pltpu.VMEM((min(heads, 4), s_tot, TM), F32), pltpu.VMEM((2, MAX_FOLD, TM), F32)],
        compiler_params=pltpu.CompilerParams(dimension_semantics=("parallel", "parallel", "arbitrary"),
                                             vmem_limit_bytes=VMEM_LIMIT),
        name="attn_mla",
    )(q, q, k, vt)


def _channel_kernel(*refs, rows, seg_tiles, final):
    (xp_ref, xt_ref, xn_ref, op_ref, ot_ref, on_ref, mod_ref, g_ref, wo_ref, wa_ref, wv_ref,
     cw_ref, cb_ref, wout_ref, gf_ref, y_ref) = refs
    t = pl.program_id(1)
    ext = rows + 2 * HALO
    mid = slice(HALO, HALO + rows)
    m = mod_ref[0, 0]
    x_ext = jnp.concatenate([xp_ref[0], xt_ref[0], xn_ref[0]], axis=0)
    o_ext = jnp.concatenate([op_ref[0], ot_ref[0], on_ref[0]], axis=0)
    halves = [slice(0, ext // 2), slice(ext // 2, ext)]
    x1_parts = [x_ext[p] + m[2:3] * _dot(o_ext[p], wo_ref[...]) for p in halves]
    x1 = jnp.concatenate(x1_parts, axis=0)
    h2 = jnp.concatenate([_norm_mod(xp, g_ref[...], m[3:4], m[4:5]) for xp in x1_parts], axis=0)
    row = lax.broadcasted_iota(jnp.int32, (ext, 1), 0)
    keep = ((row >= HALO) | (t > 0)) & ((row < HALO + rows) | (t < seg_tiles - 1))
    h2 = jnp.where(keep, h2, 0.0).astype(BF16)
    h2_mid = h2[mid]
    cw = cw_ref[...]
    cb = cb_ref[...]
    d_ff = wa_ref.shape[2]
    y = None
    chunks = [slice(c0, min(c0 + FF_CHUNK, d_ff)) for c0 in range(0, d_ff, FF_CHUNK)]
    branches = lambda cs: (_dot(h2, wa_ref[0, :, cs]), _dot(h2_mid, wv_ref[0, :, cs]))

    def hidden(cs, av):
        a, v = av
        a_prev = pltpu.roll(a, 1, 0)[mid]
        a_next = pltpu.roll(a, ext - 1, 0)[mid]
        conv = a_prev * cw[0:1, cs] + a[mid] * cw[1:2, cs] + a_next * cw[2:3, cs] + cb[:, cs]
        gate = conv * (1.0 / (1.0 + jnp.exp(-conv)))
        return (gate * v).astype(BF16)

    av, hid = {}, {}
    for step in range(len(chunks) + 2):
        if step < len(chunks):
            av[step] = branches(chunks[step])
        if 0 <= step - 1 < len(chunks):
            hid[step - 1] = hidden(chunks[step - 1], av.pop(step - 1))
        if 0 <= step - 2 < len(chunks):
            part = _dot(hid.pop(step - 2), wout_ref[0, chunks[step - 2], :])
            y = part if y is None else y + part
    x2 = x1[mid] + m[5:6] * y
    y_ref[0] = _rms(x2, gf_ref[...]) if final else x2


def _rows_at(n_rows, width, offset):
    return pl.BlockSpec((pl.Element(1), pl.Element(n_rows), pl.Element(width)),
                        lambda i, t: (i, pl.multiple_of(offset(t), HALO), 0))


def _channel(x_ctx, x_lat, o, modl, g, wo, w_in, cw, cb, w_out, gf, layer, final):
    b, ctx_len, d = x_ctx.shape
    d_ff = w_out.shape[1]
    weights = (g, wo, w_in, w_in, cw, cb, w_out, gf)
    of_layer = lambda shape, col: pl.BlockSpec((1,) + shape, lambda *_: (layer, 0, col), pipeline_mode=pl.Buffered(1))
    w_specs = [_resident(g.shape), _resident(wo.shape), of_layer((d, d_ff), 0), of_layer((d, d_ff), 1),
               _resident(cw.shape), _resident(cb.shape), of_layer((d_ff, d), 0), _resident(gf.shape)]

    def call(x, rows, o_base, mod_row):
        seg_rows = x.shape[1]
        seg_tiles = seg_rows // rows

        def tiles(width, base):
            prev = lambda t: base + jnp.maximum(t * rows - HALO, 0)
            cur = lambda t: base + t * rows
            nxt = lambda t: base + jnp.minimum((t + 1) * rows, seg_rows - HALO)
            return [_rows_at(HALO, width, prev), _rows_at(rows, width, cur), _rows_at(HALO, width, nxt)]

        return pl.pallas_call(
            functools.partial(_channel_kernel, rows=rows, seg_tiles=seg_tiles, final=final),
            grid=(b, seg_tiles),
            in_specs=(tiles(d, 0) + tiles(o.shape[2], o_base)
                      + [pl.BlockSpec((1, 1, 6, d), lambda i, t: (i, mod_row, 0, 0))] + w_specs),
            out_specs=pl.BlockSpec((1, rows, d), lambda i, t: (i, t, 0)),
            out_shape=jax.ShapeDtypeStruct((b, seg_rows, d), F32),
            compiler_params=_params(2),
            name="channel",
        )(x, x, x, o, o, o, modl, *weights)

    y_lat = call(x_lat, TM_CHANNEL, ctx_len, 1)
    return (None if final else call(x_ctx, TM, 0, 0)), y_lat


def _axial_angles(rows, rot_dim):
    row = jnp.repeat(jnp.arange(rows), GRID_W).astype(F32)
    col = jnp.tile(jnp.arange(GRID_W), rows).astype(F32)
    n_freq = rot_dim // 4
    inv = ROPE_BASE ** (-jnp.arange(n_freq, dtype=F32) / n_freq)
    return jnp.concatenate([row[:, None] * inv, col[:, None] * inv], axis=-1)


def _with_ctx_rows(cos, sin, ctx_cos, ctx_len, q_scale):
    cos = jnp.concatenate([jnp.broadcast_to(ctx_cos, (ctx_len, LANES)), cos], axis=0)
    sin = jnp.concatenate([jnp.zeros((ctx_len, LANES), F32), sin], axis=0)
    return jnp.stack([cos * q_scale, sin * q_scale, cos, sin])


def _tables_a(seq, ctx_len):
    ang = _axial_angles(seq // GRID_W, HEAD_DIM)
    cos, sin = jnp.cos(ang), jnp.sin(ang)
    reps = LANES // HEAD_DIM
    cos_l = jnp.tile(cos, (1, 2 * reps))
    sin_l = jnp.tile(jnp.concatenate([-sin, sin], axis=1), (1, reps))
    return _with_ctx_rows(cos_l, sin_l, jnp.ones((1, LANES), F32), ctx_len, LOG2E * HEAD_DIM ** -0.5)


def _tables_b(seq, ctx_len):
    ang = _axial_angles(seq // GRID_W, B_ROPE)
    cos, sin = jnp.cos(ang), jnp.sin(ang)
    copy = LANES - B_NOPE - B_ROPE
    cos_l = jnp.concatenate([jnp.ones((seq, B_NOPE), F32), cos, cos, jnp.zeros((seq, copy), F32)], axis=1)
    sin_l = jnp.concatenate([jnp.zeros((seq, B_NOPE), F32), -sin, sin, jnp.zeros((seq, copy), F32)], axis=1)
    ctx_cos = jnp.concatenate([jnp.ones((1, B_NOPE + B_ROPE), F32), jnp.zeros((1, copy), F32)], axis=1)
    return _with_ctx_rows(cos_l, sin_l, ctx_cos, ctx_len, LOG2E * (B_NOPE + B_ROPE) ** -0.5)


def _deinterleave(n):
    return jnp.concatenate([jnp.arange(0, n, 2), jnp.arange(1, n, 2)])


def _weights_a(wqkv):
    d = wqkv.shape[0]
    kw = A_KV_HEADS * HEAD_DIM
    qw = wqkv.shape[1] - 2 * kw
    perm = _deinterleave(HEAD_DIM)
    wq = wqkv[:, :qw].reshape(d, -1, HEAD_DIM)[:, :, perm].reshape(d, qw)
    wk = wqkv[:, qw:qw + kw].reshape(d, A_KV_HEADS, HEAD_DIM)[:, :, perm]
    wkk = jnp.concatenate([wk, wk], axis=2).reshape(d, 2 * kw)
    wv = wqkv[:, qw + kw:].reshape(d, A_KV_HEADS, HEAD_DIM)
    wv = jnp.concatenate([wv, jnp.zeros((d, A_KV_HEADS, LANES - HEAD_DIM), F32)], axis=2)
    wvt = wv.reshape(d, A_KV_HEADS * LANES).T
    return jnp.concatenate([wq, wkk], axis=1).astype(BF16), wvt.astype(BF16), qw, 2 * kw


def _weights_b(wdown, wuq, wuk, wuv, q_lora, kv_lora):
    d = wdown.shape[0]
    perm = _deinterleave(B_ROPE)
    assert LANES - B_NOPE - B_ROPE == B_ROPE
    w_rope = wdown[:, q_lora + kv_lora:][:, perm]
    w_rope = jnp.concatenate([jnp.zeros((d, B_NOPE), F32), w_rope, w_rope], axis=1)
    wd = jnp.concatenate([wdown[:, :q_lora + kv_lora], w_rope], axis=1)
    uq = wuq.reshape(q_lora, B_HEADS, B_NOPE + B_ROPE)
    uq_rope = uq[:, :, B_NOPE:][:, :, perm]
    uq = jnp.concatenate([uq[:, :, :B_NOPE], uq_rope, uq_rope], axis=2)
    uk = wuk.reshape(kv_lora, B_HEADS, B_NOPE)
    uk = jnp.concatenate([uk, jnp.zeros((kv_lora, B_HEADS, LANES - B_NOPE), F32)], axis=2)
    uv = wuv.reshape(kv_lora, B_HEADS, B_V)
    uv = jnp.concatenate([uv, jnp.zeros((kv_lora, B_HEADS, LANES - B_V), F32)], axis=2)
    flat = lambda w: w.reshape(w.shape[0], B_HEADS * LANES).astype(BF16)
    return wd.astype(BF16), flat(uq), flat(uk), flat(uv).T


def kernel(x, c, ctx, c_ctx, mod_w, mod_b, norm1_g, norm2_g, a_wqkv, a_wo, a_sink, b_wdown, b_qnorm_g, b_wuq,
           b_kvnorm_g, b_wuk, b_wuv, b_wo, f_win, f_conv_w, f_conv_b, f_wout, final_g):
    bsz, seq, d = x.shape
    ctx_len = ctx.shape[1]
    depth = mod_w.shape[0]
    assert ctx_len == TM and seq % TK_MLA == 0 and seq % TM_CHANNEL == 0 and seq % GRID_W == 0

    cond_rows = -(-(bsz + 1) // SUBLANES) * SUBLANES
    cond = jnp.concatenate([c, c_ctx[None], jnp.zeros((cond_rows - bsz - 1, d), F32)], axis=0)
    mod = _modulation(cond, mod_w, mod_b)

    tab_a = _tables_a(seq, ctx_len)
    tab_b = _tables_b(seq, ctx_len)
    x_ctx, x_lat = ctx, x
    w_in, w_out = f_win.astype(BF16), f_wout.astype(BF16)
    row = lambda v: v.reshape(1, -1)

    for i in range(depth):
        last = i == depth - 1
        lat = mod[i, :bsz].reshape(bsz, 6, d)
        cmod = jnp.broadcast_to(mod[i, bsz].reshape(1, 6, d), (bsz, 6, d))
        modl = jnp.stack([cmod, lat], axis=1)
        j = i // 2
        if i % 2 == 0:
            w, wvt, qw, kw = _weights_a(a_wqkv[j])
            q, k, vt = _proj_a(x_ctx, x_lat, modl, row(norm1_g[i]), w, wvt, tab_a, qw, kw)
            o = _attn_a(q, k, vt, a_sink[j], seq, not last)
            wo = a_wo[j]
        else:
            q_lora, kv_lora = b_qnorm_g.shape[1], b_kvnorm_g.shape[1]
            wd, wuq, wuk, wuvt = _weights_b(b_wdown[j], b_wuq[j], b_wuk[j], b_wuv[j], q_lora, kv_lora)
            q, k, vt = _proj_b(x_ctx, x_lat, modl, row(norm1_g[i]), wd, row(b_qnorm_g[j]), row(b_kvnorm_g[j]),
                               wuq, wuk, wuvt, tab_b)
            o = _attn_b(q, k, vt, seq, not last)
            wo = b_wo[j]
        x_ctx, x_lat = _channel(x_ctx, x_lat, o, modl, row(norm2_g[i]), wo.astype(BF16), w_in, f_conv_w[i],
                                row(f_conv_b[i]), w_out, row(final_g), i, last)
    return x_lat
```
